```python
import jax
import jax.numpy as jnp
from jax import lax
import numpy as np

D_MODEL = 2048
BATCH = 2
SEQ = 4096
DEPTH = 1
DEC_BATCH = 128
DEC_SEQ = 8
PAST_LEN = 16384
PAGE_SIZE = 128

WINDOW = 128
ATT_HEADS = 16
ATT_KV_HEADS = 4
ATT_HEAD_DIM = 64
ATT_GROUP = ATT_HEADS // ATT_KV_HEADS
ATT_Q = ATT_HEADS * ATT_HEAD_DIM
ATT_KV = ATT_KV_HEADS * ATT_HEAD_DIM
RET_HEADS = 8
RET_DK = 128
RET_DV = 256
RET_CHUNK = 128
RET_QK = RET_HEADS * RET_DK
RET_V = RET_HEADS * RET_DV
ROPE_BASE = 10000.0
D_FF = 5632
NORM_EPS = 1e-6
N_SUBLAYERS = 3
IN_SPLITS = (ATT_Q, ATT_KV, ATT_KV, RET_QK, RET_QK, RET_V, RET_V, D_MODEL, D_MODEL)
D_IN = ATT_Q + 2 * ATT_KV + 2 * RET_QK + 2 * RET_V + 2 * D_MODEL

kernel_name = 'hybrid_swa_retention_macaron_step'


def rms_norm(x, gain=None):
    xf = x.astype(jnp.float32)
    y = xf * lax.rsqrt(jnp.mean(xf * xf, axis=-1, keepdims=True) + NORM_EPS)
    if gain is not None:
        y = y * gain.astype(jnp.float32)
    return y.astype(x.dtype)


def swiglu(h, wg, wu, wd):
    return (jax.nn.silu(h @ wg) * (h @ wu)) @ wd


def split_cols(proj):
    parts = []
    start = 0
    for width in IN_SPLITS:
        parts.append(proj[..., start:start + width])
        start += width
    return parts


def rotate(x, pos):
    half = x.shape[-1] // 2
    inv_freq = ROPE_BASE ** (-jnp.linspace(0.0, 1.0, half, dtype=jnp.float32))
    ang = pos[:, None] * inv_freq[None, :]
    cos = jnp.cos(ang)[None, :, None, :]
    sin = jnp.sin(ang)[None, :, None, :]
    x1, x2 = x[..., :half], x[..., half:]
    return jnp.concatenate([x1 * cos - x2 * sin, x1 * sin + x2 * cos], axis=-1)


def alibi_slopes():
    return 2.0 ** (-8.0 * jnp.arange(1, ATT_HEADS + 1, dtype=jnp.float32) / ATT_HEADS)


def window_attend(q, k, v, kvalid, q_off, sinks):
    n, tq = q.shape[0], q.shape[1]
    tk = k.shape[1]
    qg = q.astype(jnp.float32).reshape(n, tq, ATT_KV_HEADS, ATT_GROUP, ATT_HEAD_DIM)
    s = jnp.einsum('nqkgd,nskd->nkgqs', qg, k.astype(jnp.float32)) * (ATT_HEAD_DIM ** -0.5)
    dist = q_off + jnp.arange(tq)[:, None] - jnp.arange(tk)[None, :]
    slopes = alibi_slopes().reshape(ATT_KV_HEADS, ATT_GROUP, 1, 1)
    s = s - slopes * dist.astype(jnp.float32)
    mask = ((dist >= 0) & (dist <= WINDOW))[None, None, None] & kvalid[:, None, None, None, :]
    s = jnp.where(mask, s, -jnp.inf)
    sink = sinks.astype(jnp.float32).reshape(1, ATT_KV_HEADS, ATT_GROUP, 1, 1)
    m = jnp.maximum(jnp.max(s, axis=-1, keepdims=True), sink)
    p = jnp.exp(s - m)
    p = p / (jnp.sum(p, axis=-1, keepdims=True) + jnp.exp(sink - m))
    o = jnp.einsum('nkgqs,nskd->nqkgd', p, v.astype(jnp.float32))
    return o.reshape(n, tq, ATT_Q).astype(q.dtype)


def banded_prompt_attention(q, k, v, sinks):
    b, s_len = q.shape[0], q.shape[1]
    nb = s_len // WINDOW
    pad = jnp.zeros((b, WINDOW) + k.shape[2:], k.dtype)
    kb = jnp.concatenate([pad, k], axis=1).reshape(b, nb + 1, WINDOW, ATT_KV_HEADS, ATT_HEAD_DIM)
    vb = jnp.concatenate([pad, v], axis=1).reshape(b, nb + 1, WINDOW, ATT_KV_HEADS, ATT_HEAD_DIM)
    kblk = jnp.concatenate([kb[:, :-1], kb[:, 1:]], axis=2).reshape(b * nb, 2 * WINDOW, ATT_KV_HEADS, ATT_HEAD_DIM)
    vblk = jnp.concatenate([vb[:, :-1], vb[:, 1:]], axis=2).reshape(b * nb, 2 * WINDOW, ATT_KV_HEADS, ATT_HEAD_DIM)
    qblk = q.reshape(b * nb, WINDOW, ATT_HEADS, ATT_HEAD_DIM)
    key_pos = jnp.arange(nb)[:, None] * WINDOW - WINDOW + jnp.arange(2 * WINDOW)[None, :]
    kvalid = jnp.broadcast_to((key_pos >= 0)[None], (b, nb, 2 * WINDOW)).reshape(b * nb, 2 * WINDOW)
    o = window_attend(qblk, kblk, vblk, kvalid, WINDOW, sinks)
    return o.reshape(b, s_len, ATT_Q)


def retention(q, k, v, s0, chunk):
    n, t, h, dk = q.shape
    dv = v.shape[-1]
    nc = t // chunk
    log_g = jnp.log(1.0 - 2.0 ** (-5.0 - jnp.arange(h, dtype=jnp.float32)))
    idx = jnp.arange(chunk, dtype=jnp.float32)
    diff = idx[:, None] - idx[None, :]
    decay = jnp.where(diff >= 0, jnp.exp(jnp.maximum(diff, 0.0)[None] * log_g[:, None, None]), 0.0)
    qc = q.reshape(n, nc, chunk, h, dk)
    kc = k.reshape(n, nc, chunk, h, dk)
    vc = v.reshape(n, nc, chunk, h, dv)
    scores = jnp.einsum('ncihd,ncjhd->nchij', qc, kc) * decay
    y_in = jnp.einsum('nchij,ncjhe->ncihe', scores, vc)
    k_w = jnp.exp((chunk - 1.0 - idx)[:, None] * log_g[None, :])
    kv = jnp.einsum('ncjhd,ncjhe->nchde', kc * k_w[:, :, None], vc)
    g_chunk = jnp.exp(chunk * log_g)[:, None, None]

    def step(s, kv_i):
        return g_chunk * s + kv_i, s

    s_fin, s_prev = lax.scan(step, s0, jnp.moveaxis(kv, 1, 0))
    q_w = jnp.exp((idx + 1.0)[:, None] * log_g[None, :])
    y_cross = jnp.einsum('ncihd,cnhde->ncihe', qc * q_w[:, :, None], s_prev)
    return (y_in + y_cross).reshape(n, t, h, dv), s_fin


def decoder_layer(x, c, pos, k_past, v_past, s_past, w_ada, b_ada, norm_pre, norm_post, w_in, sinks,
                  w_pa, w_pr, w_o, f1g, f1u, f1d, f2g, f2u, f2d):
    n, t, _ = x.shape
    mod = (jax.nn.silu(c) @ w_ada + b_ada).reshape(n, N_SUBLAYERS, 3, D_MODEL)
    shift, scale, gate = mod[:, :, 0], mod[:, :, 1], mod[:, :, 2]

    def pre(i, y):
        return rms_norm(y, norm_pre[i]) * (1.0 + scale[:, i, None]) + shift[:, i, None]

    def post(i, y):
        return gate[:, i, None] * rms_norm(y, norm_post[i])

    x = x + 0.5 * post(0, swiglu(pre(0, x), f1g, f1u, f1d))

    h = pre(1, x)
    qa, ka, va, qr, kr, vr, gr, gate_a, gate_r = split_cols(h @ w_in)
    qa = qa.reshape(n, t, ATT_HEADS, ATT_HEAD_DIM)
    ka = ka.reshape(n, t, ATT_KV_HEADS, ATT_HEAD_DIM)
    va = va.reshape(n, t, ATT_KV_HEADS, ATT_HEAD_DIM)
    if k_past is None:
        o_a = banded_prompt_attention(qa, ka, va, sinks)
        k_new, v_new = ka[:, -WINDOW:], va[:, -WINDOW:]
        s0 = jnp.zeros((n, RET_HEADS, RET_DK, RET_DV), jnp.float32)
        chunk = RET_CHUNK
    else:
        kcat = jnp.concatenate([k_past.astype(ka.dtype), ka], axis=1)
        vcat = jnp.concatenate([v_past.astype(va.dtype), va], axis=1)
        o_a = window_attend(qa, kcat, vcat, jnp.ones((n, kcat.shape[1]), bool), WINDOW, sinks)
        k_new, v_new = kcat[:, -WINDOW:], vcat[:, -WINDOW:]
        s0 = s_past.astype(jnp.float32)
        chunk = t
    qr = rotate(qr.reshape(n, t, RET_HEADS, RET_DK).astype(jnp.float32), pos)
    kr = rotate(kr.reshape(n, t, RET_HEADS, RET_DK).astype(jnp.float32), pos) * (RET_DK ** -0.5)
    vr = vr.reshape(n, t, RET_HEADS, RET_DV).astype(jnp.float32)
    o_r, s_new = retention(qr, kr, vr, s0, chunk)
    o_r = rms_norm(o_r).reshape(n, t, RET_V).astype(h.dtype)
    o_r = jax.nn.silu(gr) * o_r
    merged = jax.nn.sigmoid(gate_a) * (o_a @ w_pa) + jax.nn.sigmoid(gate_r) * (o_r @ w_pr)
    x = x + post(1, merged @ w_o)

    x = x + 0.5 * post(2, swiglu(pre(2, x), f2g, f2u, f2d))
    return x, k_new, v_new, s_new.astype(x.dtype)


def setup_inputs(seed: int = 0) -> dict:
    key = jax.random.key(seed)
    ks = jax.random.split(key, 22)
    f32 = jnp.float32
    nrm = lambda k, shape, s: jax.random.normal(k, shape, f32) * s
    return {
        'x_prompt': nrm(ks[0], (BATCH, SEQ, D_MODEL), 1.0),
        'x_sample': nrm(ks[1], (DEC_BATCH, DEC_SEQ, D_MODEL), 1.0),
        'cache_k_win': nrm(ks[2], (DEPTH, DEC_BATCH, WINDOW, ATT_KV_HEADS, ATT_HEAD_DIM), 1.0),
        'cache_v_win': nrm(ks[3], (DEPTH, DEC_BATCH, WINDOW, ATT_KV_HEADS, ATT_HEAD_DIM), 1.0),
        'state_ret': nrm(ks[4], (DEPTH, DEC_BATCH, RET_HEADS, RET_DK, RET_DV), 0.5),
        'c_prompt': nrm(ks[5], (BATCH, D_MODEL), 1.0),
        'c_sample': nrm(ks[6], (DEC_BATCH, D_MODEL), 1.0),
        'w_ada': nrm(ks[7], (DEPTH, D_MODEL, N_SUBLAYERS * 3 * D_MODEL), 0.5 * D_MODEL ** -0.5),
        'b_ada': nrm(ks[8], (DEPTH, N_SUBLAYERS * 3 * D_MODEL), 0.02),
        'norm_pre': 1.0 + nrm(ks[9], (DEPTH, N_SUBLAYERS, D_MODEL), 0.05),
        'norm_post': 1.0 + nrm(ks[10], (DEPTH, N_SUBLAYERS, D_MODEL), 0.05),
        'w_in': nrm(ks[11], (DEPTH, D_MODEL, D_IN), D_MODEL ** -0.5),
        'attn_sinks': nrm(ks[12], (DEPTH, ATT_HEADS), 0.5),
        'w_pa': nrm(ks[13], (DEPTH, ATT_Q, D_MODEL), ATT_Q ** -0.5),
        'w_pr': nrm(ks[14], (DEPTH, RET_V, D_MODEL), RET_V ** -0.5),
        'w_o': nrm(ks[15], (DEPTH, D_MODEL, D_MODEL), D_MODEL ** -0.5),
        'ffn1_gate': nrm(ks[16], (DEPTH, D_MODEL, D_FF), D_MODEL ** -0.5),
        'ffn1_up': nrm(ks[17], (DEPTH, D_MODEL, D_FF), D_MODEL ** -0.5),
        'ffn1_down': nrm(ks[18], (DEPTH, D_FF, D_MODEL), D_FF ** -0.5),
        'ffn2_gate': nrm(ks[19], (DEPTH, D_MODEL, D_FF), D_MODEL ** -0.5),
        'ffn2_up': nrm(ks[20], (DEPTH, D_MODEL, D_FF), D_MODEL ** -0.5),
        'ffn2_down': nrm(ks[21], (DEPTH, D_FF, D_MODEL), D_FF ** -0.5),
    }


def reference(x_prompt, x_sample, cache_k_win, cache_v_win, state_ret, c_prompt, c_sample,
              w_ada, b_ada, norm_pre, norm_post, w_in, attn_sinks, w_pa, w_pr, w_o,
              ffn1_gate, ffn1_up, ffn1_down, ffn2_gate, ffn2_up, ffn2_down):
    pos_p = jnp.arange(x_prompt.shape[1], dtype=jnp.float32)
    pos_s = jnp.arange(x_sample.shape[1], dtype=jnp.float32) + PAST_LEN
    yp, ys = x_prompt, x_sample
    kp_l, vp_l, sp_l, ks_l, vs_l, ss_l = [], [], [], [], [], []
    for l in range(DEPTH):
        w = (w_ada[l], b_ada[l], norm_pre[l], norm_post[l], w_in[l], attn_sinks[l], w_pa[l], w_pr[l],
             w_o[l], ffn1_gate[l], ffn1_up[l], ffn1_down[l], ffn2_gate[l], ffn2_up[l], ffn2_down[l])
        yp, kp, vp, sp = decoder_layer(yp, c_prompt, pos_p, None, None, None, *w)
        ys, kss, vss, sss = decoder_layer(ys, c_sample, pos_s, cache_k_win[l], cache_v_win[l], state_ret[l], *w)
        kp_l.append(kp)
        vp_l.append(vp)
        sp_l.append(sp)
        ks_l.append(kss)
        vs_l.append(vss)
        ss_l.append(sss)
    k_win_prompt = jnp.stack(kp_l)
    v_win_prompt = jnp.stack(vp_l)
    state_ret_prompt = jnp.stack(sp_l)
    k_win_sample = jnp.stack(ks_l)
    v_win_sample = jnp.stack(vs_l)
    state_ret_sample = jnp.stack(ss_l)
    return (yp, ys, k_win_prompt, v_win_prompt, state_ret_prompt, k_win_sample, v_win_sample, state_ret_sample)
```

```python
import functools
import math
from typing import NamedTuple

import jax
import jax.numpy as jnp
import numpy as np
from jax import lax
from jax.experimental import pallas as pl
from jax.experimental.pallas import tpu as pltpu

F32 = jnp.float32
BF16 = jnp.bfloat16

D_MODEL = 2048
WINDOW = 128
ATT_HEADS = 16
ATT_KV_HEADS = 4
ATT_HEAD_DIM = 64
ATT_GROUP = ATT_HEADS // ATT_KV_HEADS
ATT_Q = ATT_HEADS * ATT_HEAD_DIM
ATT_KV = ATT_KV_HEADS * ATT_HEAD_DIM
RET_HEADS = 8
RET_DK = 128
RET_DV = 256
RET_CHUNK = 128
RET_QK = RET_HEADS * RET_DK
RET_V = RET_HEADS * RET_DV
ROPE_BASE = 10000.0
D_FF = 5632
NORM_EPS = 1e-6
N_SUBLAYERS = 3
PAST_LEN = 16384
D_IN = ATT_Q + 2 * ATT_KV + 2 * RET_QK + 2 * RET_V + 2 * D_MODEL

V7X_SUBLANES = 8
V7X_BF16_ROWS = 16
V7X_VMEM_LIMIT_BYTES = 56 * 1024 * 1024

TOKEN_TILE = 512
FF_TILE = 512
COL_TILE = 512
ADA_COL_TILE = 1024
MOD_PAD_ROWS = 8

_QA_TILES = ATT_Q // COL_TILE
_KVA_TILES = 2 * ATT_KV // COL_TILE
_QKR_TILES = 2 * RET_QK // COL_TILE
_WIDE_COLS = 2 * RET_V + 2 * D_MODEL
_WIDE_TILES = _WIDE_COLS // COL_TILE
_QKR_START = _QA_TILES + _KVA_TILES
_WIDE_START = _QKR_START + _QKR_TILES
assert _WIDE_START + _WIDE_TILES == D_IN // COL_TILE and _KVA_TILES == 1

_ALIBI_SLOPES = [2.0 ** (-8.0 * (h + 1) / ATT_HEADS) for h in range(ATT_HEADS)]
_RET_LOG_G = [math.log(1.0 - 2.0 ** (-5.0 - h)) for h in range(RET_HEADS)]
_ATT_SCALE = ATT_HEAD_DIM ** -0.5
_RET_K_SCALE = RET_DK ** -0.5


class _Stream(NamedTuple):
    n_tokens: int
    rows_per_mod: int
    mod_block_rows: int
    mod_block_base: int

    @property
    def sub_rows(self):
        return min(self.rows_per_mod, TOKEN_TILE)

    @property
    def group_rows(self):
        return max(self.sub_rows, V7X_BF16_ROWS)

    @property
    def n_groups(self):
        return TOKEN_TILE // self.group_rows

    @property
    def mods_per_group(self):
        return self.group_rows // self.sub_rows


def _rms(x):
    return x * lax.rsqrt(jnp.mean(x * x, axis=-1, keepdims=True) + NORM_EPS)


def _silu(x):
    return x * jax.nn.sigmoid(x)


def _for_groups(n_groups, fn):
    if n_groups == 1:
        fn(0)
    else:
        def body(g, carry):
            fn(g)
            return carry
        lax.fori_loop(0, n_groups, body, 0)


def _group_base(st, g):
    return 0 if st.n_groups == 1 else pl.multiple_of(g * st.group_rows, st.group_rows)


def _mod_row(st, i, g, s):
    if st.rows_per_mod >= TOKEN_TILE:
        return (i * TOKEN_TILE) // st.rows_per_mod
    return g * st.mods_per_group + s


def _pre_norm(st, i, x_ref, shift_ref, scale_ref, gain, h_ref):
    def group(g):
        base = _group_base(st, g)
        parts = []
        for s in range(st.mods_per_group):
            rows = pl.ds(base + s * st.sub_rows, st.sub_rows)
            m = _mod_row(st, i, g, s)
            x = x_ref[rows, :]
            sc = scale_ref[pl.ds(m, 1), :]
            sh = shift_ref[pl.ds(m, 1), :]
            parts.append(_rms(x) * gain * (1.0 + sc) + sh)
        h = parts[0] if len(parts) == 1 else jnp.concatenate(parts, axis=0)
        h_ref[pl.ds(base, st.group_rows), :] = h.astype(BF16)
    _for_groups(st.n_groups, group)


def _post_residual(st, i, x_ref, acc_ref, gate_ref, gain, o_ref, coeff):
    def group(g):
        base = _group_base(st, g)
        for s in range(st.mods_per_group):
            rows = pl.ds(base + s * st.sub_rows, st.sub_rows)
            m = _mod_row(st, i, g, s)
            gt = gate_ref[pl.ds(m, 1), :]
            y = gt * (_rms(acc_ref[rows, :]) * gain)
            if coeff != 1.0:
                y = coeff * y
            o_ref[rows, :] = x_ref[rows, :] + y
    _for_groups(st.n_groups, group)


def _ada_kernel(c_ref, w_ref, b_ref, o_ref):
    a = _silu(c_ref[...]).astype(BF16)
    o_ref[0] = jnp.dot(a, w_ref[...].astype(BF16), preferred_element_type=F32) + b_ref[...]


def _ada(c_all, w_ada, b_ada):
    rows = c_all.shape[0]
    n_vec = N_SUBLAYERS * 3
    per_vec = D_MODEL // ADA_COL_TILE
    return pl.pallas_call(
        _ada_kernel,
        out_shape=jax.ShapeDtypeStruct((n_vec, rows, D_MODEL), F32),
        grid=(n_vec * per_vec,),
        in_specs=[
            pl.BlockSpec((rows, D_MODEL), lambda j: (0, 0)),
            pl.BlockSpec((D_MODEL, ADA_COL_TILE), lambda j: (0, j)),
            pl.BlockSpec((1, ADA_COL_TILE), lambda j: (0, j)),
        ],
        out_specs=pl.BlockSpec((1, rows, ADA_COL_TILE), lambda j: (j // per_vec, 0, j % per_vec)),
        compiler_params=pltpu.CompilerParams(
            dimension_semantics=("arbitrary",), vmem_limit_bytes=V7X_VMEM_LIMIT_BYTES),
        name="ada_mod",
    )(c_all, w_ada, b_ada.reshape(1, -1))


def _mod_spec(st, vec):
    if st.rows_per_mod >= TOKEN_TILE:
        index = lambda i, j: (vec, st.mod_block_base, 0)
    else:
        index = lambda i, j: (vec, st.mod_block_base + i, 0)
    return pl.BlockSpec((None, st.mod_block_rows, D_MODEL), index)


def _ffn_kernel(st, sub, x_ref, shift_ref, scale_ref, gate_ref, npre_ref, npost_ref,
                wg_ref, wu_ref, wd_ref, o_ref, h_ref, acc_ref):
    i = pl.program_id(0)
    j = pl.program_id(1)

    @pl.when(j == 0)
    def _():
        _pre_norm(st, i, x_ref, shift_ref, scale_ref, npre_ref[sub:sub + 1, :], h_ref)
        acc_ref[...] = jnp.zeros_like(acc_ref)

    h = h_ref[...]
    g = jnp.dot(h, wg_ref[...], preferred_element_type=F32)
    u = jnp.dot(h, wu_ref[...], preferred_element_type=F32)
    a = (_silu(g) * u).astype(BF16)
    acc_ref[...] += jnp.dot(a, wd_ref[...], preferred_element_type=F32)

    @pl.when(j == pl.num_programs(1) - 1)
    def _():
        _post_residual(st, i, x_ref, acc_ref, gate_ref, npost_ref[sub:sub + 1, :], o_ref, 0.5)


def _ffn(st, sub, x, mod, npre, npost, wg, wu, wd):
    n_tiles = st.n_tokens // TOKEN_TILE
    row_spec = pl.BlockSpec((TOKEN_TILE, D_MODEL), lambda i, j: (i, 0))
    full_spec = pl.BlockSpec((N_SUBLAYERS, D_MODEL), lambda i, j: (0, 0))
    return pl.pallas_call(
        functools.partial(_ffn_kernel, st, sub),
        out_shape=jax.ShapeDtypeStruct((st.n_tokens, D_MODEL), F32),
        grid=(n_tiles, D_FF // FF_TILE),
        in_specs=[
            row_spec,
            _mod_spec(st, 3 * sub + 0), _mod_spec(st, 3 * sub + 1), _mod_spec(st, 3 * sub + 2),
            full_spec, full_spec,
            pl.BlockSpec((D_MODEL, FF_TILE), lambda i, j: (0, j)),
            pl.BlockSpec((D_MODEL, FF_TILE), lambda i, j: (0, j)),
            pl.BlockSpec((FF_TILE, D_MODEL), lambda i, j: (j, 0)),
        ],
        out_specs=row_spec,
        scratch_shapes=[pltpu.VMEM((TOKEN_TILE, D_MODEL), BF16), pltpu.VMEM((TOKEN_TILE, D_MODEL), F32)],
        compiler_params=pltpu.CompilerParams(
            dimension_semantics=("parallel", "arbitrary"), vmem_limit_bytes=V7X_VMEM_LIMIT_BYTES),
        name=f"ffn{sub}",
    )(x, mod, mod, mod, npre, npost, wg, wu, wd)


def _proj_kernel(st, x_ref, shift_ref, scale_ref, npre_ref, w_ref,
                 qa_ref, kva_ref, qkr_ref, wide_ref, h_ref):
    i = pl.program_id(0)
    j = pl.program_id(1)

    @pl.when(j == 0)
    def _():
        _pre_norm(st, i, x_ref, shift_ref, scale_ref, npre_ref[1:2, :], h_ref)

    y = jnp.dot(h_ref[...], w_ref[...], preferred_element_type=F32)

    @pl.when(j < _QA_TILES)
    def _():
        qa_ref[...] = y.astype(qa_ref.dtype)

    @pl.when(j == _QA_TILES)
    def _():
        kva_ref[...] = y

    @pl.when((j >= _QKR_START) & (j < _WIDE_START))
    def _():
        qkr_ref[...] = y

    @pl.when(j >= _WIDE_START)
    def _():
        wide_ref[...] = y.astype(wide_ref.dtype)


def _proj(st, x, mod, npre, w_in, narrow_dtype):
    n_tiles = st.n_tokens // TOKEN_TILE
    n = st.n_tokens
    row_spec = pl.BlockSpec((TOKEN_TILE, D_MODEL), lambda i, j: (i, 0))
    out_block = (TOKEN_TILE, COL_TILE)
    return pl.pallas_call(
        functools.partial(_proj_kernel, st),
        out_shape=(
            jax.ShapeDtypeStruct((n, ATT_Q), narrow_dtype),
            jax.ShapeDtypeStruct((n, 2 * ATT_KV), F32),
            jax.ShapeDtypeStruct((n, 2 * RET_QK), F32),
            jax.ShapeDtypeStruct((n, _WIDE_COLS), narrow_dtype),
        ),
        grid=(n_tiles, D_IN // COL_TILE),
        in_specs=[
            row_spec,
            _mod_spec(st, 3), _mod_spec(st, 4),
            pl.BlockSpec((N_SUBLAYERS, D_MODEL), lambda i, j: (0, 0)),
            pl.BlockSpec((D_MODEL, COL_TILE), lambda i, j: (0, j)),
        ],
        out_specs=(
            pl.BlockSpec(out_block, lambda i, j: (i, jnp.minimum(j, _QA_TILES - 1))),
            pl.BlockSpec(out_block, lambda i, j: (i, 0)),
            pl.BlockSpec(out_block, lambda i, j: (i, jnp.clip(j - _QKR_START, 0, _QKR_TILES - 1))),
            pl.BlockSpec(out_block, lambda i, j: (i, jnp.maximum(j - _WIDE_START, 0))),
        ),
        scratch_shapes=[pltpu.VMEM((TOKEN_TILE, D_MODEL), BF16)],
        compiler_params=pltpu.CompilerParams(
            dimension_semantics=("parallel", "arbitrary"), vmem_limit_bytes=V7X_VMEM_LIMIT_BYTES),
        name="in_proj",
    )(x, mod, mod, npre, w_in)


def _attention(q, k2, v2, sinks_ref, first_valid_key):
    tq = q.shape[0]
    a_idx = lax.broadcasted_iota(jnp.int32, (tq, 2 * WINDOW), 0)
    b_idx = lax.broadcasted_iota(jnp.int32, (tq, 2 * WINDOW), 1)
    dist = WINDOW + a_idx - b_idx
    mask = (dist >= 0) & (dist <= WINDOW) & (b_idx >= first_valid_key)
    dist_f = dist.astype(F32)
    outs = []
    for kv in range(ATT_KV_HEADS):
        cols = slice(kv * ATT_HEAD_DIM, (kv + 1) * ATT_HEAD_DIM)
        kk = k2[:, cols]
        vv = v2[:, cols]
        heads = range(kv * ATT_GROUP, (kv + 1) * ATT_GROUP)
        qg = jnp.concatenate(
            [q[:, h * ATT_HEAD_DIM:(h + 1) * ATT_HEAD_DIM] for h in heads], axis=0).astype(BF16)
        s_all = lax.dot_general(qg, kk, (((1,), (1,)), ((), ())), preferred_element_type=F32)
        probs = []
        for g, h in enumerate(heads):
            s = s_all[g * tq:(g + 1) * tq] * _ATT_SCALE - _ALIBI_SLOPES[h] * dist_f
            s = jnp.where(mask, s, -jnp.inf)
            sink = sinks_ref[h]
            m = jnp.maximum(jnp.max(s, axis=-1, keepdims=True), sink)
            p = jnp.exp(s - m)
            inv = 1.0 / (jnp.sum(p, axis=-1, keepdims=True) + jnp.exp(sink - m))
            probs.append((p * inv).astype(BF16))
        o_all = jnp.dot(jnp.concatenate(probs, axis=0), vv, preferred_element_type=F32)
        outs.extend(o_all[g * tq:(g + 1) * tq] for g in range(ATT_GROUP))
    return jnp.concatenate(outs, axis=-1)


def _attn_prompt_kernel(sinks_ref, q_ref, kc_ref, vc_ref, kp_ref, vp_ref, o_ref):
    blk = pl.program_id(1)
    k2 = jnp.concatenate([kp_ref[...], kc_ref[...]], axis=0).astype(BF16)
    v2 = jnp.concatenate([vp_ref[...], vc_ref[...]], axis=0).astype(BF16)
    first_valid = jnp.where(blk == 0, WINDOW, 0)
    o_ref[...] = _attention(q_ref[...], k2, v2, sinks_ref, first_valid).astype(o_ref.dtype)


def _attn_prompt(qa, kva, sinks, batch, seq):
    nb = seq // WINDOW
    cur = lambda col: (lambda b, i: (b * nb + i, col))
    prev = lambda col: (lambda b, i: (b * nb + jnp.maximum(i - 1, 0), col))
    return pl.pallas_call(
        _attn_prompt_kernel,
        out_shape=jax.ShapeDtypeStruct((batch * seq, ATT_Q), BF16),
        grid=(batch, nb),
        in_specs=[
            pl.BlockSpec(memory_space=pltpu.SMEM),
            pl.BlockSpec((WINDOW, ATT_Q), lambda b, i: (b * nb + i, 0)),
            pl.BlockSpec((WINDOW, ATT_KV), cur(0)),
            pl.BlockSpec((WINDOW, ATT_KV), cur(1)),
            pl.BlockSpec((WINDOW, ATT_KV), prev(0)),
            pl.BlockSpec((WINDOW, ATT_KV), prev(1)),
        ],
        out_specs=pl.BlockSpec((WINDOW, ATT_Q), lambda b, i: (b * nb + i, 0)),
        compiler_params=pltpu.CompilerParams(
            dimension_semantics=("parallel", "arbitrary"), vmem_limit_bytes=V7X_VMEM_LIMIT_BYTES),
        name="attn_prompt",
    )(sinks, qa, kva, kva, kva, kva)


def _rotate(x, cos2, sin2):
    return x * cos2 + pltpu.roll(x, RET_DK // 2, axis=1) * sin2


def _retention_head(qh, kh, vh, s_prev, log_g, chunk_len):
    c = qh.shape[0]
    pad = RET_CHUNK - c
    if pad:
        kh = jnp.concatenate([kh, jnp.zeros((pad, RET_DK), F32)], axis=0)
        vh = jnp.concatenate([vh, jnp.zeros((pad, RET_DV), vh.dtype)], axis=0)
    vb = vh.astype(BF16)
    row = lax.broadcasted_iota(jnp.int32, (c, RET_CHUNK), 0)
    col = lax.broadcasted_iota(jnp.int32, (c, RET_CHUNK), 1)
    diff = (row - col).astype(F32)
    decay = jnp.where(diff >= 0, jnp.exp(jnp.maximum(diff, 0.0) * log_g), 0.0)
    scores = lax.dot_general(qh.astype(BF16), kh.astype(BF16), (((1,), (1,)), ((), ())),
                             preferred_element_type=F32) * decay
    y = jnp.dot(scores.astype(BF16), vb, preferred_element_type=F32)
    q_w = jnp.exp((row.astype(F32) + 1.0) * log_g)
    y = y + jnp.dot((qh * q_w).astype(BF16), s_prev.astype(BF16), preferred_element_type=F32)
    k_row = lax.broadcasted_iota(jnp.int32, (RET_CHUNK, RET_DK), 0).astype(F32)
    k_w = jnp.exp((chunk_len - 1.0 - k_row) * log_g)
    kt = (kh * k_w).T.astype(BF16)
    s_new = math.exp(chunk_len * log_g) * s_prev + jnp.dot(kt, vb, preferred_element_type=F32)
    return y, s_new


def _retention(q_ref, k_ref, v_ref, gr_ref, cos2, sin2, read_state, write_state, o_ref, chunk_len):
    for h in range(RET_HEADS):
        qk_cols = slice(h * RET_DK, (h + 1) * RET_DK)
        v_cols = slice(h * RET_DV, (h + 1) * RET_DV)
        qh = _rotate(q_ref[:, qk_cols], cos2, sin2)
        kh = _rotate(k_ref[:, qk_cols], cos2, sin2) * _RET_K_SCALE
        y, s_new = _retention_head(qh, kh, v_ref[:, v_cols], read_state(h), _RET_LOG_G[h], chunk_len)
        write_state(h, s_new)
        gate = gr_ref[:, v_cols].astype(F32)
        o_ref[:, v_cols] = (_silu(gate) * _rms(y)).astype(o_ref.dtype)


def _ret_prompt_kernel(q_ref, k_ref, v_ref, gr_ref, cos_ref, sin_ref, o_ref, s_ref):
    @pl.when(pl.program_id(1) == 0)
    def _():
        s_ref[...] = jnp.zeros_like(s_ref)

    def write_state(h, s):
        s_ref[0, h] = s

    _retention(q_ref, k_ref, v_ref, gr_ref, cos_ref[...], sin_ref[...],
               lambda h: s_ref[0, h], write_state, o_ref, float(RET_CHUNK))


def _ret_prompt(qkr, wide, cos2, sin2, batch, seq):
    nc = seq // RET_CHUNK
    rows = lambda col: (lambda b, c: (b * nc + c, col))
    return pl.pallas_call(
        _ret_prompt_kernel,
        out_shape=(
            jax.ShapeDtypeStruct((batch * seq, RET_V), BF16),
            jax.ShapeDtypeStruct((batch, RET_HEADS, RET_DK, RET_DV), F32),
        ),
        grid=(batch, nc),
        in_specs=[
            pl.BlockSpec((RET_CHUNK, RET_QK), rows(0)),
            pl.BlockSpec((RET_CHUNK, RET_QK), rows(1)),
            pl.BlockSpec((RET_CHUNK, RET_V), rows(0)),
            pl.BlockSpec((RET_CHUNK, RET_V), rows(1)),
            pl.BlockSpec((RET_CHUNK, RET_DK), lambda b, c: (c, 0)),
            pl.BlockSpec((RET_CHUNK, RET_DK), lambda b, c: (c, 0)),
        ],
        out_specs=(
            pl.BlockSpec((RET_CHUNK, RET_V), rows(0)),
            pl.BlockSpec((1, RET_HEADS, RET_DK, RET_DV), lambda b, c: (b, 0, 0, 0)),
        ),
        compiler_params=pltpu.CompilerParams(
            dimension_semantics=("parallel", "arbitrary"), vmem_limit_bytes=V7X_VMEM_LIMIT_BYTES),
        name="ret_prompt",
    )(qkr, qkr, wide, wide, cos2, sin2)


def _mix_sample_kernel(t_new, sinks_ref, qa_ref, kn_ref, vn_ref, ck_ref, cv_ref,
                       q_ref, k_ref, v_ref, gr_ref, cos_ref, sin_ref, s_in_ref,
                       oa_ref, or_ref, ko_ref, vo_ref, s_out_ref):
    kc, vc = ck_ref[0], cv_ref[0]
    kn, vn = kn_ref[...], vn_ref[...]
    zeros = jnp.zeros((WINDOW - t_new, ATT_KV), F32)
    k2 = jnp.concatenate([kc, kn, zeros], axis=0).astype(BF16)
    v2 = jnp.concatenate([vc, vn, zeros], axis=0).astype(BF16)
    oa_ref[...] = _attention(qa_ref[...], k2, v2, sinks_ref, 0).astype(oa_ref.dtype)
    ko_ref[0, :WINDOW - t_new, :] = kc[t_new:]
    ko_ref[0, WINDOW - t_new:, :] = kn
    vo_ref[0, :WINDOW - t_new, :] = vc[t_new:]
    vo_ref[0, WINDOW - t_new:, :] = vn

    def write_state(h, s):
        s_out_ref[0, h] = s

    _retention(q_ref, k_ref, v_ref, gr_ref, cos_ref[...], sin_ref[...],
               lambda h: s_in_ref[0, h], write_state, or_ref, float(t_new))


def _mix_sample(qa, kva, qkr, wide, cache_k, cache_v, state, sinks, cos2, sin2, n_seq, t_new):
    rows = lambda col: (lambda n: (n, col))
    cache_spec = pl.BlockSpec((1, WINDOW, ATT_KV), lambda n: (n, 0, 0))
    state_spec = pl.BlockSpec((1, RET_HEADS, RET_DK, RET_DV), lambda n: (n, 0, 0, 0))
    table_spec = pl.BlockSpec((t_new, RET_DK), lambda n: (0, 0))
    return pl.pallas_call(
        functools.partial(_mix_sample_kernel, t_new),
        out_shape=(
            jax.ShapeDtypeStruct((n_seq * t_new, ATT_Q), F32),
            jax.ShapeDtypeStruct((n_seq * t_new, RET_V), F32),
            jax.ShapeDtypeStruct(cache_k.shape, F32),
            jax.ShapeDtypeStruct(cache_v.shape, F32),
            jax.ShapeDtypeStruct(state.shape, F32),
        ),
        grid=(n_seq,),
        in_specs=[
            pl.BlockSpec(memory_space=pltpu.SMEM),
            pl.BlockSpec((t_new, ATT_Q), rows(0)),
            pl.BlockSpec((t_new, ATT_KV), rows(0)),
            pl.BlockSpec((t_new, ATT_KV), rows(1)),
            cache_spec, cache_spec,
            pl.BlockSpec((t_new, RET_QK), rows(0)),
            pl.BlockSpec((t_new, RET_QK), rows(1)),
            pl.BlockSpec((t_new, RET_V), rows(0)),
            pl.BlockSpec((t_new, RET_V), rows(1)),
            table_spec, table_spec,
            state_spec,
        ],
        out_specs=(
            pl.BlockSpec((t_new, ATT_Q), rows(0)),
            pl.BlockSpec((t_new, RET_V), rows(0)),
            cache_spec, cache_spec, state_spec,
        ),
        compiler_params=pltpu.CompilerParams(
            dimension_semantics=("parallel",), vmem_limit_bytes=V7X_VMEM_LIMIT_BYTES),
        name="mix_sample",
    )(sinks, qa, kva, kva, cache_k, cache_v, qkr, qkr, wide, wide, cos2, sin2, state)


def _merge_kernel(st, x_ref, gate_ref, npost_ref, oa_ref, or_ref, ga_ref, gr_ref,
                  wpa_ref, wpr_ref, wo_ref, o_ref, acc_ref):
    i = pl.program_id(0)
    j = pl.program_id(1)

    @pl.when(j == 0)
    def _():
        acc_ref[...] = jnp.zeros_like(acc_ref)

    a = jnp.dot(oa_ref[...].astype(BF16), wpa_ref[...], preferred_element_type=F32)
    r = jnp.dot(or_ref[...].astype(BF16), wpr_ref[...], preferred_element_type=F32)
    merged = (jax.nn.sigmoid(ga_ref[...].astype(F32)) * a
              + jax.nn.sigmoid(gr_ref[...].astype(F32)) * r)
    acc_ref[...] += jnp.dot(merged.astype(BF16), wo_ref[...], preferred_element_type=F32)

    @pl.when(j == pl.num_programs(1) - 1)
    def _():
        _post_residual(st, i, x_ref, acc_ref, gate_ref, npost_ref[1:2, :], o_ref, 1.0)


def _merge(st, x, mod, npost, o_a, o_r, wide, w_pa, w_pr, w_o):
    n_tiles = st.n_tokens // TOKEN_TILE
    n_col = D_MODEL // COL_TILE
    gate_a_block = 2 * RET_V // COL_TILE
    gate_r_block = gate_a_block + n_col
    row_spec = pl.BlockSpec((TOKEN_TILE, D_MODEL), lambda i, j: (i, 0))
    return pl.pallas_call(
        functools.partial(_merge_kernel, st),
        out_shape=jax.ShapeDtypeStruct((st.n_tokens, D_MODEL), F32),
        grid=(n_tiles, n_col),
        in_specs=[
            row_spec,
            _mod_spec(st, 5),
            pl.BlockSpec((N_SUBLAYERS, D_MODEL), lambda i, j: (0, 0)),
            pl.BlockSpec((TOKEN_TILE, ATT_Q), lambda i, j: (i, 0)),
            pl.BlockSpec((TOKEN_TILE, RET_V), lambda i, j: (i, 0)),
            pl.BlockSpec((TOKEN_TILE, COL_TILE), lambda i, j: (i, gate_a_block + j)),
            pl.BlockSpec((TOKEN_TILE, COL_TILE), lambda i, j: (i, gate_r_block + j)),
            pl.BlockSpec((ATT_Q, COL_TILE), lambda i, j: (0, j)),
            pl.BlockSpec((RET_V, COL_TILE), lambda i, j: (0, j)),
            pl.BlockSpec((COL_TILE, D_MODEL), lambda i, j: (j, 0)),
        ],
        out_specs=row_spec,
        scratch_shapes=[pltpu.VMEM((TOKEN_TILE, D_MODEL), F32)],
        compiler_params=pltpu.CompilerParams(
            dimension_semantics=("parallel", "arbitrary"), vmem_limit_bytes=V7X_VMEM_LIMIT_BYTES),
        name="merge_out",
    )(x, mod, npost, o_a, o_r, wide, wide, w_pa, w_pr, w_o)


def _rotation_tables(pos):
    half = RET_DK // 2
    inv_freq = ROPE_BASE ** (-jnp.linspace(0.0, 1.0, half, dtype=F32))
    ang = pos[:, None] * inv_freq[None, :]
    cos, sin = jnp.cos(ang), jnp.sin(ang)
    return jnp.concatenate([cos, cos], axis=-1), jnp.concatenate([-sin, sin], axis=-1)


def kernel(x_prompt, x_sample, cache_k_win, cache_v_win, state_ret, c_prompt, c_sample, w_ada, b_ada,
           norm_pre, norm_post, w_in, attn_sinks, w_pa, w_pr, w_o,
           ffn1_gate, ffn1_up, ffn1_down, ffn2_gate, ffn2_up, ffn2_down):
    batch, seq, _ = x_prompt.shape
    n_seq, t_new, _ = x_sample.shape
    assert w_ada.shape[0] == 1, "single-layer step"
    assert t_new == V7X_SUBLANES and seq % TOKEN_TILE == 0 and (n_seq * t_new) % TOKEN_TILE == 0
    assert batch <= MOD_PAD_ROWS and n_seq % MOD_PAD_ROWS == 0

    c_all = jnp.concatenate(
        [c_sample, c_prompt, jnp.zeros((MOD_PAD_ROWS - batch, D_MODEL), F32)], axis=0)
    mod = _ada(c_all, w_ada[0], b_ada[0])

    prompt = _Stream(batch * seq, seq, MOD_PAD_ROWS, n_seq // MOD_PAD_ROWS)
    sample = _Stream(n_seq * t_new, t_new, TOKEN_TILE // t_new, 0)

    npre, npost = norm_pre[0], norm_post[0]
    sinks = attn_sinks[0]
    bf = lambda w: w[0].astype(BF16)
    w_in_b, w_pa_b, w_pr_b, w_o_b = bf(w_in), bf(w_pa), bf(w_pr), bf(w_o)
    f1g, f1u, f1d = bf(ffn1_gate), bf(ffn1_up), bf(ffn1_down)
    f2g, f2u, f2d = bf(ffn2_gate), bf(ffn2_up), bf(ffn2_down)

    xp = x_prompt.reshape(batch * seq, D_MODEL)
    xp = _ffn(prompt, 0, xp, mod, npre, npost, f1g, f1u, f1d)
    qa, kva, qkr, wide = _proj(prompt, xp, mod, npre, w_in_b, BF16)
    o_a = _attn_prompt(qa, kva, sinks, batch, seq)
    cos_p, sin_p = _rotation_tables(jnp.arange(seq, dtype=F32))
    o_r, state_p = _ret_prompt(qkr, wide, cos_p, sin_p, batch, seq)
    xp = _merge(prompt, xp, mod, npost, o_a, o_r, wide, w_pa_b, w_pr_b, w_o_b)
    xp = _ffn(prompt, 2, xp, mod, npre, npost, f2g, f2u, f2d)
    kva_p = kva.reshape(batch, seq, 2 * ATT_KV)[:, seq - WINDOW:]
    kv_shape = (1, batch, WINDOW, ATT_KV_HEADS, ATT_HEAD_DIM)
    k_win_p = kva_p[..., :ATT_KV].reshape(kv_shape)
    v_win_p = kva_p[..., ATT_KV:].reshape(kv_shape)

    xs = x_sample.reshape(n_seq * t_new, D_MODEL)
    xs = _ffn(sample, 0, xs, mod, npre, npost, f1g, f1u, f1d)
    qa_s, kva_s, qkr_s, wide_s = _proj(sample, xs, mod, npre, w_in_b, F32)
    cos_s, sin_s = _rotation_tables(jnp.arange(t_new, dtype=F32) + PAST_LEN)
    o_a_s, o_r_s, k_s, v_s, state_s = _mix_sample(
        qa_s, kva_s, qkr_s, wide_s,
        cache_k_win[0].reshape(n_seq, WINDOW, ATT_KV), cache_v_win[0].reshape(n_seq, WINDOW, ATT_KV),
        state_ret[0], sinks, cos_s, sin_s, n_seq, t_new)
    xs = _merge(sample, xs, mod, npost, o_a_s, o_r_s, wide_s, w_pa_b, w_pr_b, w_o_b)
    xs = _ffn(sample, 2, xs, mod, npre, npost, f2g, f2u, f2d)
    kvs_shape = (1, n_seq, WINDOW, ATT_KV_HEADS, ATT_HEAD_DIM)

    return (xp.reshape(batch, seq, D_MODEL), xs.reshape(n_seq, t_new, D_MODEL),
            k_win_p, v_win_p, state_p[None],
            k_s.reshape(kvs_shape), v_s.reshape(kvs_shape), state_s[None])
```

```python
import functools
import math
from typing import NamedTuple

import jax
import jax.numpy as jnp
import numpy as np
from jax import lax
from jax.experimental import pallas as pl
from jax.experimental.pallas import tpu as pltpu

F32 = jnp.float32
BF16 = jnp.bfloat16

D_MODEL = 2048
WINDOW = 128
ATT_HEADS = 16
ATT_KV_HEADS = 4
ATT_HEAD_DIM = 64
ATT_GROUP = ATT_HEADS // ATT_KV_HEADS
ATT_Q = ATT_HEADS * ATT_HEAD_DIM
ATT_KV = ATT_KV_HEADS * ATT_HEAD_DIM
RET_HEADS = 8
RET_DK = 128
RET_DV = 256
RET_CHUNK = 128
RET_QK = RET_HEADS * RET_DK
RET_V = RET_HEADS * RET_DV
ROPE_BASE = 10000.0
D_FF = 5632
NORM_EPS = 1e-6
N_SUBLAYERS = 3
PAST_LEN = 16384
D_IN = ATT_Q + 2 * ATT_KV + 2 * RET_QK + 2 * RET_V + 2 * D_MODEL

V7X_SUBLANES = 8
V7X_BF16_ROWS = 16
V7X_VMEM_LIMIT_BYTES = 60 * 1024 * 1024

TOKEN_TILE = 1024
MERGE_TOKEN_TILE = 512
FF_TILE = 512
COL_TILE = 512
ADA_COL_TILE = 1024
SAMPLE_SEQS_PER_STEP = 4
MOD_PAD_ROWS = 8

_QA_TILES = ATT_Q // COL_TILE
_KVA_TILES = 2 * ATT_KV // COL_TILE
_QKR_TILES = 2 * RET_QK // COL_TILE
_WIDE_COLS = 2 * RET_V + 2 * D_MODEL
_WIDE_TILES = _WIDE_COLS // COL_TILE
_QKR_START = _QA_TILES + _KVA_TILES
_WIDE_START = _QKR_START + _QKR_TILES
assert _WIDE_START + _WIDE_TILES == D_IN // COL_TILE and _KVA_TILES == 1

_ALIBI_SLOPES = [2.0 ** (-8.0 * (h + 1) / ATT_HEADS) for h in range(ATT_HEADS)]
_RET_LOG_G = [math.log(1.0 - 2.0 ** (-5.0 - h)) for h in range(RET_HEADS)]
_ATT_SCALE = ATT_HEAD_DIM ** -0.5
_RET_K_SCALE = RET_DK ** -0.5


class _Stream(NamedTuple):
    n_tokens: int
    tile: int
    rows_per_mod: int
    mod_block_rows: int
    mod_block_base: int

    @property
    def n_tiles(self):
        return self.n_tokens // self.tile

    @property
    def sub_rows(self):
        return min(self.rows_per_mod, self.tile)

    @property
    def group_rows(self):
        return max(self.sub_rows, V7X_BF16_ROWS)

    @property
    def n_groups(self):
        return self.tile // self.group_rows

    @property
    def mods_per_group(self):
        return self.group_rows // self.sub_rows


def _rms(x):
    return x * lax.rsqrt(jnp.mean(x * x, axis=-1, keepdims=True) + NORM_EPS)


def _silu(x):
    return x * jax.nn.sigmoid(x)


def _for_groups(n_groups, fn):
    if n_groups == 1:
        fn(0)
    else:
        def body(g, carry):
            fn(g)
            return carry
        lax.fori_loop(0, n_groups, body, 0)


def _group_base(st, g):
    return 0 if st.n_groups == 1 else pl.multiple_of(g * st.group_rows, st.group_rows)


def _mod_row(st, i, g, s):
    if st.rows_per_mod >= st.tile:
        return (i * st.tile) // st.rows_per_mod
    return g * st.mods_per_group + s


def _pre_norm(st, i, x_ref, shift_ref, scale_ref, gain, h_ref):
    def group(g):
        base = _group_base(st, g)
        parts = []
        for s in range(st.mods_per_group):
            rows = pl.ds(base + s * st.sub_rows, st.sub_rows)
            m = _mod_row(st, i, g, s)
            x = x_ref[rows, :]
            sc = scale_ref[pl.ds(m, 1), :]
            sh = shift_ref[pl.ds(m, 1), :]
            parts.append(_rms(x) * gain * (1.0 + sc) + sh)
        h = parts[0] if len(parts) == 1 else jnp.concatenate(parts, axis=0)
        h_ref[pl.ds(base, st.group_rows), :] = h.astype(BF16)
    _for_groups(st.n_groups, group)


def _post_residual(st, i, x_ref, gate_ref, gain, o_ref, coeff):
    def group(g):
        base = _group_base(st, g)
        for s in range(st.mods_per_group):
            rows = pl.ds(base + s * st.sub_rows, st.sub_rows)
            m = _mod_row(st, i, g, s)
            gt = gate_ref[pl.ds(m, 1), :]
            y = gt * (_rms(o_ref[rows, :]) * gain)
            if coeff != 1.0:
                y = coeff * y
            o_ref[rows, :] = x_ref[rows, :] + y
    _for_groups(st.n_groups, group)


def _ada_kernel(c_ref, w_ref, b_ref, o_ref):
    a = _silu(c_ref[...]).astype(BF16)
    o_ref[0] = jnp.dot(a, w_ref[...].astype(BF16), preferred_element_type=F32) + b_ref[...]


def _ada(c_all, w_ada, b_ada):
    rows = c_all.shape[0]
    n_vec = N_SUBLAYERS * 3
    per_vec = D_MODEL // ADA_COL_TILE
    return pl.pallas_call(
        _ada_kernel,
        out_shape=jax.ShapeDtypeStruct((n_vec, rows, D_MODEL), F32),
        grid=(n_vec * per_vec,),
        in_specs=[
            pl.BlockSpec((rows, D_MODEL), lambda j: (0, 0)),
            pl.BlockSpec((D_MODEL, ADA_COL_TILE), lambda j: (0, j)),
            pl.BlockSpec((1, ADA_COL_TILE), lambda j: (0, j)),
        ],
        out_specs=pl.BlockSpec((1, rows, ADA_COL_TILE), lambda j: (j // per_vec, 0, j % per_vec)),
        compiler_params=pltpu.CompilerParams(
            dimension_semantics=("arbitrary",), vmem_limit_bytes=V7X_VMEM_LIMIT_BYTES),
        name="ada_mod",
    )(c_all, w_ada, b_ada.reshape(1, -1))


def _mod_spec(st, vec):
    if st.rows_per_mod >= st.tile:
        index = lambda i, j: (vec, st.mod_block_base, 0)
    else:
        index = lambda i, j: (vec, st.mod_block_base + i, 0)
    return pl.BlockSpec((None, st.mod_block_rows, D_MODEL), index)


def _ffn_kernel(st, sub, x_ref, shift_ref, scale_ref, gate_ref, npre_ref, npost_ref,
                wg_ref, wu_ref, wd_ref, o_ref, h_ref):
    i = pl.program_id(0)
    j = pl.program_id(1)

    def partial_down():
        h = h_ref[...]
        g = jnp.dot(h, wg_ref[...], preferred_element_type=F32)
        u = jnp.dot(h, wu_ref[...], preferred_element_type=F32)
        a = (_silu(g) * u).astype(BF16)
        return jnp.dot(a, wd_ref[...], preferred_element_type=F32)

    @pl.when(j == 0)
    def _():
        _pre_norm(st, i, x_ref, shift_ref, scale_ref, npre_ref[sub:sub + 1, :], h_ref)
        o_ref[...] = partial_down()

    @pl.when(j > 0)
    def _():
        o_ref[...] += partial_down()

    @pl.when(j == pl.num_programs(1) - 1)
    def _():
        _post_residual(st, i, x_ref, gate_ref, npost_ref[sub:sub + 1, :], o_ref, 0.5)


def _ffn(st, sub, x, mod, npre, npost, wg, wu, wd):
    row_spec = pl.BlockSpec((st.tile, D_MODEL), lambda i, j: (i, 0))
    full_spec = pl.BlockSpec((N_SUBLAYERS, D_MODEL), lambda i, j: (0, 0))
    return pl.pallas_call(
        functools.partial(_ffn_kernel, st, sub),
        out_shape=jax.ShapeDtypeStruct((st.n_tokens, D_MODEL), F32),
        grid=(st.n_tiles, D_FF // FF_TILE),
        in_specs=[
            row_spec,
            _mod_spec(st, 3 * sub + 0), _mod_spec(st, 3 * sub + 1), _mod_spec(st, 3 * sub + 2),
            full_spec, full_spec,
            pl.BlockSpec((D_MODEL, FF_TILE), lambda i, j: (0, j)),
            pl.BlockSpec((D_MODEL, FF_TILE), lambda i, j: (0, j)),
            pl.BlockSpec((FF_TILE, D_MODEL), lambda i, j: (j, 0)),
        ],
        out_specs=row_spec,
        scratch_shapes=[pltpu.VMEM((st.tile, D_MODEL), BF16)],
        compiler_params=pltpu.CompilerParams(
            dimension_semantics=("parallel", "arbitrary"), vmem_limit_bytes=V7X_VMEM_LIMIT_BYTES),
        name=f"ffn{sub}",
    )(x, mod, mod, mod, npre, npost, wg, wu, wd)


def _proj_kernel(st, x_ref, shift_ref, scale_ref, npre_ref, w_ref,
                 qa_ref, kva_ref, qkr_ref, wide_ref, h_ref):
    i = pl.program_id(0)
    j = pl.program_id(1)

    @pl.when(j == 0)
    def _():
        _pre_norm(st, i, x_ref, shift_ref, scale_ref, npre_ref[1:2, :], h_ref)

    def project(dst_ref):
        dst_ref[...] = jnp.dot(h_ref[...], w_ref[...], preferred_element_type=F32).astype(dst_ref.dtype)

    @pl.when(j < _QA_TILES)
    def _():
        project(qa_ref)

    @pl.when(j == _QA_TILES)
    def _():
        project(kva_ref)

    @pl.when((j >= _QKR_START) & (j < _WIDE_START))
    def _():
        project(qkr_ref)

    @pl.when(j >= _WIDE_START)
    def _():
        project(wide_ref)


def _proj(st, x, mod, npre, w_in, narrow_dtype):
    n_tiles = st.n_tiles
    n = st.n_tokens
    row_spec = pl.BlockSpec((st.tile, D_MODEL), lambda i, j: (i, 0))
    out_block = (st.tile, COL_TILE)
    return pl.pallas_call(
        functools.partial(_proj_kernel, st),
        out_shape=(
            jax.ShapeDtypeStruct((n, ATT_Q), narrow_dtype),
            jax.ShapeDtypeStruct((n, 2 * ATT_KV), F32),
            jax.ShapeDtypeStruct((n, 2 * RET_QK), F32),
            jax.ShapeDtypeStruct((n, _WIDE_COLS), narrow_dtype),
        ),
        grid=(n_tiles, D_IN // COL_TILE),
        in_specs=[
            row_spec,
            _mod_spec(st, 3), _mod_spec(st, 4),
            pl.BlockSpec((N_SUBLAYERS, D_MODEL), lambda i, j: (0, 0)),
            pl.BlockSpec((D_MODEL, COL_TILE), lambda i, j: (0, j)),
        ],
        out_specs=(
            pl.BlockSpec(out_block, lambda i, j: (i, jnp.minimum(j, _QA_TILES - 1))),
            pl.BlockSpec(out_block, lambda i, j: (i, 0)),
            pl.BlockSpec(out_block, lambda i, j: (i, jnp.clip(j - _QKR_START, 0, _QKR_TILES - 1))),
            pl.BlockSpec(out_block, lambda i, j: (i, jnp.maximum(j - _WIDE_START, 0))),
        ),
        scratch_shapes=[pltpu.VMEM((st.tile, D_MODEL), BF16)],
        compiler_params=pltpu.CompilerParams(
            dimension_semantics=("parallel", "arbitrary"), vmem_limit_bytes=V7X_VMEM_LIMIT_BYTES),
        name="in_proj",
    )(x, mod, mod, npre, w_in)


def _attention(q, k2, v2, sinks_ref, first_valid_key):
    tq = q.shape[0]
    a_idx = lax.broadcasted_iota(jnp.int32, (tq, 2 * WINDOW), 0)
    b_idx = lax.broadcasted_iota(jnp.int32, (tq, 2 * WINDOW), 1)
    dist = WINDOW + a_idx - b_idx
    mask = (dist >= 0) & (dist <= WINDOW) & (b_idx >= first_valid_key)
    dist_f = dist.astype(F32)
    outs = []
    for kv in range(ATT_KV_HEADS):
        cols = slice(kv * ATT_HEAD_DIM, (kv + 1) * ATT_HEAD_DIM)
        kk = k2[:, cols]
        vv = v2[:, cols]
        heads = range(kv * ATT_GROUP, (kv + 1) * ATT_GROUP)
        qg = jnp.concatenate(
            [q[:, h * ATT_HEAD_DIM:(h + 1) * ATT_HEAD_DIM] for h in heads], axis=0).astype(BF16)
        s_all = lax.dot_general(qg, kk, (((1,), (1,)), ((), ())), preferred_element_type=F32)
        probs = []
        for g, h in enumerate(heads):
            s = s_all[g * tq:(g + 1) * tq] * _ATT_SCALE - _ALIBI_SLOPES[h] * dist_f
            s = jnp.where(mask, s, -jnp.inf)
            sink = sinks_ref[h]
            m = jnp.maximum(jnp.max(s, axis=-1, keepdims=True), sink)
            p = jnp.exp(s - m)
            inv = 1.0 / (jnp.sum(p, axis=-1, keepdims=True) + jnp.exp(sink - m))
            probs.append((p * inv).astype(BF16))
        o_all = jnp.dot(jnp.concatenate(probs, axis=0), vv, preferred_element_type=F32)
        outs.extend(o_all[g * tq:(g + 1) * tq] for g in range(ATT_GROUP))
    return jnp.concatenate(outs, axis=-1)


def _attn_prompt_kernel(sinks_ref, q_ref, kc_ref, vc_ref, kp_ref, vp_ref, o_ref):
    blk = pl.program_id(1)
    k2 = jnp.concatenate([kp_ref[...], kc_ref[...]], axis=0).astype(BF16)
    v2 = jnp.concatenate([vp_ref[...], vc_ref[...]], axis=0).astype(BF16)
    first_valid = jnp.where(blk == 0, WINDOW, 0)
    o_ref[...] = _attention(q_ref[...], k2, v2, sinks_ref, first_valid).astype(o_ref.dtype)


def _attn_prompt(qa, kva, sinks, batch, seq):
    nb = seq // WINDOW
    cur = lambda col: (lambda b, i: (b * nb + i, col))
    prev = lambda col: (lambda b, i: (b * nb + jnp.maximum(i - 1, 0), col))
    return pl.pallas_call(
        _attn_prompt_kernel,
        out_shape=jax.ShapeDtypeStruct((batch * seq, ATT_Q), BF16),
        grid=(batch, nb),
        in_specs=[
            pl.BlockSpec(memory_space=pltpu.SMEM),
            pl.BlockSpec((WINDOW, ATT_Q), lambda b, i: (b * nb + i, 0)),
            pl.BlockSpec((WINDOW, ATT_KV), cur(0)),
            pl.BlockSpec((WINDOW, ATT_KV), cur(1)),
            pl.BlockSpec((WINDOW, ATT_KV), prev(0)),
            pl.BlockSpec((WINDOW, ATT_KV), prev(1)),
        ],
        out_specs=pl.BlockSpec((WINDOW, ATT_Q), lambda b, i: (b * nb + i, 0)),
        compiler_params=pltpu.CompilerParams(
            dimension_semantics=("parallel", "arbitrary"), vmem_limit_bytes=V7X_VMEM_LIMIT_BYTES),
        name="attn_prompt",
    )(sinks, qa, kva, kva, kva, kva)


def _rotate(x, cos2, sin2):
    return x * cos2 + pltpu.roll(x, RET_DK // 2, axis=1) * sin2


def _retention_head(qh, kh, vh, s_prev, log_g, chunk_len):
    c = qh.shape[0]
    pad = RET_CHUNK - c
    if pad:
        kh = jnp.concatenate([kh, jnp.zeros((pad, RET_DK), F32)], axis=0)
        vh = jnp.concatenate([vh, jnp.zeros((pad, RET_DV), vh.dtype)], axis=0)
    vb = vh.astype(BF16)
    row = lax.broadcasted_iota(jnp.int32, (c, RET_CHUNK), 0)
    col = lax.broadcasted_iota(jnp.int32, (c, RET_CHUNK), 1)
    diff = (row - col).astype(F32)
    decay = jnp.where(diff >= 0, jnp.exp(jnp.maximum(diff, 0.0) * log_g), 0.0)
    scores = lax.dot_general(qh.astype(BF16), kh.astype(BF16), (((1,), (1,)), ((), ())),
                             preferred_element_type=F32) * decay
    y = jnp.dot(scores.astype(BF16), vb, preferred_element_type=F32)
    q_w = jnp.exp((row.astype(F32) + 1.0) * log_g)
    y = y + jnp.dot((qh * q_w).astype(BF16), s_prev.astype(BF16), preferred_element_type=F32)
    k_row = lax.broadcasted_iota(jnp.int32, (RET_CHUNK, RET_DK), 0).astype(F32)
    k_w = jnp.exp((chunk_len - 1.0 - k_row) * log_g)
    kt = (kh * k_w).T.astype(BF16)
    s_new = math.exp(chunk_len * log_g) * s_prev + jnp.dot(kt, vb, preferred_element_type=F32)
    return y, s_new


def _retention(rows, q_ref, k_ref, v_ref, gr_ref, cos2, sin2, read_state, write_state, o_ref, chunk_len):
    for h in range(RET_HEADS):
        qk_cols = slice(h * RET_DK, (h + 1) * RET_DK)
        v_cols = slice(h * RET_DV, (h + 1) * RET_DV)
        qh = _rotate(q_ref[rows, qk_cols], cos2, sin2)
        kh = _rotate(k_ref[rows, qk_cols], cos2, sin2) * _RET_K_SCALE
        y, s_new = _retention_head(qh, kh, v_ref[rows, v_cols], read_state(h), _RET_LOG_G[h], chunk_len)
        write_state(h, s_new)
        gate = gr_ref[rows, v_cols].astype(F32)
        o_ref[rows, v_cols] = (_silu(gate) * _rms(y)).astype(o_ref.dtype)


def _ret_prompt_kernel(q_ref, k_ref, v_ref, gr_ref, cos_ref, sin_ref, o_ref, s_ref):
    @pl.when(pl.program_id(1) == 0)
    def _():
        s_ref[...] = jnp.zeros_like(s_ref)

    def write_state(h, s):
        s_ref[0, h] = s

    _retention(slice(None), q_ref, k_ref, v_ref, gr_ref, cos_ref[...], sin_ref[...],
               lambda h: s_ref[0, h], write_state, o_ref, float(RET_CHUNK))


def _ret_prompt(qkr, wide, cos2, sin2, batch, seq):
    nc = seq // RET_CHUNK
    rows = lambda col: (lambda b, c: (b * nc + c, col))
    return pl.pallas_call(
        _ret_prompt_kernel,
        out_shape=(
            jax.ShapeDtypeStruct((batch * seq, RET_V), BF16),
            jax.ShapeDtypeStruct((batch, RET_HEADS, RET_DK, RET_DV), F32),
        ),
        grid=(batch, nc),
        in_specs=[
            pl.BlockSpec((RET_CHUNK, RET_QK), rows(0)),
            pl.BlockSpec((RET_CHUNK, RET_QK), rows(1)),
            pl.BlockSpec((RET_CHUNK, RET_V), rows(0)),
            pl.BlockSpec((RET_CHUNK, RET_V), rows(1)),
            pl.BlockSpec((RET_CHUNK, RET_DK), lambda b, c: (c, 0)),
            pl.BlockSpec((RET_CHUNK, RET_DK), lambda b, c: (c, 0)),
        ],
        out_specs=(
            pl.BlockSpec((RET_CHUNK, RET_V), rows(0)),
            pl.BlockSpec((1, RET_HEADS, RET_DK, RET_DV), lambda b, c: (b, 0, 0, 0)),
        ),
        compiler_params=pltpu.CompilerParams(
            dimension_semantics=("parallel", "arbitrary"), vmem_limit_bytes=V7X_VMEM_LIMIT_BYTES),
        name="ret_prompt",
    )(qkr, qkr, wide, wide, cos2, sin2)


def _mix_sample_kernel(t_new, sinks_ref, qa_ref, kn_ref, vn_ref, ck_ref, cv_ref,
                       q_ref, k_ref, v_ref, gr_ref, cos_ref, sin_ref, s_in_ref,
                       oa_ref, or_ref, ko_ref, vo_ref, s_out_ref):
    cos2, sin2 = cos_ref[...], sin_ref[...]
    zeros = jnp.zeros((WINDOW - t_new, ATT_KV), F32)
    for n in range(SAMPLE_SEQS_PER_STEP):
        rows = slice(n * t_new, (n + 1) * t_new)
        kc, vc = ck_ref[n], cv_ref[n]
        kn, vn = kn_ref[rows, :], vn_ref[rows, :]
        k2 = jnp.concatenate([kc, kn, zeros], axis=0).astype(BF16)
        v2 = jnp.concatenate([vc, vn, zeros], axis=0).astype(BF16)
        oa_ref[rows, :] = _attention(qa_ref[rows, :], k2, v2, sinks_ref, 0).astype(oa_ref.dtype)
        ko_ref[n, :WINDOW - t_new, :] = kc[t_new:]
        ko_ref[n, WINDOW - t_new:, :] = kn
        vo_ref[n, :WINDOW - t_new, :] = vc[t_new:]
        vo_ref[n, WINDOW - t_new:, :] = vn

        def write_state(h, s, n=n):
            s_out_ref[n, h] = s

        _retention(rows, q_ref, k_ref, v_ref, gr_ref, cos2, sin2,
                   lambda h, n=n: s_in_ref[n, h], write_state, or_ref, float(t_new))


def _mix_sample(qa, kva, qkr, wide, cache_k, cache_v, state, sinks, cos2, sin2, n_seq, t_new):
    nb = SAMPLE_SEQS_PER_STEP
    rows = lambda col: (lambda n: (n, col))
    cache_spec = pl.BlockSpec((nb, WINDOW, ATT_KV), lambda n: (n, 0, 0))
    state_spec = pl.BlockSpec((nb, RET_HEADS, RET_DK, RET_DV), lambda n: (n, 0, 0, 0))
    table_spec = pl.BlockSpec((t_new, RET_DK), lambda n: (0, 0))
    return pl.pallas_call(
        functools.partial(_mix_sample_kernel, t_new),
        out_shape=(
            jax.ShapeDtypeStruct((n_seq * t_new, ATT_Q), F32),
            jax.ShapeDtypeStruct((n_seq * t_new, RET_V), F32),
            jax.ShapeDtypeStruct(cache_k.shape, F32),
            jax.ShapeDtypeStruct(cache_v.shape, F32),
            jax.ShapeDtypeStruct(state.shape, F32),
        ),
        grid=(n_seq // nb,),
        in_specs=[
            pl.BlockSpec(memory_space=pltpu.SMEM),
            pl.BlockSpec((nb * t_new, ATT_Q), rows(0)),
            pl.BlockSpec((nb * t_new, ATT_KV), rows(0)),
            pl.BlockSpec((nb * t_new, ATT_KV), rows(1)),
            cache_spec, cache_spec,
            pl.BlockSpec((nb * t_new, RET_QK), rows(0)),
            pl.BlockSpec((nb * t_new, RET_QK), rows(1)),
            pl.BlockSpec((nb * t_new, RET_V), rows(0)),
            pl.BlockSpec((nb * t_new, RET_V), rows(1)),
            table_spec, table_spec,
            state_spec,
        ],
        out_specs=(
            pl.BlockSpec((nb * t_new, ATT_Q), rows(0)),
            pl.BlockSpec((nb * t_new, RET_V), rows(0)),
            cache_spec, cache_spec, state_spec,
        ),
        compiler_params=pltpu.CompilerParams(
            dimension_semantics=("parallel",), vmem_limit_bytes=V7X_VMEM_LIMIT_BYTES),
        name="mix_sample",
    )(sinks, qa, kva, kva, cache_k, cache_v, qkr, qkr, wide, wide, cos2, sin2, state)


def _merge_kernel(st, x_ref, gate_ref, npost_ref, oa_ref, or_ref, ga_ref, gr_ref,
                  wpa_ref, wpr_ref, wo_ref, o_ref):
    i = pl.program_id(0)
    j = pl.program_id(1)

    def partial_out():
        a = jnp.dot(oa_ref[...].astype(BF16), wpa_ref[...], preferred_element_type=F32)
        r = jnp.dot(or_ref[...].astype(BF16), wpr_ref[...], preferred_element_type=F32)
        merged = (jax.nn.sigmoid(ga_ref[...].astype(F32)) * a
                  + jax.nn.sigmoid(gr_ref[...].astype(F32)) * r)
        return jnp.dot(merged.astype(BF16), wo_ref[...], preferred_element_type=F32)

    @pl.when(j == 0)
    def _():
        o_ref[...] = partial_out()

    @pl.when(j > 0)
    def _():
        o_ref[...] += partial_out()

    @pl.when(j == pl.num_programs(1) - 1)
    def _():
        _post_residual(st, i, x_ref, gate_ref, npost_ref[1:2, :], o_ref, 1.0)


def _merge(st, x, mod, npost, o_a, o_r, wide, w_pa, w_pr, w_o):
    n_col = D_MODEL // COL_TILE
    gate_a_block = 2 * RET_V // COL_TILE
    gate_r_block = gate_a_block + n_col
    row_spec = pl.BlockSpec((st.tile, D_MODEL), lambda i, j: (i, 0))
    return pl.pallas_call(
        functools.partial(_merge_kernel, st),
        out_shape=jax.ShapeDtypeStruct((st.n_tokens, D_MODEL), F32),
        grid=(st.n_tiles, n_col),
        in_specs=[
            row_spec,
            _mod_spec(st, 5),
            pl.BlockSpec((N_SUBLAYERS, D_MODEL), lambda i, j: (0, 0)),
            pl.BlockSpec((st.tile, ATT_Q), lambda i, j: (i, 0)),
            pl.BlockSpec((st.tile, RET_V), lambda i, j: (i, 0)),
            pl.BlockSpec((st.tile, COL_TILE), lambda i, j: (i, gate_a_block + j)),
            pl.BlockSpec((st.tile, COL_TILE), lambda i, j: (i, gate_r_block + j)),
            pl.BlockSpec((ATT_Q, COL_TILE), lambda i, j: (0, j)),
            pl.BlockSpec((RET_V, COL_TILE), lambda i, j: (0, j)),
            pl.BlockSpec((COL_TILE, D_MODEL), lambda i, j: (j, 0)),
        ],
        out_specs=row_spec,
        compiler_params=pltpu.CompilerParams(
            dimension_semantics=("parallel", "arbitrary"), vmem_limit_bytes=V7X_VMEM_LIMIT_BYTES),
        name="merge_out",
    )(x, mod, npost, o_a, o_r, wide, wide, w_pa, w_pr, w_o)


def _rotation_tables(pos):
    half = RET_DK // 2
    inv_freq = ROPE_BASE ** (-jnp.linspace(0.0, 1.0, half, dtype=F32))
    ang = pos[:, None] * inv_freq[None, :]
    cos, sin = jnp.cos(ang), jnp.sin(ang)
    return jnp.concatenate([cos, cos], axis=-1), jnp.concatenate([-sin, sin], axis=-1)


def kernel(x_prompt, x_sample, cache_k_win, cache_v_win, state_ret, c_prompt, c_sample, w_ada, b_ada,
           norm_pre, norm_post, w_in, attn_sinks, w_pa, w_pr, w_o,
           ffn1_gate, ffn1_up, ffn1_down, ffn2_gate, ffn2_up, ffn2_down):
    batch, seq, _ = x_prompt.shape
    n_seq, t_new, _ = x_sample.shape
    assert w_ada.shape[0] == 1, "single-layer step"
    assert t_new == V7X_SUBLANES and seq % TOKEN_TILE == 0 and (n_seq * t_new) % TOKEN_TILE == 0
    assert seq % MERGE_TOKEN_TILE == 0 and (n_seq * t_new) % MERGE_TOKEN_TILE == 0
    assert batch <= MOD_PAD_ROWS and n_seq % MOD_PAD_ROWS == 0

    c_all = jnp.concatenate(
        [c_sample, c_prompt, jnp.zeros((MOD_PAD_ROWS - batch, D_MODEL), F32)], axis=0)
    mod = _ada(c_all, w_ada[0], b_ada[0])

    def prompt_stream(tile):
        return _Stream(batch * seq, tile, seq, MOD_PAD_ROWS, n_seq // MOD_PAD_ROWS)

    def sample_stream(tile):
        return _Stream(n_seq * t_new, tile, t_new, tile // t_new, 0)

    prompt, prompt_m = prompt_stream(TOKEN_TILE), prompt_stream(MERGE_TOKEN_TILE)
    sample, sample_m = sample_stream(TOKEN_TILE), sample_stream(MERGE_TOKEN_TILE)

    npre, npost = norm_pre[0], norm_post[0]
    sinks = attn_sinks[0]
    bf = lambda w: w[0].astype(BF16)
    w_in_b, w_pa_b, w_pr_b, w_o_b = bf(w_in), bf(w_pa), bf(w_pr), bf(w_o)
    f1g, f1u, f1d = bf(ffn1_gate), bf(ffn1_up), bf(ffn1_down)
    f2g, f2u, f2d = bf(ffn2_gate), bf(ffn2_up), bf(ffn2_down)

    xp = x_prompt.reshape(batch * seq, D_MODEL)
    xp = _ffn(prompt, 0, xp, mod, npre, npost, f1g, f1u, f1d)
    qa, kva, qkr, wide = _proj(prompt, xp, mod, npre, w_in_b, BF16)
    o_a = _attn_prompt(qa, kva, sinks, batch, seq)
    cos_p, sin_p = _rotation_tables(jnp.arange(seq, dtype=F32))
    o_r, state_p = _ret_prompt(qkr, wide, cos_p, sin_p, batch, seq)
    xp = _merge(prompt_m, xp, mod, npost, o_a, o_r, wide, w_pa_b, w_pr_b, w_o_b)
    xp = _ffn(prompt, 2, xp, mod, npre, npost, f2g, f2u, f2d)
    kva_p = kva.reshape(batch, seq, 2 * ATT_KV)[:, seq - WINDOW:]
    kv_shape = (1, batch, WINDOW, ATT_KV_HEADS, ATT_HEAD_DIM)
    k_win_p = kva_p[..., :ATT_KV].reshape(kv_shape)
    v_win_p = kva_p[..., ATT_KV:].reshape(kv_shape)

    xs = x_sample.reshape(n_seq * t_new, D_MODEL)
    xs = _ffn(sample, 0, xs, mod, npre, npost, f1g, f1u, f1d)
    qa_s, kva_s, qkr_s, wide_s = _proj(sample, xs, mod, npre, w_in_b, F32)
    cos_s, sin_s = _rotation_tables(jnp.arange(t_new, dtype=F32) + PAST_LEN)
    o_a_s, o_r_s, k_s, v_s, state_s = _mix_sample(
        qa_s, kva_s, qkr_s, wide_s,
        cache_k_win[0].reshape(n_seq, WINDOW, ATT_KV), cache_v_win[0].reshape(n_seq, WINDOW, ATT_KV),
        state_ret[0], sinks, cos_s, sin_s, n_seq, t_new)
    xs = _merge(sample_m, xs, mod, npost, o_a_s, o_r_s, wide_s, w_pa_b, w_pr_b, w_o_b)
    xs = _ffn(sample, 2, xs, mod, npre, npost, f2g, f2u, f2d)
    kvs_shape = (1, n_seq, WINDOW, ATT_KV_HEADS, ATT_HEAD_DIM)

    return (xp.reshape(batch, seq, D_MODEL), xs.reshape(n_seq, t_new, D_MODEL),
            k_win_p, v_win_p, state_p[None],
            k_s.reshape(kvs_shape), v_s.reshape(kvs_shape), state_s[None])
```

```python
import functools
import math
from typing import NamedTuple

import jax
import jax.numpy as jnp
import numpy as np
from jax import lax
from jax.experimental import pallas as pl
from jax.experimental.pallas import tpu as pltpu

F32 = jnp.float32
BF16 = jnp.bfloat16

D_MODEL = 2048
WINDOW = 128
ATT_HEADS = 16
ATT_KV_HEADS = 4
ATT_HEAD_DIM = 64
ATT_GROUP = ATT_HEADS // ATT_KV_HEADS
ATT_Q = ATT_HEADS * ATT_HEAD_DIM
ATT_KV = ATT_KV_HEADS * ATT_HEAD_DIM
RET_HEADS = 8
RET_DK = 128
RET_DV = 256
RET_CHUNK = 128
RET_QK = RET_HEADS * RET_DK
RET_V = RET_HEADS * RET_DV
ROPE_BASE = 10000.0
D_FF = 5632
NORM_EPS = 1e-6
N_SUBLAYERS = 3
PAST_LEN = 16384
D_IN = ATT_Q + 2 * ATT_KV + 2 * RET_QK + 2 * RET_V + 2 * D_MODEL

V7X_SUBLANES = 8
V7X_BF16_ROWS = 16
V7X_VMEM_LIMIT_BYTES = 60 * 1024 * 1024

TOKEN_TILE = 1024
MERGE_TOKEN_TILE = 512
FF_TILE = 256
COL_TILE = 512
ADA_COL_TILE = 1024
SAMPLE_SEQS_PER_STEP = 4
MOD_PAD_ROWS = 8

_QA_TILES = ATT_Q // COL_TILE
_KVA_TILES = 2 * ATT_KV // COL_TILE
_QKR_TILES = 2 * RET_QK // COL_TILE
_WIDE_COLS = 2 * RET_V + 2 * D_MODEL
_WIDE_TILES = _WIDE_COLS // COL_TILE
_QKR_START = _QA_TILES + _KVA_TILES
_WIDE_START = _QKR_START + _QKR_TILES
assert _WIDE_START + _WIDE_TILES == D_IN // COL_TILE and _KVA_TILES == 1

_ALIBI_SLOPES = [2.0 ** (-8.0 * (h + 1) / ATT_HEADS) for h in range(ATT_HEADS)]
_RET_LOG_G = [math.log(1.0 - 2.0 ** (-5.0 - h)) for h in range(RET_HEADS)]
_ATT_SCALE = ATT_HEAD_DIM ** -0.5
_RET_K_SCALE = RET_DK ** -0.5


class _Stream(NamedTuple):
    n_tokens: int
    tile: int
    rows_per_mod: int
    mod_block_rows: int
    mod_block_base: int

    @property
    def n_tiles(self):
        return self.n_tokens // self.tile

    @property
    def sub_rows(self):
        return min(self.rows_per_mod, self.tile)

    @property
    def group_rows(self):
        return max(self.sub_rows, V7X_BF16_ROWS)

    @property
    def n_groups(self):
        return self.tile // self.group_rows

    @property
    def mods_per_group(self):
        return self.group_rows // self.sub_rows


def _rms(x):
    return x * lax.rsqrt(jnp.mean(x * x, axis=-1, keepdims=True) + NORM_EPS)


def _silu(x):
    return x * jax.nn.sigmoid(x)


def _for_groups(n_groups, fn):
    if n_groups == 1:
        fn(0)
    else:
        def body(g, carry):
            fn(g)
            return carry
        lax.fori_loop(0, n_groups, body, 0)


def _group_base(st, g):
    return 0 if st.n_groups == 1 else pl.multiple_of(g * st.group_rows, st.group_rows)


def _mod_row(st, i, g, s):
    if st.rows_per_mod >= st.tile:
        return (i * st.tile) // st.rows_per_mod
    return g * st.mods_per_group + s


def _pre_norm(st, i, x_ref, shift_ref, scale_ref, gain, h_ref):
    def group(g):
        base = _group_base(st, g)
        parts = []
        for s in range(st.mods_per_group):
            rows = pl.ds(base + s * st.sub_rows, st.sub_rows)
            m = _mod_row(st, i, g, s)
            x = x_ref[rows, :]
            sc = scale_ref[pl.ds(m, 1), :]
            sh = shift_ref[pl.ds(m, 1), :]
            parts.append(_rms(x) * gain * (1.0 + sc) + sh)
        h = parts[0] if len(parts) == 1 else jnp.concatenate(parts, axis=0)
        h_ref[pl.ds(base, st.group_rows), :] = h.astype(BF16)
    _for_groups(st.n_groups, group)


def _post_residual(st, i, x_ref, gate_ref, gain, o_ref, coeff):
    def group(g):
        base = _group_base(st, g)
        for s in range(st.mods_per_group):
            rows = pl.ds(base + s * st.sub_rows, st.sub_rows)
            m = _mod_row(st, i, g, s)
            gt = gate_ref[pl.ds(m, 1), :]
            y = gt * (_rms(o_ref[rows, :]) * gain)
            if coeff != 1.0:
                y = coeff * y
            o_ref[rows, :] = x_ref[rows, :] + y
    _for_groups(st.n_groups, group)


def _ada_kernel(c_ref, w_ref, b_ref, o_ref):
    a = _silu(c_ref[...]).astype(BF16)
    o_ref[0] = jnp.dot(a, w_ref[...].astype(BF16), preferred_element_type=F32) + b_ref[...]


def _ada(c_all, w_ada, b_ada):
    rows = c_all.shape[0]
    n_vec = N_SUBLAYERS * 3
    per_vec = D_MODEL // ADA_COL_TILE
    return pl.pallas_call(
        _ada_kernel,
        out_shape=jax.ShapeDtypeStruct((n_vec, rows, D_MODEL), F32),
        grid=(n_vec * per_vec,),
        in_specs=[
            pl.BlockSpec((rows, D_MODEL), lambda j: (0, 0)),
            pl.BlockSpec((D_MODEL, ADA_COL_TILE), lambda j: (0, j)),
            pl.BlockSpec((1, ADA_COL_TILE), lambda j: (0, j)),
        ],
        out_specs=pl.BlockSpec((1, rows, ADA_COL_TILE), lambda j: (j // per_vec, 0, j % per_vec)),
        compiler_params=pltpu.CompilerParams(
            dimension_semantics=("arbitrary",), vmem_limit_bytes=V7X_VMEM_LIMIT_BYTES),
        name="ada_mod",
    )(c_all, w_ada, b_ada.reshape(1, -1))


def _mod_spec(st, vec):
    if st.rows_per_mod >= st.tile:
        index = lambda i, j: (vec, st.mod_block_base, 0)
    else:
        index = lambda i, j: (vec, st.mod_block_base + i, 0)
    return pl.BlockSpec((None, st.mod_block_rows, D_MODEL), index)


def _ffn_kernel(st, sub, x_ref, shift_ref, scale_ref, gate_ref, npre_ref, npost_ref,
                wg_ref, wu_ref, wd_ref, o_ref, h_ref):
    i = pl.program_id(0)
    j = pl.program_id(1)

    def partial_down():
        h = h_ref[...]
        g = jnp.dot(h, wg_ref[...].astype(BF16), preferred_element_type=F32)
        u = jnp.dot(h, wu_ref[...].astype(BF16), preferred_element_type=F32)
        a = (_silu(g) * u).astype(BF16)
        return jnp.dot(a, wd_ref[...].astype(BF16), preferred_element_type=F32)

    @pl.when(j == 0)
    def _():
        _pre_norm(st, i, x_ref, shift_ref, scale_ref, npre_ref[sub:sub + 1, :], h_ref)
        o_ref[...] = partial_down()

    @pl.when(j > 0)
    def _():
        o_ref[...] += partial_down()

    @pl.when(j == pl.num_programs(1) - 1)
    def _():
        _post_residual(st, i, x_ref, gate_ref, npost_ref[sub:sub + 1, :], o_ref, 0.5)


def _ffn(st, sub, x, mod, npre, npost, wg, wu, wd):
    row_spec = pl.BlockSpec((st.tile, D_MODEL), lambda i, j: (i, 0))
    full_spec = pl.BlockSpec((N_SUBLAYERS, D_MODEL), lambda i, j: (0, 0))
    return pl.pallas_call(
        functools.partial(_ffn_kernel, st, sub),
        out_shape=jax.ShapeDtypeStruct((st.n_tokens, D_MODEL), F32),
        grid=(st.n_tiles, D_FF // FF_TILE),
        in_specs=[
            row_spec,
            _mod_spec(st, 3 * sub + 0), _mod_spec(st, 3 * sub + 1), _mod_spec(st, 3 * sub + 2),
            full_spec, full_spec,
            pl.BlockSpec((D_MODEL, FF_TILE), lambda i, j: (0, j)),
            pl.BlockSpec((D_MODEL, FF_TILE), lambda i, j: (0, j)),
            pl.BlockSpec((FF_TILE, D_MODEL), lambda i, j: (j, 0)),
        ],
        out_specs=row_spec,
        scratch_shapes=[pltpu.VMEM((st.tile, D_MODEL), BF16)],
        compiler_params=pltpu.CompilerParams(
            dimension_semantics=("parallel", "arbitrary"), vmem_limit_bytes=V7X_VMEM_LIMIT_BYTES),
        name=f"ffn{sub}",
    )(x, mod, mod, mod, npre, npost, wg, wu, wd)


def _proj_kernel(st, x_ref, shift_ref, scale_ref, npre_ref, w_ref,
                 qa_ref, kva_ref, qkr_ref, wide_ref, h_ref):
    i = pl.program_id(0)
    j = pl.program_id(1)

    @pl.when(j == 0)
    def _():
        _pre_norm(st, i, x_ref, shift_ref, scale_ref, npre_ref[1:2, :], h_ref)

    def project(dst_ref):
        dst_ref[...] = jnp.dot(h_ref[...], w_ref[...].astype(BF16),
                               preferred_element_type=F32).astype(dst_ref.dtype)

    @pl.when(j < _QA_TILES)
    def _():
        project(qa_ref)

    @pl.when(j == _QA_TILES)
    def _():
        project(kva_ref)

    @pl.when((j >= _QKR_START) & (j < _WIDE_START))
    def _():
        project(qkr_ref)

    @pl.when(j >= _WIDE_START)
    def _():
        project(wide_ref)


def _proj(st, x, mod, npre, w_in, narrow_dtype):
    n_tiles = st.n_tiles
    n = st.n_tokens
    row_spec = pl.BlockSpec((st.tile, D_MODEL), lambda i, j: (i, 0))
    out_block = (st.tile, COL_TILE)
    return pl.pallas_call(
        functools.partial(_proj_kernel, st),
        out_shape=(
            jax.ShapeDtypeStruct((n, ATT_Q), narrow_dtype),
            jax.ShapeDtypeStruct((n, 2 * ATT_KV), F32),
            jax.ShapeDtypeStruct((n, 2 * RET_QK), F32),
            jax.ShapeDtypeStruct((n, _WIDE_COLS), narrow_dtype),
        ),
        grid=(n_tiles, D_IN // COL_TILE),
        in_specs=[
            row_spec,
            _mod_spec(st, 3), _mod_spec(st, 4),
            pl.BlockSpec((N_SUBLAYERS, D_MODEL), lambda i, j: (0, 0)),
            pl.BlockSpec((D_MODEL, COL_TILE), lambda i, j: (0, j)),
        ],
        out_specs=(
            pl.BlockSpec(out_block, lambda i, j: (i, jnp.minimum(j, _QA_TILES - 1))),
            pl.BlockSpec(out_block, lambda i, j: (i, 0)),
            pl.BlockSpec(out_block, lambda i, j: (i, jnp.clip(j - _QKR_START, 0, _QKR_TILES - 1))),
            pl.BlockSpec(out_block, lambda i, j: (i, jnp.maximum(j - _WIDE_START, 0))),
        ),
        scratch_shapes=[pltpu.VMEM((st.tile, D_MODEL), BF16)],
        compiler_params=pltpu.CompilerParams(
            dimension_semantics=("parallel", "arbitrary"), vmem_limit_bytes=V7X_VMEM_LIMIT_BYTES),
        name="in_proj",
    )(x, mod, mod, npre, w_in)


def _attention(q, k2, v2, sinks_ref, first_valid_key):
    tq = q.shape[0]
    a_idx = lax.broadcasted_iota(jnp.int32, (tq, 2 * WINDOW), 0)
    b_idx = lax.broadcasted_iota(jnp.int32, (tq, 2 * WINDOW), 1)
    dist = WINDOW + a_idx - b_idx
    mask = (dist >= 0) & (dist <= WINDOW) & (b_idx >= first_valid_key)
    dist_f = dist.astype(F32)
    outs = []
    for kv in range(ATT_KV_HEADS):
        cols = slice(kv * ATT_HEAD_DIM, (kv + 1) * ATT_HEAD_DIM)
        kk = k2[:, cols]
        vv = v2[:, cols]
        heads = range(kv * ATT_GROUP, (kv + 1) * ATT_GROUP)
        qg = jnp.concatenate(
            [q[:, h * ATT_HEAD_DIM:(h + 1) * ATT_HEAD_DIM] for h in heads], axis=0).astype(BF16)
        s_all = lax.dot_general(qg, kk, (((1,), (1,)), ((), ())), preferred_element_type=F32)
        probs = []
        for g, h in enumerate(heads):
            s = s_all[g * tq:(g + 1) * tq] * _ATT_SCALE - _ALIBI_SLOPES[h] * dist_f
            s = jnp.where(mask, s, -jnp.inf)
            sink = sinks_ref[h]
            m = jnp.maximum(jnp.max(s, axis=-1, keepdims=True), sink)
            p = jnp.exp(s - m)
            inv = 1.0 / (jnp.sum(p, axis=-1, keepdims=True) + jnp.exp(sink - m))
            probs.append((p * inv).astype(BF16))
        o_all = jnp.dot(jnp.concatenate(probs, axis=0), vv, preferred_element_type=F32)
        outs.extend(o_all[g * tq:(g + 1) * tq] for g in range(ATT_GROUP))
    return jnp.concatenate(outs, axis=-1)


def _attn_prompt_kernel(sinks_ref, q_ref, kc_ref, vc_ref, kp_ref, vp_ref, o_ref):
    blk = pl.program_id(1)
    k2 = jnp.concatenate([kp_ref[...], kc_ref[...]], axis=0).astype(BF16)
    v2 = jnp.concatenate([vp_ref[...], vc_ref[...]], axis=0).astype(BF16)
    first_valid = jnp.where(blk == 0, WINDOW, 0)
    o_ref[...] = _attention(q_ref[...], k2, v2, sinks_ref, first_valid).astype(o_ref.dtype)


def _attn_prompt(qa, kva, sinks, batch, seq):
    nb = seq // WINDOW
    cur = lambda col: (lambda b, i: (b * nb + i, col))
    prev = lambda col: (lambda b, i: (b * nb + jnp.maximum(i - 1, 0), col))
    return pl.pallas_call(
        _attn_prompt_kernel,
        out_shape=jax.ShapeDtypeStruct((batch * seq, ATT_Q), BF16),
        grid=(batch, nb),
        in_specs=[
            pl.BlockSpec(memory_space=pltpu.SMEM),
            pl.BlockSpec((WINDOW, ATT_Q), lambda b, i: (b * nb + i, 0)),
            pl.BlockSpec((WINDOW, ATT_KV), cur(0)),
            pl.BlockSpec((WINDOW, ATT_KV), cur(1)),
            pl.BlockSpec((WINDOW, ATT_KV), prev(0)),
            pl.BlockSpec((WINDOW, ATT_KV), prev(1)),
        ],
        out_specs=pl.BlockSpec((WINDOW, ATT_Q), lambda b, i: (b * nb + i, 0)),
        compiler_params=pltpu.CompilerParams(
            dimension_semantics=("parallel", "arbitrary"), vmem_limit_bytes=V7X_VMEM_LIMIT_BYTES),
        name="attn_prompt",
    )(sinks, qa, kva, kva, kva, kva)


def _rotate(x, cos2, sin2):
    return x * cos2 + pltpu.roll(x, RET_DK // 2, axis=1) * sin2


def _retention_head(qh, kh, vh, s_prev, log_g, chunk_len):
    c = qh.shape[0]
    pad = RET_CHUNK - c
    if pad:
        kh = jnp.concatenate([kh, jnp.zeros((pad, RET_DK), F32)], axis=0)
        vh = jnp.concatenate([vh, jnp.zeros((pad, RET_DV), vh.dtype)], axis=0)
    vb = vh.astype(BF16)
    row = lax.broadcasted_iota(jnp.int32, (c, RET_CHUNK), 0)
    col = lax.broadcasted_iota(jnp.int32, (c, RET_CHUNK), 1)
    diff = (row - col).astype(F32)
    decay = jnp.where(diff >= 0, jnp.exp(jnp.maximum(diff, 0.0) * log_g), 0.0)
    scores = lax.dot_general(qh.astype(BF16), kh.astype(BF16), (((1,), (1,)), ((), ())),
                             preferred_element_type=F32) * decay
    y = jnp.dot(scores.astype(BF16), vb, preferred_element_type=F32)
    q_w = jnp.exp((row.astype(F32) + 1.0) * log_g)
    y = y + jnp.dot((qh * q_w).astype(BF16), s_prev.astype(BF16), preferred_element_type=F32)
    k_row = lax.broadcasted_iota(jnp.int32, (RET_CHUNK, RET_DK), 0).astype(F32)
    k_w = jnp.exp((chunk_len - 1.0 - k_row) * log_g)
    kt = (kh * k_w).T.astype(BF16)
    s_new = math.exp(chunk_len * log_g) * s_prev + jnp.dot(kt, vb, preferred_element_type=F32)
    return y, s_new


def _retention(rows, q_ref, k_ref, v_ref, gr_ref, cos2, sin2, read_state, write_state, o_ref, chunk_len):
    for h in range(RET_HEADS):
        qk_cols = slice(h * RET_DK, (h + 1) * RET_DK)
        v_cols = slice(h * RET_DV, (h + 1) * RET_DV)
        qh = _rotate(q_ref[rows, qk_cols], cos2, sin2)
        kh = _rotate(k_ref[rows, qk_cols], cos2, sin2) * _RET_K_SCALE
        y, s_new = _retention_head(qh, kh, v_ref[rows, v_cols], read_state(h), _RET_LOG_G[h], chunk_len)
        write_state(h, s_new)
        gate = gr_ref[rows, v_cols].astype(F32)
        o_ref[rows, v_cols] = (_silu(gate) * _rms(y)).astype(o_ref.dtype)


def _ret_prompt_kernel(q_ref, k_ref, v_ref, gr_ref, cos_ref, sin_ref, o_ref, s_ref):
    @pl.when(pl.program_id(1) == 0)
    def _():
        s_ref[...] = jnp.zeros_like(s_ref)

    def write_state(h, s):
        s_ref[0, h] = s

    _retention(slice(None), q_ref, k_ref, v_ref, gr_ref, cos_ref[...], sin_ref[...],
               lambda h: s_ref[0, h], write_state, o_ref, float(RET_CHUNK))


def _ret_prompt(qkr, wide, cos2, sin2, batch, seq):
    nc = seq // RET_CHUNK
    rows = lambda col: (lambda b, c: (b * nc + c, col))
    return pl.pallas_call(
        _ret_prompt_kernel,
        out_shape=(
            jax.ShapeDtypeStruct((batch * seq, RET_V), BF16),
            jax.ShapeDtypeStruct((batch, RET_HEADS, RET_DK, RET_DV), F32),
        ),
        grid=(batch, nc),
        in_specs=[
            pl.BlockSpec((RET_CHUNK, RET_QK), rows(0)),
            pl.BlockSpec((RET_CHUNK, RET_QK), rows(1)),
            pl.BlockSpec((RET_CHUNK, RET_V), rows(0)),
            pl.BlockSpec((RET_CHUNK, RET_V), rows(1)),
            pl.BlockSpec((RET_CHUNK, RET_DK), lambda b, c: (c, 0)),
            pl.BlockSpec((RET_CHUNK, RET_DK), lambda b, c: (c, 0)),
        ],
        out_specs=(
            pl.BlockSpec((RET_CHUNK, RET_V), rows(0)),
            pl.BlockSpec((1, RET_HEADS, RET_DK, RET_DV), lambda b, c: (b, 0, 0, 0)),
        ),
        compiler_params=pltpu.CompilerParams(
            dimension_semantics=("parallel", "arbitrary"), vmem_limit_bytes=V7X_VMEM_LIMIT_BYTES),
        name="ret_prompt",
    )(qkr, qkr, wide, wide, cos2, sin2)


def _mix_sample_kernel(t_new, sinks_ref, qa_ref, kn_ref, vn_ref, ck_ref, cv_ref,
                       q_ref, k_ref, v_ref, gr_ref, cos_ref, sin_ref, s_in_ref,
                       oa_ref, or_ref, ko_ref, vo_ref, s_out_ref):
    cos2, sin2 = cos_ref[...], sin_ref[...]
    zeros = jnp.zeros((WINDOW - t_new, ATT_KV), F32)
    for n in range(SAMPLE_SEQS_PER_STEP):
        rows = slice(n * t_new, (n + 1) * t_new)
        kc, vc = ck_ref[n], cv_ref[n]
        kn, vn = kn_ref[rows, :], vn_ref[rows, :]
        k2 = jnp.concatenate([kc, kn, zeros], axis=0).astype(BF16)
        v2 = jnp.concatenate([vc, vn, zeros], axis=0).astype(BF16)
        oa_ref[rows, :] = _attention(qa_ref[rows, :], k2, v2, sinks_ref, 0).astype(oa_ref.dtype)
        ko_ref[n, :WINDOW - t_new, :] = kc[t_new:]
        ko_ref[n, WINDOW - t_new:, :] = kn
        vo_ref[n, :WINDOW - t_new, :] = vc[t_new:]
        vo_ref[n, WINDOW - t_new:, :] = vn

        def write_state(h, s, n=n):
            s_out_ref[n, h] = s

        _retention(rows, q_ref, k_ref, v_ref, gr_ref, cos2, sin2,
                   lambda h, n=n: s_in_ref[n, h], write_state, or_ref, float(t_new))


def _mix_sample(qa, kva, qkr, wide, cache_k, cache_v, state, sinks, cos2, sin2, n_seq, t_new):
    nb = SAMPLE_SEQS_PER_STEP
    rows = lambda col: (lambda n: (n, col))
    cache_spec = pl.BlockSpec((nb, WINDOW, ATT_KV), lambda n: (n, 0, 0))
    state_spec = pl.BlockSpec((nb, RET_HEADS, RET_DK, RET_DV), lambda n: (n, 0, 0, 0))
    table_spec = pl.BlockSpec((t_new, RET_DK), lambda n: (0, 0))
    return pl.pallas_call(
        functools.partial(_mix_sample_kernel, t_new),
        out_shape=(
            jax.ShapeDtypeStruct((n_seq * t_new, ATT_Q), F32),
            jax.ShapeDtypeStruct((n_seq * t_new, RET_V), F32),
            jax.ShapeDtypeStruct(cache_k.shape, F32),
            jax.ShapeDtypeStruct(cache_v.shape, F32),
            jax.ShapeDtypeStruct(state.shape, F32),
        ),
        grid=(n_seq // nb,),
        in_specs=[
            pl.BlockSpec(memory_space=pltpu.SMEM),
            pl.BlockSpec((nb * t_new, ATT_Q), rows(0)),
            pl.BlockSpec((nb * t_new, ATT_KV), rows(0)),
            pl.BlockSpec((nb * t_new, ATT_KV), rows(1)),
            cache_spec, cache_spec,
            pl.BlockSpec((nb * t_new, RET_QK), rows(0)),
            pl.BlockSpec((nb * t_new, RET_QK), rows(1)),
            pl.BlockSpec((nb * t_new, RET_V), rows(0)),
            pl.BlockSpec((nb * t_new, RET_V), rows(1)),
            table_spec, table_spec,
            state_spec,
        ],
        out_specs=(
            pl.BlockSpec((nb * t_new, ATT_Q), rows(0)),
            pl.BlockSpec((nb * t_new, RET_V), rows(0)),
            cache_spec, cache_spec, state_spec,
        ),
        compiler_params=pltpu.CompilerParams(
            dimension_semantics=("parallel",), vmem_limit_bytes=V7X_VMEM_LIMIT_BYTES),
        name="mix_sample",
    )(sinks, qa, kva, kva, cache_k, cache_v, qkr, qkr, wide, wide, cos2, sin2, state)


def _merge_kernel(st, x_ref, gate_ref, npost_ref, oa_ref, or_ref, ga_ref, gr_ref,
                  wpa_ref, wpr_ref, wo_ref, o_ref):
    i = pl.program_id(0)
    oa = oa_ref[...].astype(BF16)
    orr = or_ref[...].astype(BF16)
    for c in range(D_MODEL // COL_TILE):
        cols = slice(c * COL_TILE, (c + 1) * COL_TILE)
        a = jnp.dot(oa, wpa_ref[:, cols], preferred_element_type=F32)
        r = jnp.dot(orr, wpr_ref[:, cols], preferred_element_type=F32)
        merged = (jax.nn.sigmoid(ga_ref[:, cols].astype(F32)) * a
                  + jax.nn.sigmoid(gr_ref[:, cols].astype(F32)) * r)
        part = jnp.dot(merged.astype(BF16), wo_ref[cols, :], preferred_element_type=F32)
        if c == 0:
            o_ref[...] = part
        else:
            o_ref[...] += part
    _post_residual(st, i, x_ref, gate_ref, npost_ref[1:2, :], o_ref, 1.0)


def _merge(st, x, mod, npost, o_a, o_r, wide, w_pa, w_pr, w_o):
    gate_a_block = 2 * RET_V // D_MODEL
    row_spec = pl.BlockSpec((st.tile, D_MODEL), lambda i, j: (i, 0))
    resident = lambda shape: pl.BlockSpec(shape, lambda i, j: (0, 0), pipeline_mode=pl.Buffered(1))
    return pl.pallas_call(
        functools.partial(_merge_kernel, st),
        out_shape=jax.ShapeDtypeStruct((st.n_tokens, D_MODEL), F32),
        grid=(st.n_tiles, 1),
        in_specs=[
            row_spec,
            _mod_spec(st, 5),
            pl.BlockSpec((N_SUBLAYERS, D_MODEL), lambda i, j: (0, 0)),
            pl.BlockSpec((st.tile, ATT_Q), lambda i, j: (i, 0)),
            pl.BlockSpec((st.tile, RET_V), lambda i, j: (i, 0)),
            pl.BlockSpec((st.tile, D_MODEL), lambda i, j: (i, gate_a_block)),
            pl.BlockSpec((st.tile, D_MODEL), lambda i, j: (i, gate_a_block + 1)),
            resident((ATT_Q, D_MODEL)), resident((RET_V, D_MODEL)), resident((D_MODEL, D_MODEL)),
        ],
        out_specs=row_spec,
        compiler_params=pltpu.CompilerParams(
            dimension_semantics=("parallel", "arbitrary"), vmem_limit_bytes=V7X_VMEM_LIMIT_BYTES),
        name="merge_out",
    )(x, mod, npost, o_a, o_r, wide, wide, w_pa, w_pr, w_o)


def _rotation_tables(pos):
    half = RET_DK // 2
    inv_freq = ROPE_BASE ** (-jnp.linspace(0.0, 1.0, half, dtype=F32))
    ang = pos[:, None] * inv_freq[None, :]
    cos, sin = jnp.cos(ang), jnp.sin(ang)
    return jnp.concatenate([cos, cos], axis=-1), jnp.concatenate([-sin, sin], axis=-1)


def kernel(x_prompt, x_sample, cache_k_win, cache_v_win, state_ret, c_prompt, c_sample, w_ada, b_ada,
           norm_pre, norm_post, w_in, attn_sinks, w_pa, w_pr, w_o,
           ffn1_gate, ffn1_up, ffn1_down, ffn2_gate, ffn2_up, ffn2_down):
    batch, seq, _ = x_prompt.shape
    n_seq, t_new, _ = x_sample.shape
    assert w_ada.shape[0] == 1, "single-layer step"
    assert t_new == V7X_SUBLANES and seq % TOKEN_TILE == 0 and (n_seq * t_new) % TOKEN_TILE == 0
    assert seq % MERGE_TOKEN_TILE == 0 and (n_seq * t_new) % MERGE_TOKEN_TILE == 0
    assert batch <= MOD_PAD_ROWS and n_seq % MOD_PAD_ROWS == 0

    c_all = jnp.concatenate(
        [c_sample, c_prompt, jnp.zeros((MOD_PAD_ROWS - batch, D_MODEL), F32)], axis=0)
    mod = _ada(c_all, w_ada[0], b_ada[0])

    def prompt_stream(tile):
        return _Stream(batch * seq, tile, seq, MOD_PAD_ROWS, n_seq // MOD_PAD_ROWS)

    def sample_stream(tile):
        return _Stream(n_seq * t_new, tile, t_new, tile // t_new, 0)

    prompt, prompt_m = prompt_stream(TOKEN_TILE), prompt_stream(MERGE_TOKEN_TILE)
    sample, sample_m = sample_stream(TOKEN_TILE), sample_stream(MERGE_TOKEN_TILE // 2)

    npre, npost = norm_pre[0], norm_post[0]
    sinks = attn_sinks[0]
    bf = lambda w: w[0].astype(BF16)
    w_in_b, w_pa_b, w_pr_b, w_o_b = w_in[0], bf(w_pa), bf(w_pr), bf(w_o)
    f1g, f1u, f1d = ffn1_gate[0], ffn1_up[0], ffn1_down[0]
    f2g, f2u, f2d = ffn2_gate[0], ffn2_up[0], ffn2_down[0]

    xp = x_prompt.reshape(batch * seq, D_MODEL)
    xp = _ffn(prompt, 0, xp, mod, npre, npost, f1g, f1u, f1d)
    qa, kva, qkr, wide = _proj(prompt, xp, mod, npre, w_in_b, BF16)
    o_a = _attn_prompt(qa, kva, sinks, batch, seq)
    cos_p, sin_p = _rotation_tables(jnp.arange(seq, dtype=F32))
    o_r, state_p = _ret_prompt(qkr, wide, cos_p, sin_p, batch, seq)
    xp = _merge(prompt_m, xp, mod, npost, o_a, o_r, wide, w_pa_b, w_pr_b, w_o_b)
    xp = _ffn(prompt, 2, xp, mod, npre, npost, f2g, f2u, f2d)
    kva_p = kva.reshape(batch, seq, 2 * ATT_KV)[:, seq - WINDOW:]
    kv_shape = (1, batch, WINDOW, ATT_KV_HEADS, ATT_HEAD_DIM)
    k_win_p = kva_p[..., :ATT_KV].reshape(kv_shape)
    v_win_p = kva_p[..., ATT_KV:].reshape(kv_shape)

    xs = x_sample.reshape(n_seq * t_new, D_MODEL)
    xs = _ffn(sample, 0, xs, mod, npre, npost, f1g, f1u, f1d)
    qa_s, kva_s, qkr_s, wide_s = _proj(sample, xs, mod, npre, w_in_b, F32)
    cos_s, sin_s = _rotation_tables(jnp.arange(t_new, dtype=F32) + PAST_LEN)
    o_a_s, o_r_s, k_s, v_s, state_s = _mix_sample(
        qa_s, kva_s, qkr_s, wide_s,
        cache_k_win[0].reshape(n_seq, WINDOW, ATT_KV), cache_v_win[0].reshape(n_seq, WINDOW, ATT_KV),
        state_ret[0], sinks, cos_s, sin_s, n_seq, t_new)
    xs = _merge(sample_m, xs, mod, npost, o_a_s, o_r_s, wide_s, w_pa_b, w_pr_b, w_o_b)
    xs = _ffn(sample, 2, xs, mod, npre, npost, f2g, f2u, f2d)
    kvs_shape = (1, n_seq, WINDOW, ATT_KV_HEADS, ATT_HEAD_DIM)

    return (xp.reshape(batch, seq, D_MODEL), xs.reshape(n_seq, t_new, D_MODEL),
            k_win_p, v_win_p, state_p[None],
            k_s.reshape(kvs_shape), v_s.reshape(kvs_shape), state_s[None])
```

```python
import functools
import math
from typing import NamedTuple

import jax
import jax.numpy as jnp
import numpy as np
from jax import lax
from jax.experimental import pallas as pl
from jax.experimental.pallas import tpu as pltpu

F32 = jnp.float32
BF16 = jnp.bfloat16

D_MODEL = 2048
WINDOW = 128
ATT_HEADS = 16
ATT_KV_HEADS = 4
ATT_HEAD_DIM = 64
ATT_GROUP = ATT_HEADS // ATT_KV_HEADS
ATT_Q = ATT_HEADS * ATT_HEAD_DIM
ATT_KV = ATT_KV_HEADS * ATT_HEAD_DIM
RET_HEADS = 8
RET_DK = 128
RET_DV = 256
RET_CHUNK = 128
RET_QK = RET_HEADS * RET_DK
RET_V = RET_HEADS * RET_DV
ROPE_BASE = 10000.0
D_FF = 5632
NORM_EPS = 1e-6
N_SUBLAYERS = 3
PAST_LEN = 16384
D_IN = ATT_Q + 2 * ATT_KV + 2 * RET_QK + 2 * RET_V + 2 * D_MODEL

V7X_SUBLANES = 8
V7X_BF16_ROWS = 16
V7X_VMEM_LIMIT_BYTES = 60 * 1024 * 1024

TOKEN_TILE = 1024
MERGE_TOKEN_TILE = 512
FF_TILE = 256
COL_TILE = 512
ADA_COL_TILE = 1024
SAMPLE_SEQS_PER_STEP = 8
RET_SAMPLE_SEQS = 16
MOD_PAD_ROWS = 8

_QA_TILES = ATT_Q // COL_TILE
_KVA_TILES = 2 * ATT_KV // COL_TILE
_QKR_TILES = 2 * RET_QK // COL_TILE
_WIDE_COLS = 2 * RET_V + 2 * D_MODEL
_WIDE_TILES = _WIDE_COLS // COL_TILE
_QKR_START = _QA_TILES + _KVA_TILES
_WIDE_START = _QKR_START + _QKR_TILES
assert _WIDE_START + _WIDE_TILES == D_IN // COL_TILE and _KVA_TILES == 1

_ALIBI_SLOPES = [2.0 ** (-8.0 * (h + 1) / ATT_HEADS) for h in range(ATT_HEADS)]
_RET_LOG_G = [math.log(1.0 - 2.0 ** (-5.0 - h)) for h in range(RET_HEADS)]
_ATT_SCALE = ATT_HEAD_DIM ** -0.5
_RET_K_SCALE = RET_DK ** -0.5


class _Stream(NamedTuple):
    n_tokens: int
    tile: int
    rows_per_mod: int
    mod_block_rows: int
    mod_block_base: int

    @property
    def n_tiles(self):
        return self.n_tokens // self.tile

    @property
    def sub_rows(self):
        return min(self.rows_per_mod, self.tile)

    @property
    def group_rows(self):
        return max(self.sub_rows, V7X_BF16_ROWS)

    @property
    def n_groups(self):
        return self.tile // self.group_rows

    @property
    def mods_per_group(self):
        return self.group_rows // self.sub_rows


def _rms(x):
    return x * lax.rsqrt(jnp.mean(x * x, axis=-1, keepdims=True) + NORM_EPS)


def _silu(x):
    return x * jax.nn.sigmoid(x)


def _for_groups(n_groups, fn):
    if n_groups == 1:
        fn(0)
    else:
        def body(g, carry):
            fn(g)
            return carry
        lax.fori_loop(0, n_groups, body, 0)


def _group_base(st, g):
    return 0 if st.n_groups == 1 else pl.multiple_of(g * st.group_rows, st.group_rows)


def _mod_row(st, i, g, s):
    if st.rows_per_mod >= st.tile:
        return (i * st.tile) // st.rows_per_mod
    return g * st.mods_per_group + s


def _pre_norm(st, i, x_ref, shift_ref, scale_ref, gain, h_ref):
    def group(g):
        base = _group_base(st, g)
        parts = []
        for s in range(st.mods_per_group):
            rows = pl.ds(base + s * st.sub_rows, st.sub_rows)
            m = _mod_row(st, i, g, s)
            x = x_ref[rows, :]
            sc = scale_ref[pl.ds(m, 1), :]
            sh = shift_ref[pl.ds(m, 1), :]
            parts.append(_rms(x) * gain * (1.0 + sc) + sh)
        h = parts[0] if len(parts) == 1 else jnp.concatenate(parts, axis=0)
        h_ref[pl.ds(base, st.group_rows), :] = h.astype(BF16)
    _for_groups(st.n_groups, group)


def _post_residual(st, i, x_ref, gate_ref, gain, o_ref, coeff):
    def group(g):
        base = _group_base(st, g)
        for s in range(st.mods_per_group):
            rows = pl.ds(base + s * st.sub_rows, st.sub_rows)
            m = _mod_row(st, i, g, s)
            gt = gate_ref[pl.ds(m, 1), :]
            y = gt * (_rms(o_ref[rows, :]) * gain)
            if coeff != 1.0:
                y = coeff * y
            o_ref[rows, :] = x_ref[rows, :] + y
    _for_groups(st.n_groups, group)


def _ada_kernel(c_ref, w_ref, b_ref, o_ref):
    a = _silu(c_ref[...]).astype(BF16)
    o_ref[0] = jnp.dot(a, w_ref[...].astype(BF16), preferred_element_type=F32) + b_ref[...]


def _ada(c_all, w_ada, b_ada):
    rows = c_all.shape[0]
    n_vec = N_SUBLAYERS * 3
    per_vec = D_MODEL // ADA_COL_TILE
    return pl.pallas_call(
        _ada_kernel,
        out_shape=jax.ShapeDtypeStruct((n_vec, rows, D_MODEL), F32),
        grid=(n_vec * per_vec,),
        in_specs=[
            pl.BlockSpec((rows, D_MODEL), lambda j: (0, 0)),
            pl.BlockSpec((D_MODEL, ADA_COL_TILE), lambda j: (0, j)),
            pl.BlockSpec((1, ADA_COL_TILE), lambda j: (0, j)),
        ],
        out_specs=pl.BlockSpec((1, rows, ADA_COL_TILE), lambda j: (j // per_vec, 0, j % per_vec)),
        compiler_params=pltpu.CompilerParams(
            dimension_semantics=("arbitrary",), vmem_limit_bytes=V7X_VMEM_LIMIT_BYTES),
        name="ada_mod",
    )(c_all, w_ada, b_ada.reshape(1, -1))


def _mod_spec(st, vec):
    if st.rows_per_mod >= st.tile:
        index = lambda i, j: (vec, st.mod_block_base, 0)
    else:
        index = lambda i, j: (vec, st.mod_block_base + i, 0)
    return pl.BlockSpec((None, st.mod_block_rows, D_MODEL), index)


def _ffn_kernel(st, sub, x_ref, shift_ref, scale_ref, gate_ref, npre_ref, npost_ref,
                wg_ref, wu_ref, wd_ref, o_ref, h_ref):
    i = pl.program_id(0)
    j = pl.program_id(1)

    def partial_down():
        h = h_ref[...]
        g = jnp.dot(h, wg_ref[...].astype(BF16), preferred_element_type=F32)
        u = jnp.dot(h, wu_ref[...].astype(BF16), preferred_element_type=F32)
        a = (_silu(g) * u).astype(BF16)
        return jnp.dot(a, wd_ref[...].astype(BF16), preferred_element_type=F32)

    @pl.when(j == 0)
    def _():
        _pre_norm(st, i, x_ref, shift_ref, scale_ref, npre_ref[sub:sub + 1, :], h_ref)
        o_ref[...] = partial_down()

    @pl.when(j > 0)
    def _():
        o_ref[...] += partial_down()

    @pl.when(j == pl.num_programs(1) - 1)
    def _():
        _post_residual(st, i, x_ref, gate_ref, npost_ref[sub:sub + 1, :], o_ref, 0.5)


def _ffn(st, sub, x, mod, npre, npost, wg, wu, wd):
    row_spec = pl.BlockSpec((st.tile, D_MODEL), lambda i, j: (i, 0))
    full_spec = pl.BlockSpec((N_SUBLAYERS, D_MODEL), lambda i, j: (0, 0))
    return pl.pallas_call(
        functools.partial(_ffn_kernel, st, sub),
        out_shape=jax.ShapeDtypeStruct((st.n_tokens, D_MODEL), F32),
        grid=(st.n_tiles, D_FF // FF_TILE),
        in_specs=[
            row_spec,
            _mod_spec(st, 3 * sub + 0), _mod_spec(st, 3 * sub + 1), _mod_spec(st, 3 * sub + 2),
            full_spec, full_spec,
            pl.BlockSpec((D_MODEL, FF_TILE), lambda i, j: (0, j)),
            pl.BlockSpec((D_MODEL, FF_TILE), lambda i, j: (0, j)),
            pl.BlockSpec((FF_TILE, D_MODEL), lambda i, j: (j, 0)),
        ],
        out_specs=row_spec,
        scratch_shapes=[pltpu.VMEM((st.tile, D_MODEL), BF16)],
        compiler_params=pltpu.CompilerParams(
            dimension_semantics=("parallel", "arbitrary"), vmem_limit_bytes=V7X_VMEM_LIMIT_BYTES),
        name=f"ffn{sub}",
    )(x, mod, mod, mod, npre, npost, wg, wu, wd)


def _proj_kernel(st, x_ref, shift_ref, scale_ref, npre_ref, w_ref,
                 qa_ref, kva_ref, qkr_ref, wide_ref, h_ref):
    i = pl.program_id(0)
    j = pl.program_id(1)

    @pl.when(j == 0)
    def _():
        _pre_norm(st, i, x_ref, shift_ref, scale_ref, npre_ref[1:2, :], h_ref)

    def project(dst_ref):
        dst_ref[...] = jnp.dot(h_ref[...], w_ref[...].astype(BF16),
                               preferred_element_type=F32).astype(dst_ref.dtype)

    @pl.when(j < _QA_TILES)
    def _():
        project(qa_ref)

    @pl.when(j == _QA_TILES)
    def _():
        project(kva_ref)

    @pl.when((j >= _QKR_START) & (j < _WIDE_START))
    def _():
        project(qkr_ref)

    @pl.when(j >= _WIDE_START)
    def _():
        project(wide_ref)


def _proj(st, x, mod, npre, w_in, narrow_dtype):
    n_tiles = st.n_tiles
    n = st.n_tokens
    row_spec = pl.BlockSpec((st.tile, D_MODEL), lambda i, j: (i, 0))
    out_block = (st.tile, COL_TILE)
    return pl.pallas_call(
        functools.partial(_proj_kernel, st),
        out_shape=(
            jax.ShapeDtypeStruct((n, ATT_Q), narrow_dtype),
            jax.ShapeDtypeStruct((n, 2 * ATT_KV), F32),
            jax.ShapeDtypeStruct((n, 2 * RET_QK), F32),
            jax.ShapeDtypeStruct((n, _WIDE_COLS), narrow_dtype),
        ),
        grid=(n_tiles, D_IN // COL_TILE),
        in_specs=[
            row_spec,
            _mod_spec(st, 3), _mod_spec(st, 4),
            pl.BlockSpec((N_SUBLAYERS, D_MODEL), lambda i, j: (0, 0)),
            pl.BlockSpec((D_MODEL, COL_TILE), lambda i, j: (0, j)),
        ],
        out_specs=(
            pl.BlockSpec(out_block, lambda i, j: (i, jnp.minimum(j, _QA_TILES - 1))),
            pl.BlockSpec(out_block, lambda i, j: (i, 0)),
            pl.BlockSpec(out_block, lambda i, j: (i, jnp.clip(j - _QKR_START, 0, _QKR_TILES - 1))),
            pl.BlockSpec(out_block, lambda i, j: (i, jnp.maximum(j - _WIDE_START, 0))),
        ),
        scratch_shapes=[pltpu.VMEM((st.tile, D_MODEL), BF16)],
        compiler_params=pltpu.CompilerParams(
            dimension_semantics=("parallel", "arbitrary"), vmem_limit_bytes=V7X_VMEM_LIMIT_BYTES),
        name="in_proj",
    )(x, mod, mod, npre, w_in)


def _attention(q, k2, v2, sinks_ref, first_valid_key):
    tq = q.shape[0]
    a_idx = lax.broadcasted_iota(jnp.int32, (tq, 2 * WINDOW), 0)
    b_idx = lax.broadcasted_iota(jnp.int32, (tq, 2 * WINDOW), 1)
    dist = WINDOW + a_idx - b_idx
    mask = (dist >= 0) & (dist <= WINDOW) & (b_idx >= first_valid_key)
    dist_f = jnp.where(mask, dist.astype(F32), jnp.inf)
    q = q * _ATT_SCALE
    outs = []
    for kv in range(ATT_KV_HEADS):
        cols = slice(kv * ATT_HEAD_DIM, (kv + 1) * ATT_HEAD_DIM)
        kk = k2[:, cols]
        vv = v2[:, cols]
        heads = range(kv * ATT_GROUP, (kv + 1) * ATT_GROUP)
        qg = jnp.concatenate(
            [q[:, h * ATT_HEAD_DIM:(h + 1) * ATT_HEAD_DIM] for h in heads], axis=0).astype(BF16)
        s_all = lax.dot_general(qg, kk, (((1,), (1,)), ((), ())), preferred_element_type=F32)
        probs = []
        for g, h in enumerate(heads):
            s = s_all[g * tq:(g + 1) * tq] - _ALIBI_SLOPES[h] * dist_f
            sink = sinks_ref[h]
            m = jnp.maximum(jnp.max(s, axis=-1, keepdims=True), sink)
            p = jnp.exp(s - m)
            inv = 1.0 / (jnp.sum(p, axis=-1, keepdims=True) + jnp.exp(sink - m))
            probs.append((p * inv).astype(BF16))
        o_all = jnp.dot(jnp.concatenate(probs, axis=0), vv, preferred_element_type=F32)
        outs.extend(o_all[g * tq:(g + 1) * tq] for g in range(ATT_GROUP))
    return jnp.concatenate(outs, axis=-1)


def _attn_prompt_kernel(sinks_ref, q_ref, kc_ref, vc_ref, kp_ref, vp_ref, o_ref):
    blk = pl.program_id(1)
    k2 = jnp.concatenate([kp_ref[...], kc_ref[...]], axis=0).astype(BF16)
    v2 = jnp.concatenate([vp_ref[...], vc_ref[...]], axis=0).astype(BF16)
    first_valid = jnp.where(blk == 0, WINDOW, 0)
    o_ref[...] = _attention(q_ref[...], k2, v2, sinks_ref, first_valid).astype(o_ref.dtype)


def _attn_prompt(qa, kva, sinks, batch, seq):
    nb = seq // WINDOW
    cur = lambda col: (lambda b, i: (b * nb + i, col))
    prev = lambda col: (lambda b, i: (b * nb + jnp.maximum(i - 1, 0), col))
    return pl.pallas_call(
        _attn_prompt_kernel,
        out_shape=jax.ShapeDtypeStruct((batch * seq, ATT_Q), BF16),
        grid=(batch, nb),
        in_specs=[
            pl.BlockSpec(memory_space=pltpu.SMEM),
            pl.BlockSpec((WINDOW, ATT_Q), lambda b, i: (b * nb + i, 0)),
            pl.BlockSpec((WINDOW, ATT_KV), cur(0)),
            pl.BlockSpec((WINDOW, ATT_KV), cur(1)),
            pl.BlockSpec((WINDOW, ATT_KV), prev(0)),
            pl.BlockSpec((WINDOW, ATT_KV), prev(1)),
        ],
        out_specs=pl.BlockSpec((WINDOW, ATT_Q), lambda b, i: (b * nb + i, 0)),
        compiler_params=pltpu.CompilerParams(
            dimension_semantics=("parallel", "arbitrary"), vmem_limit_bytes=V7X_VMEM_LIMIT_BYTES),
        name="attn_prompt",
    )(sinks, qa, kva, kva, kva, kva)


def _rotate(x, cos2, sin2):
    return x * cos2 + pltpu.roll(x, RET_DK // 2, axis=1) * sin2


def _ret_prompt_kernel(q_ref, k_ref, v_ref, gr_ref, cos_ref, sin_ref, o_ref, s_ref,
                       decay_ref, qw_ref, kw_ref):
    @pl.when(pl.program_id(1) == 0)
    def _():
        s_ref[...] = jnp.zeros_like(s_ref)
        row = lax.broadcasted_iota(jnp.int32, (RET_CHUNK, RET_CHUNK), 0).astype(F32)
        col = lax.broadcasted_iota(jnp.int32, (RET_CHUNK, RET_CHUNK), 1).astype(F32)
        diff = row - col
        for h in range(RET_HEADS):
            log_g = _RET_LOG_G[h]
            decay_ref[h] = jnp.where(diff >= 0, jnp.exp(jnp.maximum(diff, 0.0) * log_g), 0.0)
            qw_ref[h] = jnp.exp((row + 1.0) * log_g)
            kw_ref[h] = jnp.exp((RET_CHUNK - 1.0 - row) * log_g)

    cos2, sin2 = cos_ref[...], sin_ref[...]
    for h in range(RET_HEADS):
        qk_cols = slice(h * RET_DK, (h + 1) * RET_DK)
        v_cols = slice(h * RET_DV, (h + 1) * RET_DV)
        qh = _rotate(q_ref[:, qk_cols], cos2, sin2)
        kh = _rotate(k_ref[:, qk_cols], cos2, sin2) * _RET_K_SCALE
        vb = v_ref[:, v_cols].astype(BF16)
        s_prev = s_ref[0, h]
        scores = lax.dot_general(qh.astype(BF16), kh.astype(BF16), (((1,), (1,)), ((), ())),
                                 preferred_element_type=F32) * decay_ref[h]
        y = jnp.dot(scores.astype(BF16), vb, preferred_element_type=F32)
        y = y + jnp.dot((qh * qw_ref[h]).astype(BF16), s_prev.astype(BF16), preferred_element_type=F32)
        kt = (kh * kw_ref[h]).T.astype(BF16)
        s_ref[0, h] = (math.exp(RET_CHUNK * _RET_LOG_G[h]) * s_prev
                       + jnp.dot(kt, vb, preferred_element_type=F32))
        gate = gr_ref[:, v_cols].astype(F32)
        o_ref[:, v_cols] = (_silu(gate) * _rms(y)).astype(o_ref.dtype)


def _ret_prompt(qkr, wide, cos2, sin2, batch, seq):
    nc = seq // RET_CHUNK
    rows = lambda col: (lambda b, c: (b * nc + c, col))
    return pl.pallas_call(
        _ret_prompt_kernel,
        out_shape=(
            jax.ShapeDtypeStruct((batch * seq, RET_V), BF16),
            jax.ShapeDtypeStruct((batch, RET_HEADS, RET_DK, RET_DV), F32),
        ),
        grid=(batch, nc),
        in_specs=[
            pl.BlockSpec((RET_CHUNK, RET_QK), rows(0)),
            pl.BlockSpec((RET_CHUNK, RET_QK), rows(1)),
            pl.BlockSpec((RET_CHUNK, RET_V), rows(0)),
            pl.BlockSpec((RET_CHUNK, RET_V), rows(1)),
            pl.BlockSpec((RET_CHUNK, RET_DK), lambda b, c: (c, 0)),
            pl.BlockSpec((RET_CHUNK, RET_DK), lambda b, c: (c, 0)),
        ],
        out_specs=(
            pl.BlockSpec((RET_CHUNK, RET_V), rows(0)),
            pl.BlockSpec((1, RET_HEADS, RET_DK, RET_DV), lambda b, c: (b, 0, 0, 0)),
        ),
        scratch_shapes=[pltpu.VMEM((RET_HEADS, RET_CHUNK, RET_CHUNK), F32)] * 3,
        compiler_params=pltpu.CompilerParams(
            dimension_semantics=("parallel", "arbitrary"), vmem_limit_bytes=V7X_VMEM_LIMIT_BYTES),
        name="ret_prompt",
    )(qkr, qkr, wide, wide, cos2, sin2)


def _attn_sample_kernel(t_new, sink_ref, qa_ref, kn_ref, vn_ref, ck_ref, cv_ref,
                        oa_ref, ko_ref, vo_ref):
    n_rows = ATT_HEADS * t_new
    pair = 2 * ATT_HEAD_DIM
    row = lax.broadcasted_iota(jnp.int32, (n_rows, 2 * WINDOW), 0)
    key = lax.broadcasted_iota(jnp.int32, (n_rows, 2 * WINDOW), 1)
    head = row // t_new
    dist = WINDOW + (row - head * t_new) - key
    slope = jnp.exp2((head.astype(F32) + 1.0) * (-8.0 / ATT_HEADS))
    bias = jnp.where((dist >= 0) & (dist <= WINDOW), -slope * dist.astype(F32), -jnp.inf)
    sink = sink_ref[...]
    lower = lax.broadcasted_iota(jnp.int32, (t_new, pair), 1) < ATT_HEAD_DIM
    zero_group = jnp.zeros((t_new, pair), F32)
    zero_keys = jnp.zeros((WINDOW - t_new, ATT_KV), F32)

    for n in range(SAMPLE_SEQS_PER_STEP):
        rows = slice(n * t_new, (n + 1) * t_new)
        kc, vc = ck_ref[n], cv_ref[n]
        kn, vn = kn_ref[rows, :], vn_ref[rows, :]
        ko_ref[n, :WINDOW - t_new, :] = kc[t_new:]
        ko_ref[n, WINDOW - t_new:, :] = kn
        vo_ref[n, :WINDOW - t_new, :] = vc[t_new:]
        vo_ref[n, WINDOW - t_new:, :] = vn
        k2 = jnp.concatenate([kc, kn, zero_keys], axis=0).astype(BF16)
        v2 = jnp.concatenate([vc, vn, zero_keys], axis=0).astype(BF16)

        q = qa_ref[rows, :] * _ATT_SCALE
        q_swapped = pltpu.roll(q, ATT_HEAD_DIM, axis=1)
        blocks = []
        for h in range(ATT_HEADS):
            kv = h // ATT_GROUP
            want_lower = kv % 2 == 0
            if (h % 2 == 0) == want_lower:
                src = q[:, (h // 2) * pair:(h // 2 + 1) * pair]
            else:
                g = (h + 1) // 2 % (ATT_HEADS // 2)
                src = q_swapped[:, g * pair:(g + 1) * pair]
            piece = jnp.where(lower if want_lower else ~lower, src, 0.0)
            blocks.append(jnp.concatenate(
                [piece, zero_group] if kv // 2 == 0 else [zero_group, piece], axis=1))
        q_bd = jnp.concatenate(blocks, axis=0).astype(BF16)

        s = lax.dot_general(q_bd, k2, (((1,), (1,)), ((), ())), preferred_element_type=F32) + bias
        m = jnp.maximum(jnp.max(s, axis=-1, keepdims=True), sink)
        p = jnp.exp(s - m)
        inv = 1.0 / (jnp.sum(p, axis=-1, keepdims=True) + jnp.exp(sink - m))
        o = jnp.dot((p * inv).astype(BF16), v2, preferred_element_type=F32)

        outs = []
        for g in range(ATT_HEADS // 2):
            h0 = 2 * g
            kv = h0 // ATT_GROUP
            cols = slice((kv // 2) * pair, (kv // 2 + 1) * pair)
            a = o[h0 * t_new:(h0 + 1) * t_new, cols]
            b = o[(h0 + 1) * t_new:(h0 + 2) * t_new, cols]
            if kv % 2 == 0:
                b = pltpu.roll(b, ATT_HEAD_DIM, axis=1)
            else:
                a = pltpu.roll(a, ATT_HEAD_DIM, axis=1)
            outs.append(jnp.where(lower, a, b))
        oa_ref[rows, :] = jnp.concatenate(outs, axis=1)


def _attn_sample(qa, kva, cache_k, cache_v, sink_rows, n_seq, t_new):
    nb = SAMPLE_SEQS_PER_STEP
    rows = lambda col: (lambda n: (n, col))
    cache_spec = pl.BlockSpec((nb, WINDOW, ATT_KV), lambda n: (n, 0, 0))
    return pl.pallas_call(
        functools.partial(_attn_sample_kernel, t_new),
        out_shape=(
            jax.ShapeDtypeStruct((n_seq * t_new, ATT_Q), F32),
            jax.ShapeDtypeStruct(cache_k.shape, F32),
            jax.ShapeDtypeStruct(cache_v.shape, F32),
        ),
        grid=(n_seq // nb,),
        in_specs=[
            pl.BlockSpec((ATT_HEADS * t_new, 1), lambda n: (0, 0)),
            pl.BlockSpec((nb * t_new, ATT_Q), rows(0)),
            pl.BlockSpec((nb * t_new, ATT_KV), rows(0)),
            pl.BlockSpec((nb * t_new, ATT_KV), rows(1)),
            cache_spec, cache_spec,
        ],
        out_specs=(pl.BlockSpec((nb * t_new, ATT_Q), rows(0)), cache_spec, cache_spec),
        compiler_params=pltpu.CompilerParams(
            dimension_semantics=("parallel",), vmem_limit_bytes=V7X_VMEM_LIMIT_BYTES),
        name="attn_sample",
    )(sink_rows, qa, kva, kva, cache_k, cache_v)


def _ret_sample_kernel(t_new, log_g_ref, g_chunk_ref, q_ref, k_ref, v_ref, gr_ref, cos_ref, sin_ref,
                       s_in_ref, o_ref, s_out_ref):
    h = pl.program_id(1)
    log_g = log_g_ref[h]
    g_chunk = g_chunk_ref[h]
    n_seq = s_in_ref.shape[0]
    n_rows = n_seq * t_new
    cos2, sin2 = cos_ref[...], sin_ref[...]
    q = _rotate(q_ref[...], cos2, sin2)
    k = _rotate(k_ref[...], cos2, sin2) * _RET_K_SCALE
    vb = v_ref[...].astype(BF16)

    row = lax.broadcasted_iota(jnp.int32, (n_rows, n_rows), 0)
    col = lax.broadcasted_iota(jnp.int32, (n_rows, n_rows), 1)
    row_seq, col_seq = row // t_new, col // t_new
    diff = (row - col).astype(F32)
    decay = jnp.where((row_seq == col_seq) & (row >= col),
                      jnp.exp(jnp.maximum(diff, 0.0) * log_g), 0.0)
    scores = lax.dot_general(q.astype(BF16), k.astype(BF16), (((1,), (1,)), ((), ())),
                             preferred_element_type=F32) * decay
    y = jnp.dot(scores.astype(BF16), vb, preferred_element_type=F32)

    t = (row - row_seq * t_new).astype(F32)
    qw = q * jnp.exp((t + 1.0) * log_g)
    y_cross = [jnp.dot(qw[n * t_new:(n + 1) * t_new].astype(BF16), s_in_ref[n].astype(BF16),
                       preferred_element_type=F32) for n in range(n_seq)]
    y = y + jnp.concatenate(y_cross, axis=0)

    kt = (k * jnp.exp((t_new - 1.0 - t) * log_g)).T
    lhs = jnp.concatenate([jnp.where(col_seq == n, kt, 0.0) for n in range(n_seq)], axis=0)
    kv = jnp.dot(lhs.astype(BF16), vb, preferred_element_type=F32)
    for n in range(n_seq):
        s_out_ref[n] = g_chunk * s_in_ref[n] + kv[n * RET_DK:(n + 1) * RET_DK]

    o_ref[...] = _silu(gr_ref[...]) * _rms(y)


def _ret_sample(qkr, wide, cos2, sin2, state, n_seq, t_new):
    nb = RET_SAMPLE_SEQS
    n_rows = nb * t_new
    assert n_rows == RET_DK
    log_g = jnp.asarray(_RET_LOG_G, F32)
    g_chunk = jnp.asarray([math.exp(t_new * g) for g in _RET_LOG_G], F32)
    smem = pl.BlockSpec(memory_space=pltpu.SMEM)
    table_spec = pl.BlockSpec((n_rows, RET_DK), lambda g, h: (0, 0))
    state_spec = pl.BlockSpec((nb, None, RET_DK, RET_DV), lambda g, h: (g, h, 0, 0))
    return pl.pallas_call(
        functools.partial(_ret_sample_kernel, t_new),
        out_shape=(
            jax.ShapeDtypeStruct((n_seq * t_new, RET_V), F32),
            jax.ShapeDtypeStruct(state.shape, F32),
        ),
        grid=(n_seq // nb, RET_HEADS),
        in_specs=[
            smem, smem,
            pl.BlockSpec((n_rows, RET_DK), lambda g, h: (g, h)),
            pl.BlockSpec((n_rows, RET_DK), lambda g, h: (g, RET_HEADS + h)),
            pl.BlockSpec((n_rows, RET_DV), lambda g, h: (g, h)),
            pl.BlockSpec((n_rows, RET_DV), lambda g, h: (g, RET_HEADS + h)),
            table_spec, table_spec,
            state_spec,
        ],
        out_specs=(pl.BlockSpec((n_rows, RET_DV), lambda g, h: (g, h)), state_spec),
        compiler_params=pltpu.CompilerParams(
            dimension_semantics=("parallel", "arbitrary"), vmem_limit_bytes=V7X_VMEM_LIMIT_BYTES),
        name="ret_sample",
    )(log_g, g_chunk, qkr, qkr, wide, wide, jnp.tile(cos2, (nb, 1)), jnp.tile(sin2, (nb, 1)), state)


def _merge_kernel(st, x_ref, gate_ref, npost_ref, oa_ref, or_ref, ga_ref, gr_ref,
                  wpa_ref, wpr_ref, wo_ref, o_ref):
    i = pl.program_id(0)
    oa = oa_ref[...].astype(BF16)
    orr = or_ref[...].astype(BF16)
    for c in range(D_MODEL // COL_TILE):
        cols = slice(c * COL_TILE, (c + 1) * COL_TILE)
        a = jnp.dot(oa, wpa_ref[:, cols], preferred_element_type=F32)
        r = jnp.dot(orr, wpr_ref[:, cols], preferred_element_type=F32)
        merged = (jax.nn.sigmoid(ga_ref[:, cols].astype(F32)) * a
                  + jax.nn.sigmoid(gr_ref[:, cols].astype(F32)) * r)
        part = jnp.dot(merged.astype(BF16), wo_ref[cols, :], preferred_element_type=F32)
        if c == 0:
            o_ref[...] = part
        else:
            o_ref[...] += part
    _post_residual(st, i, x_ref, gate_ref, npost_ref[1:2, :], o_ref, 1.0)


def _merge(st, x, mod, npost, o_a, o_r, wide, w_pa, w_pr, w_o):
    gate_a_block = 2 * RET_V // D_MODEL
    row_spec = pl.BlockSpec((st.tile, D_MODEL), lambda i, j: (i, 0))
    resident = lambda shape: pl.BlockSpec(shape, lambda i, j: (0, 0), pipeline_mode=pl.Buffered(1))
    return pl.pallas_call(
        functools.partial(_merge_kernel, st),
        out_shape=jax.ShapeDtypeStruct((st.n_tokens, D_MODEL), F32),
        grid=(st.n_tiles, 1),
        in_specs=[
            row_spec,
            _mod_spec(st, 5),
            pl.BlockSpec((N_SUBLAYERS, D_MODEL), lambda i, j: (0, 0)),
            pl.BlockSpec((st.tile, ATT_Q), lambda i, j: (i, 0)),
            pl.BlockSpec((st.tile, RET_V), lambda i, j: (i, 0)),
            pl.BlockSpec((st.tile, D_MODEL), lambda i, j: (i, gate_a_block)),
            pl.BlockSpec((st.tile, D_MODEL), lambda i, j: (i, gate_a_block + 1)),
            resident((ATT_Q, D_MODEL)), resident((RET_V, D_MODEL)), resident((D_MODEL, D_MODEL)),
        ],
        out_specs=row_spec,
        compiler_params=pltpu.CompilerParams(
            dimension_semantics=("parallel", "arbitrary"), vmem_limit_bytes=V7X_VMEM_LIMIT_BYTES),
        name="merge_out",
    )(x, mod, npost, o_a, o_r, wide, wide, w_pa, w_pr, w_o)


def _rotation_tables(pos):
    half = RET_DK // 2
    inv_freq = ROPE_BASE ** (-jnp.linspace(0.0, 1.0, half, dtype=F32))
    ang = pos[:, None] * inv_freq[None, :]
    cos, sin = jnp.cos(ang), jnp.sin(ang)
    return jnp.concatenate([cos, cos], axis=-1), jnp.concatenate([-sin, sin], axis=-1)


def kernel(x_prompt, x_sample, cache_k_win, cache_v_win, state_ret, c_prompt, c_sample, w_ada, b_ada,
           norm_pre, norm_post, w_in, attn_sinks, w_pa, w_pr, w_o,
           ffn1_gate, ffn1_up, ffn1_down, ffn2_gate, ffn2_up, ffn2_down):
    batch, seq, _ = x_prompt.shape
    n_seq, t_new, _ = x_sample.shape
    assert w_ada.shape[0] == 1, "single-layer step"
    assert t_new == V7X_SUBLANES and seq % TOKEN_TILE == 0 and (n_seq * t_new) % TOKEN_TILE == 0
    assert seq % MERGE_TOKEN_TILE == 0 and (n_seq * t_new) % MERGE_TOKEN_TILE == 0
    assert batch <= MOD_PAD_ROWS and n_seq % MOD_PAD_ROWS == 0

    c_all = jnp.concatenate(
        [c_sample, c_prompt, jnp.zeros((MOD_PAD_ROWS - batch, D_MODEL), F32)], axis=0)
    mod = _ada(c_all, w_ada[0], b_ada[0])

    def prompt_stream(tile):
        return _Stream(batch * seq, tile, seq, MOD_PAD_ROWS, n_seq // MOD_PAD_ROWS)

    def sample_stream(tile):
        return _Stream(n_seq * t_new, tile, t_new, tile // t_new, 0)

    prompt, prompt_m = prompt_stream(TOKEN_TILE), prompt_stream(MERGE_TOKEN_TILE)
    sample, sample_m = sample_stream(TOKEN_TILE), sample_stream(MERGE_TOKEN_TILE // 2)

    npre, npost = norm_pre[0], norm_post[0]
    sinks = attn_sinks[0]
    bf = lambda w: w[0].astype(BF16)
    w_in_b, w_pa_b, w_pr_b, w_o_b = w_in[0], bf(w_pa), bf(w_pr), bf(w_o)
    f1g, f1u, f1d = ffn1_gate[0], ffn1_up[0], ffn1_down[0]
    f2g, f2u, f2d = ffn2_gate[0], ffn2_up[0], ffn2_down[0]

    xp = x_prompt.reshape(batch * seq, D_MODEL)
    xp = _ffn(prompt, 0, xp, mod, npre, npost, f1g, f1u, f1d)
    qa, kva, qkr, wide = _proj(prompt, xp, mod, npre, w_in_b, BF16)
    o_a = _attn_prompt(qa, kva, sinks, batch, seq)
    cos_p, sin_p = _rotation_tables(jnp.arange(seq, dtype=F32))
    o_r, state_p = _ret_prompt(qkr, wide, cos_p, sin_p, batch, seq)
    xp = _merge(prompt_m, xp, mod, npost, o_a, o_r, wide, w_pa_b, w_pr_b, w_o_b)
    xp = _ffn(prompt, 2, xp, mod, npre, npost, f2g, f2u, f2d)
    kva_p = kva.reshape(batch, seq, 2 * ATT_KV)[:, seq - WINDOW:]
    kv_shape = (1, batch, WINDOW, ATT_KV_HEADS, ATT_HEAD_DIM)
    k_win_p = kva_p[..., :ATT_KV].reshape(kv_shape)
    v_win_p = kva_p[..., ATT_KV:].reshape(kv_shape)

    xs = x_sample.reshape(n_seq * t_new, D_MODEL)
    xs = _ffn(sample, 0, xs, mod, npre, npost, f1g, f1u, f1d)
    qa_s, kva_s, qkr_s, wide_s = _proj(sample, xs, mod, npre, w_in_b, F32)
    cos_s, sin_s = _rotation_tables(jnp.arange(t_new, dtype=F32) + PAST_LEN)
    sink_rows = jnp.repeat(sinks, t_new)[:, None]
    o_a_s, k_s, v_s = _attn_sample(
        qa_s, kva_s,
        cache_k_win[0].reshape(n_seq, WINDOW, ATT_KV), cache_v_win[0].reshape(n_seq, WINDOW, ATT_KV),
        sink_rows, n_seq, t_new)
    o_r_s, state_s = _ret_sample(qkr_s, wide_s, cos_s, sin_s, state_ret[0], n_seq, t_new)
    xs = _merge(sample_m, xs, mod, npost, o_a_s, o_r_s, wide_s, w_pa_b, w_pr_b, w_o_b)
    xs = _ffn(sample, 2, xs, mod, npre, npost, f2g, f2u, f2d)
    kvs_shape = (1, n_seq, WINDOW, ATT_KV_HEADS, ATT_HEAD_DIM)

    return (xp.reshape(batch, seq, D_MODEL), xs.reshape(n_seq, t_new, D_MODEL),
            k_win_p, v_win_p, state_p[None],
            k_s.reshape(kvs_shape), v_s.reshape(kvs_shape), state_s[None])
```

```python
import functools
import math
from typing import NamedTuple

import jax
import jax.numpy as jnp
import numpy as np
from jax import lax
from jax.experimental import pallas as pl
from jax.experimental.pallas import tpu as pltpu

F32 = jnp.float32
BF16 = jnp.bfloat16

D_MODEL = 2048
WINDOW = 128
ATT_HEADS = 16
ATT_KV_HEADS = 4
ATT_HEAD_DIM = 64
ATT_GROUP = ATT_HEADS // ATT_KV_HEADS
ATT_Q = ATT_HEADS * ATT_HEAD_DIM
ATT_KV = ATT_KV_HEADS * ATT_HEAD_DIM
RET_HEADS = 8
RET_DK = 128
RET_DV = 256
RET_CHUNK = 128
RET_QK = RET_HEADS * RET_DK
RET_V = RET_HEADS * RET_DV
ROPE_BASE = 10000.0
D_FF = 5632
NORM_EPS = 1e-6
N_SUBLAYERS = 3
PAST_LEN = 16384
D_IN = ATT_Q + 2 * ATT_KV + 2 * RET_QK + 2 * RET_V + 2 * D_MODEL

V7X_SUBLANES = 8
V7X_BF16_ROWS = 16
V7X_VMEM_LIMIT_BYTES = 60 * 1024 * 1024

TOKEN_TILE = 1024
MERGE_TOKEN_TILE = 512
FF_TILE_F32 = 256
FF_TILE_BF16 = 512
COL_TILE = 512
ADA_COL_TILE = 1024
MIXER_BLOCKS_PER_STEP = 2
SAMPLE_SEQS_PER_STEP = 8
RET_SAMPLE_SEQS = 16
MOD_PAD_ROWS = 8

_QA_TILES = ATT_Q // COL_TILE
_KVA_TILES = 2 * ATT_KV // COL_TILE
_QKR_TILES = 2 * RET_QK // COL_TILE
_WIDE_COLS = 2 * RET_V + 2 * D_MODEL
_WIDE_TILES = _WIDE_COLS // COL_TILE
_QKR_START = _QA_TILES + _KVA_TILES
_WIDE_START = _QKR_START + _QKR_TILES
assert _WIDE_START + _WIDE_TILES == D_IN // COL_TILE and _KVA_TILES == 1

_ALIBI_SLOPES = [2.0 ** (-8.0 * (h + 1) / ATT_HEADS) for h in range(ATT_HEADS)]
_RET_LOG_G = [math.log(1.0 - 2.0 ** (-5.0 - h)) for h in range(RET_HEADS)]
_ATT_SCALE = ATT_HEAD_DIM ** -0.5
_RET_K_SCALE = RET_DK ** -0.5


class _Stream(NamedTuple):
    n_tokens: int
    tile: int
    rows_per_mod: int
    mod_block_rows: int
    mod_block_base: int

    @property
    def n_tiles(self):
        return self.n_tokens // self.tile

    @property
    def sub_rows(self):
        return min(self.rows_per_mod, self.tile)

    @property
    def group_rows(self):
        return max(self.sub_rows, V7X_BF16_ROWS)

    @property
    def n_groups(self):
        return self.tile // self.group_rows

    @property
    def mods_per_group(self):
        return self.group_rows // self.sub_rows


def _rms(x):
    return x * lax.rsqrt(jnp.mean(x * x, axis=-1, keepdims=True) + NORM_EPS)


def _silu(x):
    return x * jax.nn.sigmoid(x)


def _for_groups(n_groups, fn):
    if n_groups == 1:
        fn(0)
    else:
        def body(g, carry):
            fn(g)
            return carry
        lax.fori_loop(0, n_groups, body, 0)


def _group_base(st, g):
    return 0 if st.n_groups == 1 else pl.multiple_of(g * st.group_rows, st.group_rows)


def _mod_row(st, i, g, s):
    if st.rows_per_mod >= st.tile:
        return (i * st.tile) // st.rows_per_mod
    return g * st.mods_per_group + s


def _pre_norm(st, i, x_ref, shift_ref, scale_ref, gain, h_ref):
    def group(g):
        base = _group_base(st, g)
        parts = []
        for s in range(st.mods_per_group):
            rows = pl.ds(base + s * st.sub_rows, st.sub_rows)
            m = _mod_row(st, i, g, s)
            x = x_ref[rows, :]
            row_gain = gain * (1.0 + scale_ref[pl.ds(m, 1), :])
            sh = shift_ref[pl.ds(m, 1), :]
            parts.append(_rms(x) * row_gain + sh)
        h = parts[0] if len(parts) == 1 else jnp.concatenate(parts, axis=0)
        h_ref[pl.ds(base, st.group_rows), :] = h.astype(BF16)
    _for_groups(st.n_groups, group)


def _post_residual(st, i, x_ref, gate_ref, gain, o_ref, coeff):
    def group(g):
        base = _group_base(st, g)
        for s in range(st.mods_per_group):
            rows = pl.ds(base + s * st.sub_rows, st.sub_rows)
            m = _mod_row(st, i, g, s)
            row_gain = gate_ref[pl.ds(m, 1), :] * gain
            if coeff != 1.0:
                row_gain = coeff * row_gain
            o_ref[rows, :] = x_ref[rows, :] + _rms(o_ref[rows, :]) * row_gain
    _for_groups(st.n_groups, group)


def _ada_kernel(c_ref, w_ref, b_ref, o_ref):
    a = _silu(c_ref[...]).astype(BF16)
    o_ref[0] = jnp.dot(a, w_ref[...].astype(BF16), preferred_element_type=F32) + b_ref[...]


def _ada(c_all, w_ada, b_ada):
    rows = c_all.shape[0]
    n_vec = N_SUBLAYERS * 3
    per_vec = D_MODEL // ADA_COL_TILE
    return pl.pallas_call(
        _ada_kernel,
        out_shape=jax.ShapeDtypeStruct((n_vec, rows, D_MODEL), F32),
        grid=(n_vec * per_vec,),
        in_specs=[
            pl.BlockSpec((rows, D_MODEL), lambda j: (0, 0)),
            pl.BlockSpec((D_MODEL, ADA_COL_TILE), lambda j: (0, j)),
            pl.BlockSpec((1, ADA_COL_TILE), lambda j: (0, j)),
        ],
        out_specs=pl.BlockSpec((1, rows, ADA_COL_TILE), lambda j: (j // per_vec, 0, j % per_vec)),
        compiler_params=pltpu.CompilerParams(
            dimension_semantics=("arbitrary",), vmem_limit_bytes=V7X_VMEM_LIMIT_BYTES),
        name="ada_mod",
    )(c_all, w_ada, b_ada.reshape(1, -1))


def _mod_spec(st, vec):
    if st.rows_per_mod >= st.tile:
        index = lambda i, j: (vec, st.mod_block_base, 0)
    else:
        index = lambda i, j: (vec, st.mod_block_base + i, 0)
    return pl.BlockSpec((None, st.mod_block_rows, D_MODEL), index)


def _ffn_kernel(st, sub, emit_bf16, x_ref, shift_ref, scale_ref, gate_ref, npre_ref, npost_ref,
                wg_ref, wu_ref, wd_ref, o_ref, *rest):
    i = pl.program_id(0)
    j = pl.program_id(1)
    h_ref = rest[-1]

    def partial_down():
        h = h_ref[...]
        wg, wu, wd = wg_ref[...].astype(BF16), wu_ref[...].astype(BF16), wd_ref[...].astype(BF16)
        if emit_bf16:
            wg_out_ref, wu_out_ref, wd_out_ref = rest[:3]
            wg_out_ref[...] = wg
            wu_out_ref[...] = wu
            wd_out_ref[...] = wd
        g = jnp.dot(h, wg, preferred_element_type=F32)
        u = jnp.dot(h, wu, preferred_element_type=F32)
        a = (_silu(g) * u).astype(BF16)
        return jnp.dot(a, wd, preferred_element_type=F32)

    @pl.when(j == 0)
    def _():
        _pre_norm(st, i, x_ref, shift_ref, scale_ref, npre_ref[sub:sub + 1, :], h_ref)
        o_ref[...] = partial_down()

    @pl.when(j > 0)
    def _():
        o_ref[...] += partial_down()

    @pl.when(j == pl.num_programs(1) - 1)
    def _():
        _post_residual(st, i, x_ref, gate_ref, npost_ref[sub:sub + 1, :], o_ref, 0.5)


def _ffn(st, sub, x, mod, npre, npost, wg, wu, wd, emit_bf16=False):
    tiled = wg.ndim == 3
    ff_tile = FF_TILE_BF16 if tiled else FF_TILE_F32
    row_spec = pl.BlockSpec((st.tile, D_MODEL), lambda i, j: (i, 0))
    full_spec = pl.BlockSpec((N_SUBLAYERS, D_MODEL), lambda i, j: (0, 0))
    if tiled:
        up_spec = pl.BlockSpec((None, D_MODEL, ff_tile), lambda i, j: (j, 0, 0))
    else:
        up_spec = pl.BlockSpec((D_MODEL, ff_tile), lambda i, j: (0, j))
    down_spec = pl.BlockSpec((ff_tile, D_MODEL), lambda i, j: (j, 0))
    out_shape = [jax.ShapeDtypeStruct((st.n_tokens, D_MODEL), F32)]
    out_specs = [row_spec]
    if emit_bf16:
        assert st.n_tiles == 1 and not tiled
        per = FF_TILE_BF16 // ff_tile
        up_shape = jax.ShapeDtypeStruct((D_FF // FF_TILE_BF16, D_MODEL, FF_TILE_BF16), BF16)
        up_out = pl.BlockSpec((None, D_MODEL, ff_tile), lambda i, j: (j // per, 0, j % per))
        out_shape += [up_shape, up_shape, jax.ShapeDtypeStruct((D_FF, D_MODEL), BF16)]
        out_specs += [up_out, up_out, down_spec]
    outs = pl.pallas_call(
        functools.partial(_ffn_kernel, st, sub, emit_bf16),
        out_shape=out_shape,
        grid=(st.n_tiles, D_FF // ff_tile),
        in_specs=[
            row_spec,
            _mod_spec(st, 3 * sub + 0), _mod_spec(st, 3 * sub + 1), _mod_spec(st, 3 * sub + 2),
            full_spec, full_spec,
            up_spec, up_spec, down_spec,
        ],
        out_specs=out_specs,
        scratch_shapes=[pltpu.VMEM((st.tile, D_MODEL), BF16)],
        compiler_params=pltpu.CompilerParams(
            dimension_semantics=("parallel", "arbitrary"), vmem_limit_bytes=V7X_VMEM_LIMIT_BYTES),
        name=f"ffn{sub}",
    )(x, mod, mod, mod, npre, npost, wg, wu, wd)
    return outs if emit_bf16 else outs[0]


def _proj_kernel(st, emit_bf16, x_ref, shift_ref, scale_ref, npre_ref, w_ref,
                 qa_ref, kva_ref, qkr_ref, wide_ref, *rest):
    i = pl.program_id(0)
    j = pl.program_id(1)
    h_ref = rest[-1]

    @pl.when(j == 0)
    def _():
        _pre_norm(st, i, x_ref, shift_ref, scale_ref, npre_ref[1:2, :], h_ref)

    def project(dst_ref):
        w = w_ref[...].astype(BF16)
        if emit_bf16:
            rest[0][...] = w
        dst_ref[...] = jnp.dot(h_ref[...], w, preferred_element_type=F32).astype(dst_ref.dtype)

    @pl.when(j < _QA_TILES)
    def _():
        project(qa_ref)

    @pl.when(j == _QA_TILES)
    def _():
        project(kva_ref)

    @pl.when((j >= _QKR_START) & (j < _WIDE_START))
    def _():
        project(qkr_ref)

    @pl.when(j >= _WIDE_START)
    def _():
        project(wide_ref)


def _proj(st, x, mod, npre, w_in, narrow_dtype, emit_bf16=False):
    n = st.n_tokens
    tiled = w_in.ndim == 3
    row_spec = pl.BlockSpec((st.tile, D_MODEL), lambda i, j: (i, 0))
    out_block = (st.tile, COL_TILE)
    tile_spec = pl.BlockSpec((None, D_MODEL, COL_TILE), lambda i, j: (j, 0, 0))
    out_shape = [
        jax.ShapeDtypeStruct((n, ATT_Q), narrow_dtype),
        jax.ShapeDtypeStruct((n, 2 * ATT_KV), F32),
        jax.ShapeDtypeStruct((n, 2 * RET_QK), F32),
        jax.ShapeDtypeStruct((n, _WIDE_COLS), narrow_dtype),
    ]
    out_specs = [
        pl.BlockSpec(out_block, lambda i, j: (i, jnp.minimum(j, _QA_TILES - 1))),
        pl.BlockSpec(out_block, lambda i, j: (i, 0)),
        pl.BlockSpec(out_block, lambda i, j: (i, jnp.clip(j - _QKR_START, 0, _QKR_TILES - 1))),
        pl.BlockSpec(out_block, lambda i, j: (i, jnp.maximum(j - _WIDE_START, 0))),
    ]
    if emit_bf16:
        assert st.n_tiles == 1 and not tiled
        out_shape.append(jax.ShapeDtypeStruct((D_IN // COL_TILE, D_MODEL, COL_TILE), BF16))
        out_specs.append(tile_spec)
    outs = pl.pallas_call(
        functools.partial(_proj_kernel, st, emit_bf16),
        out_shape=out_shape,
        grid=(st.n_tiles, D_IN // COL_TILE),
        in_specs=[
            row_spec,
            _mod_spec(st, 3), _mod_spec(st, 4),
            pl.BlockSpec((N_SUBLAYERS, D_MODEL), lambda i, j: (0, 0)),
            tile_spec if tiled else pl.BlockSpec((D_MODEL, COL_TILE), lambda i, j: (0, j)),
        ],
        out_specs=out_specs,
        scratch_shapes=[pltpu.VMEM((st.tile, D_MODEL), BF16)],
        compiler_params=pltpu.CompilerParams(
            dimension_semantics=("parallel", "arbitrary"), vmem_limit_bytes=V7X_VMEM_LIMIT_BYTES),
        name="in_proj",
    )(x, mod, mod, npre, w_in)
    return outs


def _attention(q, k2, v2, sinks_ref, first_valid_key):
    tq = q.shape[0]
    a_idx = lax.broadcasted_iota(jnp.int32, (tq, 2 * WINDOW), 0)
    b_idx = lax.broadcasted_iota(jnp.int32, (tq, 2 * WINDOW), 1)
    dist = WINDOW + a_idx - b_idx
    mask = (dist >= 0) & (dist <= WINDOW) & (b_idx >= first_valid_key)
    dist_f = jnp.where(mask, dist.astype(F32), jnp.inf)
    q = q * _ATT_SCALE
    outs = []
    for kv in range(ATT_KV_HEADS):
        cols = slice(kv * ATT_HEAD_DIM, (kv + 1) * ATT_HEAD_DIM)
        kk = k2[:, cols]
        vv = v2[:, cols]
        heads = range(kv * ATT_GROUP, (kv + 1) * ATT_GROUP)
        qg = jnp.concatenate(
            [q[:, h * ATT_HEAD_DIM:(h + 1) * ATT_HEAD_DIM] for h in heads], axis=0).astype(BF16)
        s_all = lax.dot_general(qg, kk, (((1,), (1,)), ((), ())), preferred_element_type=F32)
        probs = []
        for g, h in enumerate(heads):
            s = s_all[g * tq:(g + 1) * tq] - _ALIBI_SLOPES[h] * dist_f
            sink = sinks_ref[h]
            m = jnp.maximum(jnp.max(s, axis=-1, keepdims=True), sink)
            p = jnp.exp(s - m)
            inv = 1.0 / (jnp.sum(p, axis=-1, keepdims=True) + jnp.exp(sink - m))
            probs.append((p * inv).astype(BF16))
        o_all = jnp.dot(jnp.concatenate(probs, axis=0), vv, preferred_element_type=F32)
        outs.extend(o_all[g * tq:(g + 1) * tq] for g in range(ATT_GROUP))
    return jnp.concatenate(outs, axis=-1)


def _attn_prompt_kernel(sinks_ref, q_ref, kc_ref, vc_ref, kp_ref, vp_ref, o_ref):
    step = pl.program_id(1)
    k_tiles = [kp_ref[...]] + [kc_ref[r * WINDOW:(r + 1) * WINDOW, :] for r in range(MIXER_BLOCKS_PER_STEP)]
    v_tiles = [vp_ref[...]] + [vc_ref[r * WINDOW:(r + 1) * WINDOW, :] for r in range(MIXER_BLOCKS_PER_STEP)]
    for r in range(MIXER_BLOCKS_PER_STEP):
        rows = slice(r * WINDOW, (r + 1) * WINDOW)
        k2 = jnp.concatenate(k_tiles[r:r + 2], axis=0).astype(BF16)
        v2 = jnp.concatenate(v_tiles[r:r + 2], axis=0).astype(BF16)
        first_valid = jnp.where(step == 0, WINDOW, 0) if r == 0 else 0
        o_ref[rows, :] = _attention(q_ref[rows, :], k2, v2, sinks_ref, first_valid).astype(o_ref.dtype)


def _attn_prompt(qa, kva, sinks, batch, seq):
    per = MIXER_BLOCKS_PER_STEP
    ns = seq // (per * WINDOW)
    cur = lambda col: (lambda b, i: (b * ns + i, col))
    prev = lambda col: (lambda b, i: (b * ns * per + jnp.maximum(i * per - 1, 0), col))
    return pl.pallas_call(
        _attn_prompt_kernel,
        out_shape=jax.ShapeDtypeStruct((batch * seq, ATT_Q), BF16),
        grid=(batch, ns),
        in_specs=[
            pl.BlockSpec(memory_space=pltpu.SMEM),
            pl.BlockSpec((per * WINDOW, ATT_Q), lambda b, i: (b * ns + i, 0)),
            pl.BlockSpec((per * WINDOW, ATT_KV), cur(0)),
            pl.BlockSpec((per * WINDOW, ATT_KV), cur(1)),
            pl.BlockSpec((WINDOW, ATT_KV), prev(0)),
            pl.BlockSpec((WINDOW, ATT_KV), prev(1)),
        ],
        out_specs=pl.BlockSpec((per * WINDOW, ATT_Q), lambda b, i: (b * ns + i, 0)),
        compiler_params=pltpu.CompilerParams(
            dimension_semantics=("parallel", "arbitrary"), vmem_limit_bytes=V7X_VMEM_LIMIT_BYTES),
        name="attn_prompt",
    )(sinks, qa, kva, kva, kva, kva)


def _rotate(x, cos2, sin2):
    return x * cos2 + pltpu.roll(x, RET_DK // 2, axis=1) * sin2


def _ret_prompt_kernel(q_ref, k_ref, v_ref, gr_ref, cos_ref, sin_ref, o_ref, s_ref,
                       decay_ref, qw_ref, kw_ref):
    @pl.when(pl.program_id(1) == 0)
    def _():
        s_ref[...] = jnp.zeros_like(s_ref)
        row = lax.broadcasted_iota(jnp.int32, (RET_CHUNK, RET_CHUNK), 0).astype(F32)
        col = lax.broadcasted_iota(jnp.int32, (RET_CHUNK, RET_CHUNK), 1).astype(F32)
        diff = row - col
        for h in range(RET_HEADS):
            log_g = _RET_LOG_G[h]
            decay_ref[h] = jnp.where(diff >= 0, jnp.exp(jnp.maximum(diff, 0.0) * log_g), 0.0)
            qw_ref[h] = jnp.exp((row + 1.0) * log_g)
            kw_ref[h] = jnp.exp((RET_CHUNK - 1.0 - row) * log_g)

    for r in range(MIXER_BLOCKS_PER_STEP):
        rows = slice(r * RET_CHUNK, (r + 1) * RET_CHUNK)
        cos2, sin2 = cos_ref[rows, :], sin_ref[rows, :]
        for h in range(RET_HEADS):
            qk_cols = slice(h * RET_DK, (h + 1) * RET_DK)
            v_cols = slice(h * RET_DV, (h + 1) * RET_DV)
            qh = _rotate(q_ref[rows, qk_cols], cos2, sin2)
            kh = _rotate(k_ref[rows, qk_cols], cos2, sin2) * _RET_K_SCALE
            vb = v_ref[rows, v_cols].astype(BF16)
            s_prev = s_ref[0, h]
            scores = lax.dot_general(qh.astype(BF16), kh.astype(BF16), (((1,), (1,)), ((), ())),
                                     preferred_element_type=F32) * decay_ref[h]
            y = jnp.dot(scores.astype(BF16), vb, preferred_element_type=F32)
            y = y + jnp.dot((qh * qw_ref[h]).astype(BF16), s_prev.astype(BF16),
                            preferred_element_type=F32)
            kt = (kh * kw_ref[h]).T.astype(BF16)
            s_ref[0, h] = (math.exp(RET_CHUNK * _RET_LOG_G[h]) * s_prev
                           + jnp.dot(kt, vb, preferred_element_type=F32))
            gate = gr_ref[rows, v_cols].astype(F32)
            o_ref[rows, v_cols] = (_silu(gate) * _rms(y)).astype(o_ref.dtype)


def _ret_prompt(qkr, wide, cos2, sin2, batch, seq):
    step_rows = MIXER_BLOCKS_PER_STEP * RET_CHUNK
    nc = seq // step_rows
    rows = lambda col: (lambda b, c: (b * nc + c, col))
    return pl.pallas_call(
        _ret_prompt_kernel,
        out_shape=(
            jax.ShapeDtypeStruct((batch * seq, RET_V), BF16),
            jax.ShapeDtypeStruct((batch, RET_HEADS, RET_DK, RET_DV), F32),
        ),
        grid=(batch, nc),
        in_specs=[
            pl.BlockSpec((step_rows, RET_QK), rows(0)),
            pl.BlockSpec((step_rows, RET_QK), rows(1)),
            pl.BlockSpec((step_rows, RET_V), rows(0)),
            pl.BlockSpec((step_rows, RET_V), rows(1)),
            pl.BlockSpec((step_rows, RET_DK), lambda b, c: (c, 0)),
            pl.BlockSpec((step_rows, RET_DK), lambda b, c: (c, 0)),
        ],
        out_specs=(
            pl.BlockSpec((step_rows, RET_V), rows(0)),
            pl.BlockSpec((1, RET_HEADS, RET_DK, RET_DV), lambda b, c: (b, 0, 0, 0)),
        ),
        scratch_shapes=[pltpu.VMEM((RET_HEADS, RET_CHUNK, RET_CHUNK), F32)] * 3,
        compiler_params=pltpu.CompilerParams(
            dimension_semantics=("parallel", "arbitrary"), vmem_limit_bytes=V7X_VMEM_LIMIT_BYTES),
        name="ret_prompt",
    )(qkr, qkr, wide, wide, cos2, sin2)


def _attn_sample_kernel(t_new, sink_ref, qa_ref, kn_ref, vn_ref, ck_ref, cv_ref,
                        oa_ref, ko_ref, vo_ref):
    n_rows = ATT_HEADS * t_new
    pair = 2 * ATT_HEAD_DIM
    row = lax.broadcasted_iota(jnp.int32, (n_rows, 2 * WINDOW), 0)
    key = lax.broadcasted_iota(jnp.int32, (n_rows, 2 * WINDOW), 1)
    head = row // t_new
    dist = WINDOW + (row - head * t_new) - key
    slope = jnp.exp2((head.astype(F32) + 1.0) * (-8.0 / ATT_HEADS))
    bias = jnp.where((dist >= 0) & (dist <= WINDOW), -slope * dist.astype(F32), -jnp.inf)
    sink = sink_ref[...]
    lower = lax.broadcasted_iota(jnp.int32, (t_new, pair), 1) < ATT_HEAD_DIM
    zero_group = jnp.zeros((t_new, pair), F32)
    zero_keys = jnp.zeros((WINDOW - t_new, ATT_KV), F32)

    for n in range(SAMPLE_SEQS_PER_STEP):
        rows = slice(n * t_new, (n + 1) * t_new)
        kc, vc = ck_ref[n], cv_ref[n]
        kn, vn = kn_ref[rows, :], vn_ref[rows, :]
        ko_ref[n, :WINDOW - t_new, :] = kc[t_new:]
        ko_ref[n, WINDOW - t_new:, :] = kn
        vo_ref[n, :WINDOW - t_new, :] = vc[t_new:]
        vo_ref[n, WINDOW - t_new:, :] = vn
        k2 = jnp.concatenate([kc, kn, zero_keys], axis=0).astype(BF16)
        v2 = jnp.concatenate([vc, vn, zero_keys], axis=0).astype(BF16)

        q = qa_ref[rows, :] * _ATT_SCALE
        q_swapped = pltpu.roll(q, ATT_HEAD_DIM, axis=1)
        blocks = []
        for h in range(ATT_HEADS):
            kv = h // ATT_GROUP
            want_lower = kv % 2 == 0
            if (h % 2 == 0) == want_lower:
                src = q[:, (h // 2) * pair:(h // 2 + 1) * pair]
            else:
                g = (h + 1) // 2 % (ATT_HEADS // 2)
                src = q_swapped[:, g * pair:(g + 1) * pair]
            piece = jnp.where(lower if want_lower else ~lower, src, 0.0)
            blocks.append(jnp.concatenate(
                [piece, zero_group] if kv // 2 == 0 else [zero_group, piece], axis=1))
        q_bd = jnp.concatenate(blocks, axis=0).astype(BF16)

        s = lax.dot_general(q_bd, k2, (((1,), (1,)), ((), ())), preferred_element_type=F32) + bias
        m = jnp.maximum(jnp.max(s, axis=-1, keepdims=True), sink)
        p = jnp.exp(s - m)
        inv = 1.0 / (jnp.sum(p, axis=-1, keepdims=True) + jnp.exp(sink - m))
        o = jnp.dot((p * inv).astype(BF16), v2, preferred_element_type=F32)

        outs = []
        for g in range(ATT_HEADS // 2):
            h0 = 2 * g
            kv = h0 // ATT_GROUP
            cols = slice((kv // 2) * pair, (kv // 2 + 1) * pair)
            a = o[h0 * t_new:(h0 + 1) * t_new, cols]
            b = o[(h0 + 1) * t_new:(h0 + 2) * t_new, cols]
            if kv % 2 == 0:
                b = pltpu.roll(b, ATT_HEAD_DIM, axis=1)
            else:
                a = pltpu.roll(a, ATT_HEAD_DIM, axis=1)
            outs.append(jnp.where(lower, a, b))
        oa_ref[rows, :] = jnp.concatenate(outs, axis=1)


def _attn_sample(qa, kva, cache_k, cache_v, sink_rows, n_seq, t_new):
    nb = SAMPLE_SEQS_PER_STEP
    rows = lambda col: (lambda n: (n, col))
    cache_spec = pl.BlockSpec((nb, WINDOW, ATT_KV), lambda n: (n, 0, 0))
    return pl.pallas_call(
        functools.partial(_attn_sample_kernel, t_new),
        out_shape=(
            jax.ShapeDtypeStruct((n_seq * t_new, ATT_Q), F32),
            jax.ShapeDtypeStruct(cache_k.shape, F32),
            jax.ShapeDtypeStruct(cache_v.shape, F32),
        ),
        grid=(n_seq // nb,),
        in_specs=[
            pl.BlockSpec((ATT_HEADS * t_new, 1), lambda n: (0, 0)),
            pl.BlockSpec((nb * t_new, ATT_Q), rows(0)),
            pl.BlockSpec((nb * t_new, ATT_KV), rows(0)),
            pl.BlockSpec((nb * t_new, ATT_KV), rows(1)),
            cache_spec, cache_spec,
        ],
        out_specs=(pl.BlockSpec((nb * t_new, ATT_Q), rows(0)), cache_spec, cache_spec),
        compiler_params=pltpu.CompilerParams(
            dimension_semantics=("parallel",), vmem_limit_bytes=V7X_VMEM_LIMIT_BYTES),
        name="attn_sample",
    )(sink_rows, qa, kva, kva, cache_k, cache_v)


def _ret_sample_kernel(t_new, log_g_ref, g_chunk_ref, q_ref, k_ref, v_ref, gr_ref, cos_ref, sin_ref,
                       s_in_ref, o_ref, s_out_ref):
    h = pl.program_id(1)
    log_g = log_g_ref[h]
    g_chunk = g_chunk_ref[h]
    n_seq = s_in_ref.shape[0]
    n_rows = n_seq * t_new
    cos2, sin2 = cos_ref[...], sin_ref[...]
    q = _rotate(q_ref[...], cos2, sin2)
    k = _rotate(k_ref[...], cos2, sin2) * _RET_K_SCALE
    vb = v_ref[...].astype(BF16)

    row = lax.broadcasted_iota(jnp.int32, (n_rows, n_rows), 0)
    col = lax.broadcasted_iota(jnp.int32, (n_rows, n_rows), 1)
    row_seq, col_seq = row // t_new, col // t_new
    diff = (row - col).astype(F32)
    decay = jnp.where((row_seq == col_seq) & (row >= col),
                      jnp.exp(jnp.maximum(diff, 0.0) * log_g), 0.0)
    scores = lax.dot_general(q.astype(BF16), k.astype(BF16), (((1,), (1,)), ((), ())),
                             preferred_element_type=F32) * decay
    y = jnp.dot(scores.astype(BF16), vb, preferred_element_type=F32)

    t = (row - row_seq * t_new).astype(F32)
    qw = q * jnp.exp((t + 1.0) * log_g)
    y_cross = [jnp.dot(qw[n * t_new:(n + 1) * t_new].astype(BF16), s_in_ref[n].astype(BF16),
                       preferred_element_type=F32) for n in range(n_seq)]
    y = y + jnp.concatenate(y_cross, axis=0)

    kt = (k * jnp.exp((t_new - 1.0 - t) * log_g)).T
    lhs = jnp.concatenate([jnp.where(col_seq == n, kt, 0.0) for n in range(n_seq)], axis=0)
    kv = jnp.dot(lhs.astype(BF16), vb, preferred_element_type=F32)
    for n in range(n_seq):
        s_out_ref[n] = g_chunk * s_in_ref[n] + kv[n * RET_DK:(n + 1) * RET_DK]

    o_ref[...] = _silu(gr_ref[...]) * _rms(y)


def _ret_sample(qkr, wide, cos2, sin2, state, n_seq, t_new):
    nb = RET_SAMPLE_SEQS
    n_rows = nb * t_new
    assert n_rows == RET_DK
    log_g = jnp.asarray(_RET_LOG_G, F32)
    g_chunk = jnp.asarray([math.exp(t_new * g) for g in _RET_LOG_G], F32)
    smem = pl.BlockSpec(memory_space=pltpu.SMEM)
    table_spec = pl.BlockSpec((n_rows, RET_DK), lambda g, h: (0, 0))
    state_spec = pl.BlockSpec((nb, None, RET_DK, RET_DV), lambda g, h: (g, h, 0, 0))
    return pl.pallas_call(
        functools.partial(_ret_sample_kernel, t_new),
        out_shape=(
            jax.ShapeDtypeStruct((n_seq * t_new, RET_V), F32),
            jax.ShapeDtypeStruct(state.shape, F32),
        ),
        grid=(n_seq // nb, RET_HEADS),
        in_specs=[
            smem, smem,
            pl.BlockSpec((n_rows, RET_DK), lambda g, h: (g, h)),
            pl.BlockSpec((n_rows, RET_DK), lambda g, h: (g, RET_HEADS + h)),
            pl.BlockSpec((n_rows, RET_DV), lambda g, h: (g, h)),
            pl.BlockSpec((n_rows, RET_DV), lambda g, h: (g, RET_HEADS + h)),
            table_spec, table_spec,
            state_spec,
        ],
        out_specs=(pl.BlockSpec((n_rows, RET_DV), lambda g, h: (g, h)), state_spec),
        compiler_params=pltpu.CompilerParams(
            dimension_semantics=("parallel", "arbitrary"), vmem_limit_bytes=V7X_VMEM_LIMIT_BYTES),
        name="ret_sample",
    )(log_g, g_chunk, qkr, qkr, wide, wide, jnp.tile(cos2, (nb, 1)), jnp.tile(sin2, (nb, 1)), state)


def _merge_kernel(st, x_ref, gate_ref, npost_ref, oa_ref, or_ref, ga_ref, gr_ref,
                  wpa_ref, wpr_ref, wo_ref, o_ref):
    i = pl.program_id(0)
    oa = oa_ref[...].astype(BF16)
    orr = or_ref[...].astype(BF16)
    for c in range(D_MODEL // COL_TILE):
        cols = slice(c * COL_TILE, (c + 1) * COL_TILE)
        a = jnp.dot(oa, wpa_ref[:, cols], preferred_element_type=F32)
        r = jnp.dot(orr, wpr_ref[:, cols], preferred_element_type=F32)
        merged = (jax.nn.sigmoid(ga_ref[:, cols].astype(F32)) * a
                  + jax.nn.sigmoid(gr_ref[:, cols].astype(F32)) * r)
        part = jnp.dot(merged.astype(BF16), wo_ref[cols, :], preferred_element_type=F32)
        if c == 0:
            o_ref[...] = part
        else:
            o_ref[...] += part
    _post_residual(st, i, x_ref, gate_ref, npost_ref[1:2, :], o_ref, 1.0)


def _merge(st, x, mod, npost, o_a, o_r, wide, w_pa, w_pr, w_o):
    gate_a_block = 2 * RET_V // D_MODEL
    row_spec = pl.BlockSpec((st.tile, D_MODEL), lambda i, j: (i, 0))
    resident = lambda shape: pl.BlockSpec(shape, lambda i, j: (0, 0), pipeline_mode=pl.Buffered(1))
    return pl.pallas_call(
        functools.partial(_merge_kernel, st),
        out_shape=jax.ShapeDtypeStruct((st.n_tokens, D_MODEL), F32),
        grid=(st.n_tiles, 1),
        in_specs=[
            row_spec,
            _mod_spec(st, 5),
            pl.BlockSpec((N_SUBLAYERS, D_MODEL), lambda i, j: (0, 0)),
            pl.BlockSpec((st.tile, ATT_Q), lambda i, j: (i, 0)),
            pl.BlockSpec((st.tile, RET_V), lambda i, j: (i, 0)),
            pl.BlockSpec((st.tile, D_MODEL), lambda i, j: (i, gate_a_block)),
            pl.BlockSpec((st.tile, D_MODEL), lambda i, j: (i, gate_a_block + 1)),
            resident((ATT_Q, D_MODEL)), resident((RET_V, D_MODEL)), resident((D_MODEL, D_MODEL)),
        ],
        out_specs=row_spec,
        compiler_params=pltpu.CompilerParams(
            dimension_semantics=("parallel", "arbitrary"), vmem_limit_bytes=V7X_VMEM_LIMIT_BYTES),
        name="merge_out",
    )(x, mod, npost, o_a, o_r, wide, wide, w_pa, w_pr, w_o)


def _rotation_tables(pos):
    half = RET_DK // 2
    inv_freq = ROPE_BASE ** (-jnp.linspace(0.0, 1.0, half, dtype=F32))
    ang = pos[:, None] * inv_freq[None, :]
    cos, sin = jnp.cos(ang), jnp.sin(ang)
    return jnp.concatenate([cos, cos], axis=-1), jnp.concatenate([-sin, sin], axis=-1)


def kernel(x_prompt, x_sample, cache_k_win, cache_v_win, state_ret, c_prompt, c_sample, w_ada, b_ada,
           norm_pre, norm_post, w_in, attn_sinks, w_pa, w_pr, w_o,
           ffn1_gate, ffn1_up, ffn1_down, ffn2_gate, ffn2_up, ffn2_down):
    batch, seq, _ = x_prompt.shape
    n_seq, t_new, _ = x_sample.shape
    assert w_ada.shape[0] == 1, "single-layer step"
    assert t_new == V7X_SUBLANES and seq % TOKEN_TILE == 0 and (n_seq * t_new) % TOKEN_TILE == 0
    assert seq % MERGE_TOKEN_TILE == 0 and (n_seq * t_new) % MERGE_TOKEN_TILE == 0
    assert batch <= MOD_PAD_ROWS and n_seq % MOD_PAD_ROWS == 0

    c_all = jnp.concatenate(
        [c_sample, c_prompt, jnp.zeros((MOD_PAD_ROWS - batch, D_MODEL), F32)], axis=0)
    mod = _ada(c_all, w_ada[0], b_ada[0])

    def prompt_stream(tile):
        return _Stream(batch * seq, tile, seq, MOD_PAD_ROWS, n_seq // MOD_PAD_ROWS)

    def sample_stream(tile):
        return _Stream(n_seq * t_new, tile, t_new, tile // t_new, 0)

    prompt, prompt_m = prompt_stream(TOKEN_TILE), prompt_stream(MERGE_TOKEN_TILE)
    sample, sample_m = sample_stream(TOKEN_TILE), sample_stream(MERGE_TOKEN_TILE // 2)

    npre, npost = norm_pre[0], norm_post[0]
    sinks = attn_sinks[0]
    bf = lambda w: w[0].astype(BF16)
    w_pa_b, w_pr_b, w_o_b = bf(w_pa), bf(w_pr), bf(w_o)

    xs = x_sample.reshape(n_seq * t_new, D_MODEL)
    xs, f1g, f1u, f1d = _ffn(sample, 0, xs, mod, npre, npost,
                             ffn1_gate[0], ffn1_up[0], ffn1_down[0], emit_bf16=True)
    qa_s, kva_s, qkr_s, wide_s, w_in_b = _proj(sample, xs, mod, npre, w_in[0], F32, emit_bf16=True)

    xp = x_prompt.reshape(batch * seq, D_MODEL)
    xp = _ffn(prompt, 0, xp, mod, npre, npost, f1g, f1u, f1d)
    qa, kva, qkr, wide = _proj(prompt, xp, mod, npre, w_in_b, BF16)
    o_a = _attn_prompt(qa, kva, sinks, batch, seq)
    cos_p, sin_p = _rotation_tables(jnp.arange(seq, dtype=F32))
    o_r, state_p = _ret_prompt(qkr, wide, cos_p, sin_p, batch, seq)
    xp = _merge(prompt_m, xp, mod, npost, o_a, o_r, wide, w_pa_b, w_pr_b, w_o_b)
    kva_p = kva.reshape(batch, seq, 2 * ATT_KV)[:, seq - WINDOW:]
    kv_shape = (1, batch, WINDOW, ATT_KV_HEADS, ATT_HEAD_DIM)
    k_win_p = kva_p[..., :ATT_KV].reshape(kv_shape)
    v_win_p = kva_p[..., ATT_KV:].reshape(kv_shape)

    cos_s, sin_s = _rotation_tables(jnp.arange(t_new, dtype=F32) + PAST_LEN)
    sink_rows = jnp.repeat(sinks, t_new)[:, None]
    o_a_s, k_s, v_s = _attn_sample(
        qa_s, kva_s,
        cache_k_win[0].reshape(n_seq, WINDOW, ATT_KV), cache_v_win[0].reshape(n_seq, WINDOW, ATT_KV),
        sink_rows, n_seq, t_new)
    o_r_s, state_s = _ret_sample(qkr_s, wide_s, cos_s, sin_s, state_ret[0], n_seq, t_new)
    xs = _merge(sample_m, xs, mod, npost, o_a_s, o_r_s, wide_s, w_pa_b, w_pr_b, w_o_b)
    xs, f2g, f2u, f2d = _ffn(sample, 2, xs, mod, npre, npost,
                             ffn2_gate[0], ffn2_up[0], ffn2_down[0], emit_bf16=True)
    xp = _ffn(prompt, 2, xp, mod, npre, npost, f2g, f2u, f2d)
    kvs_shape = (1, n_seq, WINDOW, ATT_KV_HEADS, ATT_HEAD_DIM)

    return (xp.reshape(batch, seq, D_MODEL), xs.reshape(n_seq, t_new, D_MODEL),
            k_win_p, v_win_p, state_p[None],
            k_s.reshape(kvs_shape), v_s.reshape(kvs_shape), state_s[None])
```

```python
import functools
import math
from typing import NamedTuple

import jax
import jax.numpy as jnp
import numpy as np
from jax import lax
from jax.experimental import pallas as pl
from jax.experimental.pallas import tpu as pltpu

F32 = jnp.float32
BF16 = jnp.bfloat16

D_MODEL = 2048
WINDOW = 128
ATT_HEADS = 16
ATT_KV_HEADS = 4
ATT_HEAD_DIM = 64
ATT_GROUP = ATT_HEADS // ATT_KV_HEADS
ATT_Q = ATT_HEADS * ATT_HEAD_DIM
ATT_KV = ATT_KV_HEADS * ATT_HEAD_DIM
RET_HEADS = 8
RET_DK = 128
RET_DV = 256
RET_CHUNK = 128
RET_QK = RET_HEADS * RET_DK
RET_V = RET_HEADS * RET_DV
ROPE_BASE = 10000.0
D_FF = 5632
NORM_EPS = 1e-6
N_SUBLAYERS = 3
PAST_LEN = 16384
D_IN = ATT_Q + 2 * ATT_KV + 2 * RET_QK + 2 * RET_V + 2 * D_MODEL

V7X_SUBLANES = 8
V7X_BF16_ROWS = 16
V7X_VMEM_LIMIT_BYTES = 60 * 1024 * 1024

TOKEN_TILE = 1024
MERGE_TOKEN_TILE = 512
FF_TILE_F32 = 256
FF_TILE_BF16 = 512
COL_TILE = 512
ADA_COL_TILE = 1024
ATTN_BLOCKS_PER_STEP = 2
RET_CHUNKS_PER_STEP = 4
RET_SAMPLE_HEADS = 2
SAMPLE_SEQS_PER_STEP = 8
RET_SAMPLE_SEQS = 16
MOD_PAD_ROWS = 8

_QA_TILES = ATT_Q // COL_TILE
_KVA_TILES = 2 * ATT_KV // COL_TILE
_QKR_TILES = 2 * RET_QK // COL_TILE
_WIDE_COLS = 2 * RET_V + 2 * D_MODEL
_WIDE_TILES = _WIDE_COLS // COL_TILE
_QKR_START = _QA_TILES + _KVA_TILES
_WIDE_START = _QKR_START + _QKR_TILES
assert _WIDE_START + _WIDE_TILES == D_IN // COL_TILE and _KVA_TILES == 1

_ALIBI_SLOPES = [2.0 ** (-8.0 * (h + 1) / ATT_HEADS) for h in range(ATT_HEADS)]
_RET_LOG_G = [math.log(1.0 - 2.0 ** (-5.0 - h)) for h in range(RET_HEADS)]
_ATT_SCALE = ATT_HEAD_DIM ** -0.5
_RET_K_SCALE = RET_DK ** -0.5


class _Stream(NamedTuple):
    n_tokens: int
    tile: int
    rows_per_mod: int
    mod_block_rows: int
    mod_block_base: int

    @property
    def n_tiles(self):
        return self.n_tokens // self.tile

    @property
    def sub_rows(self):
        return min(self.rows_per_mod, self.tile)

    @property
    def group_rows(self):
        return max(self.sub_rows, V7X_BF16_ROWS)

    @property
    def n_groups(self):
        return self.tile // self.group_rows

    @property
    def mods_per_group(self):
        return self.group_rows // self.sub_rows


def _rms(x):
    return x * lax.rsqrt(jnp.mean(x * x, axis=-1, keepdims=True) + NORM_EPS)


def _silu(x):
    return x * jax.nn.sigmoid(x)


def _for_groups(n_groups, fn):
    if n_groups == 1:
        fn(0)
    else:
        def body(g, carry):
            fn(g)
            return carry
        lax.fori_loop(0, n_groups, body, 0)


def _group_base(st, g):
    return 0 if st.n_groups == 1 else pl.multiple_of(g * st.group_rows, st.group_rows)


def _mod_row(st, i, g, s):
    if st.rows_per_mod >= st.tile:
        return (i * st.tile) // st.rows_per_mod
    return g * st.mods_per_group + s


def _pre_norm(st, i, x_ref, shift_ref, scale_ref, gain, h_ref):
    def group(g):
        base = _group_base(st, g)
        parts = []
        for s in range(st.mods_per_group):
            rows = pl.ds(base + s * st.sub_rows, st.sub_rows)
            m = _mod_row(st, i, g, s)
            x = x_ref[rows, :]
            row_gain = gain * (1.0 + scale_ref[pl.ds(m, 1), :])
            sh = shift_ref[pl.ds(m, 1), :]
            parts.append(_rms(x) * row_gain + sh)
        h = parts[0] if len(parts) == 1 else jnp.concatenate(parts, axis=0)
        h_ref[pl.ds(base, st.group_rows), :] = h.astype(BF16)
    _for_groups(st.n_groups, group)


def _post_residual(st, i, x_ref, gate_ref, gain, o_ref, coeff):
    def group(g):
        base = _group_base(st, g)
        for s in range(st.mods_per_group):
            rows = pl.ds(base + s * st.sub_rows, st.sub_rows)
            m = _mod_row(st, i, g, s)
            row_gain = gate_ref[pl.ds(m, 1), :] * gain
            if coeff != 1.0:
                row_gain = coeff * row_gain
            o_ref[rows, :] = x_ref[rows, :] + _rms(o_ref[rows, :]) * row_gain
    _for_groups(st.n_groups, group)


def _ada_kernel(c_ref, w_ref, b_ref, o_ref):
    a = _silu(c_ref[...]).astype(BF16)
    o_ref[0] = jnp.dot(a, w_ref[...].astype(BF16), preferred_element_type=F32) + b_ref[...]


def _ada(c_all, w_ada, b_ada):
    rows = c_all.shape[0]
    n_vec = N_SUBLAYERS * 3
    per_vec = D_MODEL // ADA_COL_TILE
    return pl.pallas_call(
        _ada_kernel,
        out_shape=jax.ShapeDtypeStruct((n_vec, rows, D_MODEL), F32),
        grid=(n_vec * per_vec,),
        in_specs=[
            pl.BlockSpec((rows, D_MODEL), lambda j: (0, 0)),
            pl.BlockSpec((D_MODEL, ADA_COL_TILE), lambda j: (0, j)),
            pl.BlockSpec((1, ADA_COL_TILE), lambda j: (0, j)),
        ],
        out_specs=pl.BlockSpec((1, rows, ADA_COL_TILE), lambda j: (j // per_vec, 0, j % per_vec)),
        compiler_params=pltpu.CompilerParams(
            dimension_semantics=("arbitrary",), vmem_limit_bytes=V7X_VMEM_LIMIT_BYTES),
        name="ada_mod",
    )(c_all, w_ada, b_ada.reshape(1, -1))


def _mod_spec(st, vec):
    if st.rows_per_mod >= st.tile:
        index = lambda i, j: (vec, st.mod_block_base, 0)
    else:
        index = lambda i, j: (vec, st.mod_block_base + i, 0)
    return pl.BlockSpec((None, st.mod_block_rows, D_MODEL), index)


def _ffn_kernel(st, sub, emit_bf16, x_ref, shift_ref, scale_ref, gate_ref, npre_ref, npost_ref,
                wg_ref, wu_ref, wd_ref, o_ref, *rest):
    i = pl.program_id(0)
    j = pl.program_id(1)
    h_ref = rest[-1]

    def partial_down():
        h = h_ref[...]
        wg, wu, wd = wg_ref[...].astype(BF16), wu_ref[...].astype(BF16), wd_ref[...].astype(BF16)
        if emit_bf16:
            wg_out_ref, wu_out_ref, wd_out_ref = rest[:3]
            wg_out_ref[...] = wg
            wu_out_ref[...] = wu
            wd_out_ref[...] = wd
        g = jnp.dot(h, wg, preferred_element_type=F32)
        u = jnp.dot(h, wu, preferred_element_type=F32)
        a = (_silu(g) * u).astype(BF16)
        return jnp.dot(a, wd, preferred_element_type=F32)

    @pl.when(j == 0)
    def _():
        _pre_norm(st, i, x_ref, shift_ref, scale_ref, npre_ref[sub:sub + 1, :], h_ref)
        o_ref[...] = partial_down()

    last = pl.num_programs(1) - 1

    @pl.when((j > 0) & (j < last))
    def _():
        o_ref[...] += partial_down()

    @pl.when(j == last)
    def _():
        o_ref[...] += partial_down()
        _post_residual(st, i, x_ref, gate_ref, npost_ref[sub:sub + 1, :], o_ref, 0.5)


def _ffn(st, sub, x, mod, npre, npost, wg, wu, wd, emit_bf16=False):
    tiled = wg.ndim == 3
    ff_tile = FF_TILE_BF16 if tiled else FF_TILE_F32
    row_spec = pl.BlockSpec((st.tile, D_MODEL), lambda i, j: (i, 0))
    full_spec = pl.BlockSpec((N_SUBLAYERS, D_MODEL), lambda i, j: (0, 0))
    if tiled:
        up_spec = pl.BlockSpec((None, D_MODEL, ff_tile), lambda i, j: (j, 0, 0))
    else:
        up_spec = pl.BlockSpec((D_MODEL, ff_tile), lambda i, j: (0, j))
    down_spec = pl.BlockSpec((ff_tile, D_MODEL), lambda i, j: (j, 0))
    out_shape = [jax.ShapeDtypeStruct((st.n_tokens, D_MODEL), F32)]
    out_specs = [row_spec]
    if emit_bf16:
        assert st.n_tiles == 1 and not tiled
        per = FF_TILE_BF16 // ff_tile
        up_shape = jax.ShapeDtypeStruct((D_FF // FF_TILE_BF16, D_MODEL, FF_TILE_BF16), BF16)
        up_out = pl.BlockSpec((None, D_MODEL, ff_tile), lambda i, j: (j // per, 0, j % per))
        out_shape += [up_shape, up_shape, jax.ShapeDtypeStruct((D_FF, D_MODEL), BF16)]
        out_specs += [up_out, up_out, down_spec]
    outs = pl.pallas_call(
        functools.partial(_ffn_kernel, st, sub, emit_bf16),
        out_shape=out_shape,
        grid=(st.n_tiles, D_FF // ff_tile),
        in_specs=[
            row_spec,
            _mod_spec(st, 3 * sub + 0), _mod_spec(st, 3 * sub + 1), _mod_spec(st, 3 * sub + 2),
            full_spec, full_spec,
            up_spec, up_spec, down_spec,
        ],
        out_specs=out_specs,
        scratch_shapes=[pltpu.VMEM((st.tile, D_MODEL), BF16)],
        compiler_params=pltpu.CompilerParams(
            dimension_semantics=("parallel", "arbitrary"), vmem_limit_bytes=V7X_VMEM_LIMIT_BYTES),
        name=f"ffn{sub}",
    )(x, mod, mod, mod, npre, npost, wg, wu, wd)
    return outs if emit_bf16 else outs[0]


def _proj_kernel(st, emit_bf16, x_ref, shift_ref, scale_ref, npre_ref, w_ref,
                 qa_ref, kva_ref, qkr_ref, wide_ref, *rest):
    i = pl.program_id(0)
    j = pl.program_id(1)
    h_ref = rest[-1]

    def project(dst_ref):
        w = w_ref[...].astype(BF16)
        if emit_bf16:
            rest[0][...] = w
        dst_ref[...] = jnp.dot(h_ref[...], w, preferred_element_type=F32).astype(dst_ref.dtype)

    @pl.when(j == 0)
    def _():
        _pre_norm(st, i, x_ref, shift_ref, scale_ref, npre_ref[1:2, :], h_ref)
        project(qa_ref)

    @pl.when((j > 0) & (j < _QA_TILES))
    def _():
        project(qa_ref)

    @pl.when(j == _QA_TILES)
    def _():
        project(kva_ref)

    @pl.when((j >= _QKR_START) & (j < _WIDE_START))
    def _():
        project(qkr_ref)

    @pl.when(j >= _WIDE_START)
    def _():
        project(wide_ref)


def _proj(st, x, mod, npre, w_in, narrow_dtype, emit_bf16=False):
    n = st.n_tokens
    tiled = w_in.ndim == 3
    row_spec = pl.BlockSpec((st.tile, D_MODEL), lambda i, j: (i, 0))
    out_block = (st.tile, COL_TILE)
    tile_spec = pl.BlockSpec((None, D_MODEL, COL_TILE), lambda i, j: (j, 0, 0))
    out_shape = [
        jax.ShapeDtypeStruct((n, ATT_Q), narrow_dtype),
        jax.ShapeDtypeStruct((n, 2 * ATT_KV), F32),
        jax.ShapeDtypeStruct((n, 2 * RET_QK), F32),
        jax.ShapeDtypeStruct((n, _WIDE_COLS), narrow_dtype),
    ]
    out_specs = [
        pl.BlockSpec(out_block, lambda i, j: (i, jnp.minimum(j, _QA_TILES - 1))),
        pl.BlockSpec(out_block, lambda i, j: (i, 0)),
        pl.BlockSpec(out_block, lambda i, j: (i, jnp.clip(j - _QKR_START, 0, _QKR_TILES - 1))),
        pl.BlockSpec(out_block, lambda i, j: (i, jnp.maximum(j - _WIDE_START, 0))),
    ]
    if emit_bf16:
        assert st.n_tiles == 1 and not tiled
        out_shape.append(jax.ShapeDtypeStruct((D_IN // COL_TILE, D_MODEL, COL_TILE), BF16))
        out_specs.append(tile_spec)
    outs = pl.pallas_call(
        functools.partial(_proj_kernel, st, emit_bf16),
        out_shape=out_shape,
        grid=(st.n_tiles, D_IN // COL_TILE),
        in_specs=[
            row_spec,
            _mod_spec(st, 3), _mod_spec(st, 4),
            pl.BlockSpec((N_SUBLAYERS, D_MODEL), lambda i, j: (0, 0)),
            tile_spec if tiled else pl.BlockSpec((D_MODEL, COL_TILE), lambda i, j: (0, j)),
        ],
        out_specs=out_specs,
        scratch_shapes=[pltpu.VMEM((st.tile, D_MODEL), BF16)],
        compiler_params=pltpu.CompilerParams(
            dimension_semantics=("parallel", "arbitrary"), vmem_limit_bytes=V7X_VMEM_LIMIT_BYTES),
        name="in_proj",
    )(x, mod, mod, npre, w_in)
    return outs


def _attention(q, k2, v2, sinks_ref, first_valid_key):
    tq = q.shape[0]
    a_idx = lax.broadcasted_iota(jnp.int32, (tq, 2 * WINDOW), 0)
    b_idx = lax.broadcasted_iota(jnp.int32, (tq, 2 * WINDOW), 1)
    dist = WINDOW + a_idx - b_idx
    mask = (dist >= 0) & (dist <= WINDOW) & (b_idx >= first_valid_key)
    dist_f = jnp.where(mask, dist.astype(F32), jnp.inf)
    q = q * _ATT_SCALE
    outs = []
    for kv in range(ATT_KV_HEADS):
        cols = slice(kv * ATT_HEAD_DIM, (kv + 1) * ATT_HEAD_DIM)
        kk = k2[:, cols]
        vv = v2[:, cols]
        heads = range(kv * ATT_GROUP, (kv + 1) * ATT_GROUP)
        qg = jnp.concatenate(
            [q[:, h * ATT_HEAD_DIM:(h + 1) * ATT_HEAD_DIM] for h in heads], axis=0).astype(BF16)
        s_all = lax.dot_general(qg, kk, (((1,), (1,)), ((), ())), preferred_element_type=F32)
        probs = []
        for g, h in enumerate(heads):
            s = s_all[g * tq:(g + 1) * tq] - _ALIBI_SLOPES[h] * dist_f
            sink = sinks_ref[h]
            m = jnp.maximum(jnp.max(s, axis=-1, keepdims=True), sink)
            p = jnp.exp(s - m)
            inv = 1.0 / (jnp.sum(p, axis=-1, keepdims=True) + jnp.exp(sink - m))
            probs.append((p * inv).astype(BF16))
        o_all = jnp.dot(jnp.concatenate(probs, axis=0), vv, preferred_element_type=F32)
        outs.extend(o_all[g * tq:(g + 1) * tq] for g in range(ATT_GROUP))
    return jnp.concatenate(outs, axis=-1)


def _attn_prompt_kernel(sinks_ref, q_ref, kc_ref, vc_ref, kp_ref, vp_ref, o_ref):
    step = pl.program_id(1)
    k_tiles = [kp_ref[...]] + [kc_ref[r * WINDOW:(r + 1) * WINDOW, :] for r in range(ATTN_BLOCKS_PER_STEP)]
    v_tiles = [vp_ref[...]] + [vc_ref[r * WINDOW:(r + 1) * WINDOW, :] for r in range(ATTN_BLOCKS_PER_STEP)]
    for r in range(ATTN_BLOCKS_PER_STEP):
        rows = slice(r * WINDOW, (r + 1) * WINDOW)
        k2 = jnp.concatenate(k_tiles[r:r + 2], axis=0).astype(BF16)
        v2 = jnp.concatenate(v_tiles[r:r + 2], axis=0).astype(BF16)
        first_valid = jnp.where(step == 0, WINDOW, 0) if r == 0 else 0
        o_ref[rows, :] = _attention(q_ref[rows, :], k2, v2, sinks_ref, first_valid).astype(o_ref.dtype)


def _attn_prompt(qa, kva, sinks, batch, seq):
    per = ATTN_BLOCKS_PER_STEP
    ns = seq // (per * WINDOW)
    cur = lambda col: (lambda b, i: (b * ns + i, col))
    prev = lambda col: (lambda b, i: (b * ns * per + jnp.maximum(i * per - 1, 0), col))
    return pl.pallas_call(
        _attn_prompt_kernel,
        out_shape=jax.ShapeDtypeStruct((batch * seq, ATT_Q), BF16),
        grid=(batch, ns),
        in_specs=[
            pl.BlockSpec(memory_space=pltpu.SMEM),
            pl.BlockSpec((per * WINDOW, ATT_Q), lambda b, i: (b * ns + i, 0)),
            pl.BlockSpec((per * WINDOW, ATT_KV), cur(0)),
            pl.BlockSpec((per * WINDOW, ATT_KV), cur(1)),
            pl.BlockSpec((WINDOW, ATT_KV), prev(0)),
            pl.BlockSpec((WINDOW, ATT_KV), prev(1)),
        ],
        out_specs=pl.BlockSpec((per * WINDOW, ATT_Q), lambda b, i: (b * ns + i, 0)),
        compiler_params=pltpu.CompilerParams(
            dimension_semantics=("parallel", "arbitrary"), vmem_limit_bytes=V7X_VMEM_LIMIT_BYTES),
        name="attn_prompt",
    )(sinks, qa, kva, kva, kva, kva)


def _rotate(x, cos2, sin2):
    return x * cos2 + pltpu.roll(x, RET_DK // 2, axis=1) * sin2


def _ret_prompt_kernel(q_ref, k_ref, v_ref, gr_ref, cos_ref, sin_ref, o_ref, s_ref,
                       decay_ref, qw_ref, kw_ref):
    @pl.when(pl.program_id(1) == 0)
    def _():
        s_ref[...] = jnp.zeros_like(s_ref)
        row = lax.broadcasted_iota(jnp.int32, (RET_CHUNK, RET_CHUNK), 0).astype(F32)
        col = lax.broadcasted_iota(jnp.int32, (RET_CHUNK, RET_CHUNK), 1).astype(F32)
        diff = row - col
        for h in range(RET_HEADS):
            log_g = _RET_LOG_G[h]
            decay_ref[h] = jnp.where(diff >= 0, jnp.exp(jnp.maximum(diff, 0.0) * log_g), 0.0)
            qw_ref[h] = jnp.exp((row + 1.0) * log_g)
            kw_ref[h] = jnp.exp((RET_CHUNK - 1.0 - row) * log_g)

    for r in range(RET_CHUNKS_PER_STEP):
        rows = slice(r * RET_CHUNK, (r + 1) * RET_CHUNK)
        cos2, sin2 = cos_ref[rows, :], sin_ref[rows, :]
        for h in range(RET_HEADS):
            qk_cols = slice(h * RET_DK, (h + 1) * RET_DK)
            v_cols = slice(h * RET_DV, (h + 1) * RET_DV)
            qh = _rotate(q_ref[rows, qk_cols], cos2, sin2)
            kh = _rotate(k_ref[rows, qk_cols], cos2, sin2) * _RET_K_SCALE
            vb = v_ref[rows, v_cols].astype(BF16)
            s_prev = s_ref[0, h]
            scores = lax.dot_general(qh.astype(BF16), kh.astype(BF16), (((1,), (1,)), ((), ())),
                                     preferred_element_type=F32) * decay_ref[h]
            y = jnp.dot(scores.astype(BF16), vb, preferred_element_type=F32)
            y = y + jnp.dot((qh * qw_ref[h]).astype(BF16), s_prev.astype(BF16),
                            preferred_element_type=F32)
            kt = (kh * kw_ref[h]).T.astype(BF16)
            s_ref[0, h] = (math.exp(RET_CHUNK * _RET_LOG_G[h]) * s_prev
                           + jnp.dot(kt, vb, preferred_element_type=F32))
            gate = gr_ref[rows, v_cols].astype(F32)
            o_ref[rows, v_cols] = (_silu(gate) * _rms(y)).astype(o_ref.dtype)


def _ret_prompt(qkr, wide, cos2, sin2, batch, seq):
    step_rows = RET_CHUNKS_PER_STEP * RET_CHUNK
    nc = seq // step_rows
    rows = lambda col: (lambda b, c: (b * nc + c, col))
    return pl.pallas_call(
        _ret_prompt_kernel,
        out_shape=(
            jax.ShapeDtypeStruct((batch * seq, RET_V), BF16),
            jax.ShapeDtypeStruct((batch, RET_HEADS, RET_DK, RET_DV), F32),
        ),
        grid=(batch, nc),
        in_specs=[
            pl.BlockSpec((step_rows, RET_QK), rows(0)),
            pl.BlockSpec((step_rows, RET_QK), rows(1)),
            pl.BlockSpec((step_rows, RET_V), rows(0)),
            pl.BlockSpec((step_rows, RET_V), rows(1)),
            pl.BlockSpec((step_rows, RET_DK), lambda b, c: (c, 0)),
            pl.BlockSpec((step_rows, RET_DK), lambda b, c: (c, 0)),
        ],
        out_specs=(
            pl.BlockSpec((step_rows, RET_V), rows(0)),
            pl.BlockSpec((1, RET_HEADS, RET_DK, RET_DV), lambda b, c: (b, 0, 0, 0)),
        ),
        scratch_shapes=[pltpu.VMEM((RET_HEADS, RET_CHUNK, RET_CHUNK), F32)] * 3,
        compiler_params=pltpu.CompilerParams(
            dimension_semantics=("parallel", "arbitrary"), vmem_limit_bytes=V7X_VMEM_LIMIT_BYTES),
        name="ret_prompt",
    )(qkr, qkr, wide, wide, cos2, sin2)


def _attn_sample_kernel(t_new, sink_ref, qa_ref, kn_ref, vn_ref, ck_ref, cv_ref,
                        oa_ref, ko_ref, vo_ref):
    n_rows = ATT_HEADS * t_new
    pair = 2 * ATT_HEAD_DIM
    row = lax.broadcasted_iota(jnp.int32, (n_rows, 2 * WINDOW), 0)
    key = lax.broadcasted_iota(jnp.int32, (n_rows, 2 * WINDOW), 1)
    head = row // t_new
    dist = WINDOW + (row - head * t_new) - key
    slope = jnp.exp2((head.astype(F32) + 1.0) * (-8.0 / ATT_HEADS))
    bias = jnp.where((dist >= 0) & (dist <= WINDOW), -slope * dist.astype(F32), -jnp.inf)
    sink = sink_ref[...]
    lower = lax.broadcasted_iota(jnp.int32, (t_new, pair), 1) < ATT_HEAD_DIM
    zero_group = jnp.zeros((t_new, pair), F32)
    zero_keys = jnp.zeros((WINDOW - t_new, ATT_KV), F32)

    for n in range(SAMPLE_SEQS_PER_STEP):
        rows = slice(n * t_new, (n + 1) * t_new)
        kc, vc = ck_ref[n], cv_ref[n]
        kn, vn = kn_ref[rows, :], vn_ref[rows, :]
        ko_ref[n, :WINDOW - t_new, :] = kc[t_new:]
        ko_ref[n, WINDOW - t_new:, :] = kn
        vo_ref[n, :WINDOW - t_new, :] = vc[t_new:]
        vo_ref[n, WINDOW - t_new:, :] = vn
        k2 = jnp.concatenate([kc, kn, zero_keys], axis=0).astype(BF16)
        v2 = jnp.concatenate([vc, vn, zero_keys], axis=0).astype(BF16)

        q = qa_ref[rows, :] * _ATT_SCALE
        q_swapped = pltpu.roll(q, ATT_HEAD_DIM, axis=1)
        blocks = []
        for h in range(ATT_HEADS):
            kv = h // ATT_GROUP
            want_lower = kv % 2 == 0
            if (h % 2 == 0) == want_lower:
                src = q[:, (h // 2) * pair:(h // 2 + 1) * pair]
            else:
                g = (h + 1) // 2 % (ATT_HEADS // 2)
                src = q_swapped[:, g * pair:(g + 1) * pair]
            piece = jnp.where(lower if want_lower else ~lower, src, 0.0)
            blocks.append(jnp.concatenate(
                [piece, zero_group] if kv // 2 == 0 else [zero_group, piece], axis=1))
        q_bd = jnp.concatenate(blocks, axis=0).astype(BF16)

        s = lax.dot_general(q_bd, k2, (((1,), (1,)), ((), ())), preferred_element_type=F32) + bias
        m = jnp.maximum(jnp.max(s, axis=-1, keepdims=True), sink)
        p = jnp.exp(s - m)
        inv = 1.0 / (jnp.sum(p, axis=-1, keepdims=True) + jnp.exp(sink - m))
        o = jnp.dot((p * inv).astype(BF16), v2, preferred_element_type=F32)

        outs = []
        for g in range(ATT_HEADS // 2):
            h0 = 2 * g
            kv = h0 // ATT_GROUP
            cols = slice((kv // 2) * pair, (kv // 2 + 1) * pair)
            a = o[h0 * t_new:(h0 + 1) * t_new, cols]
            b = o[(h0 + 1) * t_new:(h0 + 2) * t_new, cols]
            if kv % 2 == 0:
                b = pltpu.roll(b, ATT_HEAD_DIM, axis=1)
            else:
                a = pltpu.roll(a, ATT_HEAD_DIM, axis=1)
            outs.append(jnp.where(lower, a, b))
        oa_ref[rows, :] = jnp.concatenate(outs, axis=1)


def _attn_sample(qa, kva, cache_k, cache_v, sink_rows, n_seq, t_new):
    nb = SAMPLE_SEQS_PER_STEP
    rows = lambda col: (lambda n: (n, col))
    cache_spec = pl.BlockSpec((nb, WINDOW, ATT_KV), lambda n: (n, 0, 0))
    return pl.pallas_call(
        functools.partial(_attn_sample_kernel, t_new),
        out_shape=(
            jax.ShapeDtypeStruct((n_seq * t_new, ATT_Q), F32),
            jax.ShapeDtypeStruct(cache_k.shape, F32),
            jax.ShapeDtypeStruct(cache_v.shape, F32),
        ),
        grid=(n_seq // nb,),
        in_specs=[
            pl.BlockSpec((ATT_HEADS * t_new, 1), lambda n: (0, 0)),
            pl.BlockSpec((nb * t_new, ATT_Q), rows(0)),
            pl.BlockSpec((nb * t_new, ATT_KV), rows(0)),
            pl.BlockSpec((nb * t_new, ATT_KV), rows(1)),
            cache_spec, cache_spec,
        ],
        out_specs=(pl.BlockSpec((nb * t_new, ATT_Q), rows(0)), cache_spec, cache_spec),
        compiler_params=pltpu.CompilerParams(
            dimension_semantics=("parallel",), vmem_limit_bytes=V7X_VMEM_LIMIT_BYTES),
        name="attn_sample",
    )(sink_rows, qa, kva, kva, cache_k, cache_v)


def _ret_sample_kernel(t_new, log_g_ref, g_chunk_ref, q_ref, k_ref, v_ref, gr_ref, cos_ref, sin_ref,
                       s_in_ref, o_ref, s_out_ref):
    n_seq = s_in_ref.shape[0]
    n_rows = n_seq * t_new
    cos2, sin2 = cos_ref[...], sin_ref[...]
    row = lax.broadcasted_iota(jnp.int32, (n_rows, n_rows), 0)
    col = lax.broadcasted_iota(jnp.int32, (n_rows, n_rows), 1)
    row_seq, col_seq = row // t_new, col // t_new
    same_chunk_causal = (row_seq == col_seq) & (row >= col)
    diff = jnp.maximum((row - col).astype(F32), 0.0)
    t = (row - row_seq * t_new).astype(F32)

    for hh in range(RET_SAMPLE_HEADS):
        h = pl.program_id(1) * RET_SAMPLE_HEADS + hh
        log_g = log_g_ref[h]
        g_chunk = g_chunk_ref[h]
        qk_cols = slice(hh * RET_DK, (hh + 1) * RET_DK)
        v_cols = slice(hh * RET_DV, (hh + 1) * RET_DV)
        q = _rotate(q_ref[:, qk_cols], cos2, sin2)
        k = _rotate(k_ref[:, qk_cols], cos2, sin2) * _RET_K_SCALE
        vb = v_ref[:, v_cols].astype(BF16)

        decay = jnp.where(same_chunk_causal, jnp.exp(diff * log_g), 0.0)
        scores = lax.dot_general(q.astype(BF16), k.astype(BF16), (((1,), (1,)), ((), ())),
                                 preferred_element_type=F32) * decay
        y = jnp.dot(scores.astype(BF16), vb, preferred_element_type=F32)

        qw = q * jnp.exp((t + 1.0) * log_g)
        y_cross = [jnp.dot(qw[n * t_new:(n + 1) * t_new].astype(BF16), s_in_ref[n, hh].astype(BF16),
                           preferred_element_type=F32) for n in range(n_seq)]
        y = y + jnp.concatenate(y_cross, axis=0)

        kt = (k * jnp.exp((t_new - 1.0 - t) * log_g)).T
        lhs = jnp.concatenate([jnp.where(col_seq == n, kt, 0.0) for n in range(n_seq)], axis=0)
        kv = jnp.dot(lhs.astype(BF16), vb, preferred_element_type=F32)
        for n in range(n_seq):
            s_out_ref[n, hh] = g_chunk * s_in_ref[n, hh] + kv[n * RET_DK:(n + 1) * RET_DK]

        o_ref[:, v_cols] = _silu(gr_ref[:, v_cols]) * _rms(y)


def _ret_sample(qkr, wide, cos2, sin2, state, n_seq, t_new):
    nb = RET_SAMPLE_SEQS
    nh = RET_SAMPLE_HEADS
    n_rows = nb * t_new
    assert n_rows == RET_DK
    log_g = jnp.asarray(_RET_LOG_G, F32)
    g_chunk = jnp.asarray([math.exp(t_new * g) for g in _RET_LOG_G], F32)
    smem = pl.BlockSpec(memory_space=pltpu.SMEM)
    table_spec = pl.BlockSpec((n_rows, RET_DK), lambda g, h: (0, 0))
    state_spec = pl.BlockSpec((nb, nh, RET_DK, RET_DV), lambda g, h: (g, h, 0, 0))
    return pl.pallas_call(
        functools.partial(_ret_sample_kernel, t_new),
        out_shape=(
            jax.ShapeDtypeStruct((n_seq * t_new, RET_V), F32),
            jax.ShapeDtypeStruct(state.shape, F32),
        ),
        grid=(n_seq // nb, RET_HEADS // nh),
        in_specs=[
            smem, smem,
            pl.BlockSpec((n_rows, nh * RET_DK), lambda g, h: (g, h)),
            pl.BlockSpec((n_rows, nh * RET_DK), lambda g, h: (g, RET_HEADS // nh + h)),
            pl.BlockSpec((n_rows, nh * RET_DV), lambda g, h: (g, h)),
            pl.BlockSpec((n_rows, nh * RET_DV), lambda g, h: (g, RET_HEADS // nh + h)),
            table_spec, table_spec,
            state_spec,
        ],
        out_specs=(pl.BlockSpec((n_rows, nh * RET_DV), lambda g, h: (g, h)), state_spec),
        compiler_params=pltpu.CompilerParams(
            dimension_semantics=("parallel", "arbitrary"), vmem_limit_bytes=V7X_VMEM_LIMIT_BYTES),
        name="ret_sample",
    )(log_g, g_chunk, qkr, qkr, wide, wide, jnp.tile(cos2, (nb, 1)), jnp.tile(sin2, (nb, 1)), state)


def _merge_kernel(st, x_ref, gate_ref, npost_ref, oa_ref, or_ref, ga_ref, gr_ref,
                  wpa_ref, wpr_ref, wo_ref, o_ref):
    i = pl.program_id(0)
    oa = oa_ref[...].astype(BF16)
    orr = or_ref[...].astype(BF16)
    for c in range(D_MODEL // COL_TILE):
        cols = slice(c * COL_TILE, (c + 1) * COL_TILE)
        a = jnp.dot(oa, wpa_ref[:, cols], preferred_element_type=F32)
        r = jnp.dot(orr, wpr_ref[:, cols], preferred_element_type=F32)
        merged = (jax.nn.sigmoid(ga_ref[:, cols].astype(F32)) * a
                  + jax.nn.sigmoid(gr_ref[:, cols].astype(F32)) * r)
        part = jnp.dot(merged.astype(BF16), wo_ref[cols, :], preferred_element_type=F32)
        if c == 0:
            o_ref[...] = part
        else:
            o_ref[...] += part
    _post_residual(st, i, x_ref, gate_ref, npost_ref[1:2, :], o_ref, 1.0)


def _merge(st, x, mod, npost, o_a, o_r, wide, w_pa, w_pr, w_o):
    gate_a_block = 2 * RET_V // D_MODEL
    row_spec = pl.BlockSpec((st.tile, D_MODEL), lambda i, j: (i, 0))
    resident = lambda shape: pl.BlockSpec(shape, lambda i, j: (0, 0), pipeline_mode=pl.Buffered(1))
    return pl.pallas_call(
        functools.partial(_merge_kernel, st),
        out_shape=jax.ShapeDtypeStruct((st.n_tokens, D_MODEL), F32),
        grid=(st.n_tiles, 1),
        in_specs=[
            row_spec,
            _mod_spec(st, 5),
            pl.BlockSpec((N_SUBLAYERS, D_MODEL), lambda i, j: (0, 0)),
            pl.BlockSpec((st.tile, ATT_Q), lambda i, j: (i, 0)),
            pl.BlockSpec((st.tile, RET_V), lambda i, j: (i, 0)),
            pl.BlockSpec((st.tile, D_MODEL), lambda i, j: (i, gate_a_block)),
            pl.BlockSpec((st.tile, D_MODEL), lambda i, j: (i, gate_a_block + 1)),
            resident((ATT_Q, D_MODEL)), resident((RET_V, D_MODEL)), resident((D_MODEL, D_MODEL)),
        ],
        out_specs=row_spec,
        compiler_params=pltpu.CompilerParams(
            dimension_semantics=("parallel", "arbitrary"), vmem_limit_bytes=V7X_VMEM_LIMIT_BYTES),
        name="merge_out",
    )(x, mod, npost, o_a, o_r, wide, wide, w_pa, w_pr, w_o)


def _rotation_tables(pos):
    half = RET_DK // 2
    inv_freq = ROPE_BASE ** (-jnp.linspace(0.0, 1.0, half, dtype=F32))
    ang = pos[:, None] * inv_freq[None, :]
    cos, sin = jnp.cos(ang), jnp.sin(ang)
    return jnp.concatenate([cos, cos], axis=-1), jnp.concatenate([-sin, sin], axis=-1)


def kernel(x_prompt, x_sample, cache_k_win, cache_v_win, state_ret, c_prompt, c_sample, w_ada, b_ada,
           norm_pre, norm_post, w_in, attn_sinks, w_pa, w_pr, w_o,
           ffn1_gate, ffn1_up, ffn1_down, ffn2_gate, ffn2_up, ffn2_down):
    batch, seq, _ = x_prompt.shape
    n_seq, t_new, _ = x_sample.shape
    assert w_ada.shape[0] == 1, "single-layer step"
    assert t_new == V7X_SUBLANES and seq % TOKEN_TILE == 0 and (n_seq * t_new) % TOKEN_TILE == 0
    assert seq % MERGE_TOKEN_TILE == 0 and (n_seq * t_new) % MERGE_TOKEN_TILE == 0
    assert batch <= MOD_PAD_ROWS and n_seq % MOD_PAD_ROWS == 0

    c_all = jnp.concatenate(
        [c_sample, c_prompt, jnp.zeros((MOD_PAD_ROWS - batch, D_MODEL), F32)], axis=0)
    mod = _ada(c_all, w_ada[0], b_ada[0])

    def prompt_stream(tile):
        return _Stream(batch * seq, tile, seq, MOD_PAD_ROWS, n_seq // MOD_PAD_ROWS)

    def sample_stream(tile):
        return _Stream(n_seq * t_new, tile, t_new, tile // t_new, 0)

    prompt, prompt_m = prompt_stream(TOKEN_TILE), prompt_stream(MERGE_TOKEN_TILE)
    sample, sample_m = sample_stream(TOKEN_TILE), sample_stream(MERGE_TOKEN_TILE // 2)

    npre, npost = norm_pre[0], norm_post[0]
    sinks = attn_sinks[0]
    bf = lambda w: w[0].astype(BF16)
    w_pa_b, w_pr_b, w_o_b = bf(w_pa), bf(w_pr), bf(w_o)

    xs = x_sample.reshape(n_seq * t_new, D_MODEL)
    xs, f1g, f1u, f1d = _ffn(sample, 0, xs, mod, npre, npost,
                             ffn1_gate[0], ffn1_up[0], ffn1_down[0], emit_bf16=True)
    qa_s, kva_s, qkr_s, wide_s, w_in_b = _proj(sample, xs, mod, npre, w_in[0], F32, emit_bf16=True)

    xp = x_prompt.reshape(batch * seq, D_MODEL)
    xp = _ffn(prompt, 0, xp, mod, npre, npost, f1g, f1u, f1d)
    qa, kva, qkr, wide = _proj(prompt, xp, mod, npre, w_in_b, BF16)
    o_a = _attn_prompt(qa, kva, sinks, batch, seq)
    cos_p, sin_p = _rotation_tables(jnp.arange(seq, dtype=F32))
    o_r, state_p = _ret_prompt(qkr, wide, cos_p, sin_p, batch, seq)
    xp = _merge(prompt_m, xp, mod, npost, o_a, o_r, wide, w_pa_b, w_pr_b, w_o_b)
    kva_p = kva.reshape(batch, seq, 2 * ATT_KV)[:, seq - WINDOW:]
    kv_shape = (1, batch, WINDOW, ATT_KV_HEADS, ATT_HEAD_DIM)
    k_win_p = kva_p[..., :ATT_KV].reshape(kv_shape)
    v_win_p = kva_p[..., ATT_KV:].reshape(kv_shape)

    cos_s, sin_s = _rotation_tables(jnp.arange(t_new, dtype=F32) + PAST_LEN)
    sink_rows = jnp.repeat(sinks, t_new)[:, None]
    o_a_s, k_s, v_s = _attn_sample(
        qa_s, kva_s,
        cache_k_win[0].reshape(n_seq, WINDOW, ATT_KV), cache_v_win[0].reshape(n_seq, WINDOW, ATT_KV),
        sink_rows, n_seq, t_new)
    o_r_s, state_s = _ret_sample(qkr_s, wide_s, cos_s, sin_s, state_ret[0], n_seq, t_new)
    xs = _merge(sample_m, xs, mod, npost, o_a_s, o_r_s, wide_s, w_pa_b, w_pr_b, w_o_b)
    xs, f2g, f2u, f2d = _ffn(sample, 2, xs, mod, npre, npost,
                             ffn2_gate[0], ffn2_up[0], ffn2_down[0], emit_bf16=True)
    xp = _ffn(prompt, 2, xp, mod, npre, npost, f2g, f2u, f2d)
    kvs_shape = (1, n_seq, WINDOW, ATT_KV_HEADS, ATT_HEAD_DIM)

    return (xp.reshape(batch, seq, D_MODEL), xs.reshape(n_seq, t_new, D_MODEL),
            k_win_p, v_win_p, state_p[None],
            k_s.reshape(kvs_shape), v_s.reshape(kvs_shape), state_s[None])
```

```python
import functools
import math
from typing import NamedTuple

import jax
import jax.numpy as jnp
import numpy as np
from jax import lax
from jax.experimental import pallas as pl
from jax.experimental.pallas import tpu as pltpu

F32 = jnp.float32
BF16 = jnp.bfloat16

D_MODEL = 2048
WINDOW = 128
ATT_HEADS = 16
ATT_KV_HEADS = 4
ATT_HEAD_DIM = 64
ATT_GROUP = ATT_HEADS // ATT_KV_HEADS
ATT_Q = ATT_HEADS * ATT_HEAD_DIM
ATT_KV = ATT_KV_HEADS * ATT_HEAD_DIM
RET_HEADS = 8
RET_DK = 128
RET_DV = 256
RET_CHUNK = 128
RET_QK = RET_HEADS * RET_DK
RET_V = RET_HEADS * RET_DV
ROPE_BASE = 10000.0
D_FF = 5632
NORM_EPS = 1e-6
N_SUBLAYERS = 3
PAST_LEN = 16384
D_IN = ATT_Q + 2 * ATT_KV + 2 * RET_QK + 2 * RET_V + 2 * D_MODEL

V7X_SUBLANES = 8
V7X_BF16_ROWS = 16
NORM_GROUP_ROWS = 128
V7X_VMEM_LIMIT_BYTES = 60 * 1024 * 1024

TOKEN_TILE = 1024
MERGE_TOKEN_TILE = 512
FF_SUBTILE = 256
FF_SUBTILES_BF16 = 2
F32_WEIGHT_K_PARTS = 4
COL_TILE = 512
ADA_COL_TILE = 1024
ATTN_BLOCKS_PER_STEP = 2
RET_CHUNKS_PER_STEP = 4
RET_SAMPLE_HEADS = 2
SAMPLE_SEQS_PER_STEP = 8
RET_SAMPLE_SEQS = 16
MOD_PAD_ROWS = 8

_QA_TILES = ATT_Q // COL_TILE
_KVA_TILES = 2 * ATT_KV // COL_TILE
_QKR_TILES = 2 * RET_QK // COL_TILE
_WIDE_COLS = 2 * RET_V + 2 * D_MODEL
_WIDE_TILES = _WIDE_COLS // COL_TILE
_QKR_START = _QA_TILES + _KVA_TILES
_WIDE_START = _QKR_START + _QKR_TILES
assert _WIDE_START + _WIDE_TILES == D_IN // COL_TILE and _KVA_TILES == 1

_ALIBI_SLOPES = [2.0 ** (-8.0 * (h + 1) / ATT_HEADS) for h in range(ATT_HEADS)]
_RET_LOG_G = [math.log(1.0 - 2.0 ** (-5.0 - h)) for h in range(RET_HEADS)]
_ATT_SCALE = ATT_HEAD_DIM ** -0.5
_RET_K_SCALE = RET_DK ** -0.5


class _Stream(NamedTuple):
    n_tokens: int
    tile: int
    rows_per_mod: int
    mod_block_rows: int
    mod_block_base: int

    @property
    def n_tiles(self):
        return self.n_tokens // self.tile

    @property
    def sub_rows(self):
        return min(self.rows_per_mod, self.tile)

    @property
    def group_rows(self):
        return max(self.sub_rows, NORM_GROUP_ROWS)

    @property
    def n_groups(self):
        return self.tile // self.group_rows

    @property
    def mods_per_group(self):
        return self.group_rows // self.sub_rows


def _rms(x):
    return x * lax.rsqrt(jnp.mean(x * x, axis=-1, keepdims=True) + NORM_EPS)


def _silu(x):
    return x * jax.nn.sigmoid(x)


def _for_groups(n_groups, fn):
    if n_groups == 1:
        fn(0)
    else:
        def body(g, carry):
            fn(g)
            return carry
        lax.fori_loop(0, n_groups, body, 0)


def _group_base(st, g):
    return 0 if st.n_groups == 1 else pl.multiple_of(g * st.group_rows, st.group_rows)


def _mod_row(st, i, g, s):
    if st.rows_per_mod >= st.tile:
        return (i * st.tile) // st.rows_per_mod
    return g * st.mods_per_group + s


def _pre_norm(st, i, x_ref, shift_ref, scale_ref, gain, h_ref):
    def group(g):
        base = _group_base(st, g)
        parts = []
        for s in range(st.mods_per_group):
            rows = pl.ds(base + s * st.sub_rows, st.sub_rows)
            m = _mod_row(st, i, g, s)
            x = x_ref[rows, :]
            row_gain = gain * (1.0 + scale_ref[pl.ds(m, 1), :])
            sh = shift_ref[pl.ds(m, 1), :]
            parts.append(_rms(x) * row_gain + sh)
        h = parts[0] if len(parts) == 1 else jnp.concatenate(parts, axis=0)
        h_ref[pl.ds(base, st.group_rows), :] = h.astype(BF16)
    _for_groups(st.n_groups, group)


def _post_residual(st, i, x_ref, gate_ref, gain, o_ref, coeff):
    def group(g):
        base = _group_base(st, g)
        results = []
        for s in range(st.mods_per_group):
            rows = pl.ds(base + s * st.sub_rows, st.sub_rows)
            m = _mod_row(st, i, g, s)
            row_gain = gate_ref[pl.ds(m, 1), :] * gain
            if coeff != 1.0:
                row_gain = coeff * row_gain
            results.append((rows, x_ref[rows, :] + _rms(o_ref[rows, :]) * row_gain))
        for rows, value in results:
            o_ref[rows, :] = value
    _for_groups(st.n_groups, group)


def _ada_kernel(c_ref, w_ref, b_ref, o_ref):
    a = _silu(c_ref[...]).astype(BF16)
    o_ref[0] = jnp.dot(a, w_ref[...].astype(BF16), preferred_element_type=F32) + b_ref[...]


def _ada(c_all, w_ada, b_ada):
    rows = c_all.shape[0]
    n_vec = N_SUBLAYERS * 3
    per_vec = D_MODEL // ADA_COL_TILE
    return pl.pallas_call(
        _ada_kernel,
        out_shape=jax.ShapeDtypeStruct((n_vec, rows, D_MODEL), F32),
        grid=(n_vec * per_vec,),
        in_specs=[
            pl.BlockSpec((rows, D_MODEL), lambda j: (0, 0)),
            pl.BlockSpec((D_MODEL, ADA_COL_TILE), lambda j: (0, j)),
            pl.BlockSpec((1, ADA_COL_TILE), lambda j: (0, j)),
        ],
        out_specs=pl.BlockSpec((1, rows, ADA_COL_TILE), lambda j: (j // per_vec, 0, j % per_vec)),
        compiler_params=pltpu.CompilerParams(
            dimension_semantics=("arbitrary",), vmem_limit_bytes=V7X_VMEM_LIMIT_BYTES),
        name="ada_mod",
    )(c_all, w_ada, b_ada.reshape(1, -1))


def _mod_spec(st, vec):
    if st.rows_per_mod >= st.tile:
        index = lambda i, j: (vec, st.mod_block_base, 0)
    else:
        index = lambda i, j: (vec, st.mod_block_base + i, 0)
    return pl.BlockSpec((None, st.mod_block_rows, D_MODEL), index)


def _ffn_kernel(st, sub, k_parts, emit_bf16, x_ref, shift_ref, scale_ref, gate_ref, npre_ref, npost_ref,
                *refs):
    i = pl.program_id(0)
    j = pl.program_id(1)
    n_up = max(k_parts, 1)
    wg_refs, wu_refs, wd_ref, o_ref = refs[:n_up], refs[n_up:2 * n_up], refs[2 * n_up], refs[2 * n_up + 1]
    rest = refs[2 * n_up + 2:]
    h_ref = rest[-1]

    def up_f32(h, w_refs, out_ref):
        rows = D_MODEL // k_parts
        acc = None
        for p, w_ref in enumerate(w_refs):
            w = w_ref[...].astype(BF16)
            if out_ref is not None:
                out_ref[p * rows:(p + 1) * rows, :] = w
            part = jnp.dot(h[:, p * rows:(p + 1) * rows], w, preferred_element_type=F32)
            acc = part if acc is None else acc + part
        return acc

    def partial_down():
        h = h_ref[...]
        if k_parts:
            wg_out_ref, wu_out_ref, wd_out_ref = rest[:3] if emit_bf16 else (None, None, None)
            wd = wd_ref[...].astype(BF16)
            if emit_bf16:
                wd_out_ref[...] = wd
            g = up_f32(h, wg_refs, wg_out_ref)
            u = up_f32(h, wu_refs, wu_out_ref)
            return jnp.dot((_silu(g) * u).astype(BF16), wd, preferred_element_type=F32)
        acts = []
        for t in range(wg_refs[0].shape[0]):
            g = jnp.dot(h, wg_refs[0][t], preferred_element_type=F32)
            u = jnp.dot(h, wu_refs[0][t], preferred_element_type=F32)
            acts.append((_silu(g) * u).astype(BF16))
        return jnp.dot(jnp.concatenate(acts, axis=1), wd_ref[...], preferred_element_type=F32)

    @pl.when(j == 0)
    def _():
        _pre_norm(st, i, x_ref, shift_ref, scale_ref, npre_ref[sub:sub + 1, :], h_ref)
        o_ref[...] = partial_down()

    last = pl.num_programs(1) - 1

    @pl.when((j > 0) & (j < last))
    def _():
        o_ref[...] += partial_down()

    @pl.when(j == last)
    def _():
        o_ref[...] += partial_down()
        _post_residual(st, i, x_ref, gate_ref, npost_ref[sub:sub + 1, :], o_ref, 0.5)


def _ffn(st, sub, x, mod, npre, npost, wg, wu, wd, emit_bf16=False):
    tiled = wg.ndim == 3
    n_sub = FF_SUBTILES_BF16 if tiled else 1
    ff_tile = n_sub * FF_SUBTILE
    row_spec = pl.BlockSpec((st.tile, D_MODEL), lambda i, j: (i, 0))
    full_spec = pl.BlockSpec((N_SUBLAYERS, D_MODEL), lambda i, j: (0, 0))
    tile_spec = pl.BlockSpec((None, D_MODEL, FF_SUBTILE), lambda i, j: (j, 0, 0))
    if tiled:
        k_parts = 0
        up_specs = [pl.BlockSpec((n_sub, D_MODEL, FF_SUBTILE), lambda i, j: (j, 0, 0))]
        up_args = lambda w: [w]
    else:
        k_parts = F32_WEIGHT_K_PARTS
        rows = D_MODEL // k_parts
        up_specs = [pl.BlockSpec((rows, ff_tile), functools.partial(lambda p, i, j: (p, j), p))
                    for p in range(k_parts)]
        up_args = lambda w: [w] * k_parts
    down_spec = pl.BlockSpec((ff_tile, D_MODEL), lambda i, j: (j, 0))
    out_shape = [jax.ShapeDtypeStruct((st.n_tokens, D_MODEL), F32)]
    out_specs = [row_spec]
    if emit_bf16:
        assert st.n_tiles == 1 and not tiled
        up_shape = jax.ShapeDtypeStruct((D_FF // FF_SUBTILE, D_MODEL, FF_SUBTILE), BF16)
        out_shape += [up_shape, up_shape, jax.ShapeDtypeStruct((D_FF, D_MODEL), BF16)]
        out_specs += [tile_spec, tile_spec, down_spec]
    outs = pl.pallas_call(
        functools.partial(_ffn_kernel, st, sub, k_parts, emit_bf16),
        out_shape=out_shape,
        grid=(st.n_tiles, D_FF // ff_tile),
        in_specs=[
            row_spec,
            _mod_spec(st, 3 * sub + 0), _mod_spec(st, 3 * sub + 1), _mod_spec(st, 3 * sub + 2),
            full_spec, full_spec,
            *up_specs, *up_specs, down_spec,
        ],
        out_specs=out_specs,
        scratch_shapes=[pltpu.VMEM((st.tile, D_MODEL), BF16)],
        compiler_params=pltpu.CompilerParams(
            dimension_semantics=("parallel", "arbitrary"), vmem_limit_bytes=V7X_VMEM_LIMIT_BYTES),
        name=f"ffn{sub}",
    )(x, mod, mod, mod, npre, npost, *up_args(wg), *up_args(wu), wd)
    return outs if emit_bf16 else outs[0]


def _proj_kernel(st, k_parts, emit_bf16, x_ref, shift_ref, scale_ref, npre_ref, *refs):
    i = pl.program_id(0)
    j = pl.program_id(1)
    w_refs = refs[:k_parts]
    qa_ref, kva_ref, qkr_ref, wide_ref = refs[k_parts:k_parts + 4]
    rest = refs[k_parts + 4:]
    h_ref = rest[-1]
    rows = D_MODEL // k_parts

    def project(dst_ref):
        h = h_ref[...]
        acc = None
        for p, w_ref in enumerate(w_refs):
            w = w_ref[...].astype(BF16)
            if emit_bf16:
                rest[0][p * rows:(p + 1) * rows, :] = w
            part = jnp.dot(h[:, p * rows:(p + 1) * rows], w, preferred_element_type=F32)
            acc = part if acc is None else acc + part
        dst_ref[...] = acc.astype(dst_ref.dtype)

    @pl.when(j == 0)
    def _():
        _pre_norm(st, i, x_ref, shift_ref, scale_ref, npre_ref[1:2, :], h_ref)
        project(qa_ref)

    @pl.when((j > 0) & (j < _QA_TILES))
    def _():
        project(qa_ref)

    @pl.when(j == _QA_TILES)
    def _():
        project(kva_ref)

    @pl.when((j >= _QKR_START) & (j < _WIDE_START))
    def _():
        project(qkr_ref)

    @pl.when(j >= _WIDE_START)
    def _():
        project(wide_ref)


def _proj(st, x, mod, npre, w_in, narrow_dtype, emit_bf16=False):
    n = st.n_tokens
    tiled = w_in.ndim == 3
    row_spec = pl.BlockSpec((st.tile, D_MODEL), lambda i, j: (i, 0))
    out_block = (st.tile, COL_TILE)
    tile_spec = pl.BlockSpec((None, D_MODEL, COL_TILE), lambda i, j: (j, 0, 0))
    out_shape = [
        jax.ShapeDtypeStruct((n, ATT_Q), narrow_dtype),
        jax.ShapeDtypeStruct((n, 2 * ATT_KV), F32),
        jax.ShapeDtypeStruct((n, 2 * RET_QK), F32),
        jax.ShapeDtypeStruct((n, _WIDE_COLS), narrow_dtype),
    ]
    out_specs = [
        pl.BlockSpec(out_block, lambda i, j: (i, jnp.minimum(j, _QA_TILES - 1))),
        pl.BlockSpec(out_block, lambda i, j: (i, 0)),
        pl.BlockSpec(out_block, lambda i, j: (i, jnp.clip(j - _QKR_START, 0, _QKR_TILES - 1))),
        pl.BlockSpec(out_block, lambda i, j: (i, jnp.maximum(j - _WIDE_START, 0))),
    ]
    if emit_bf16:
        assert st.n_tiles == 1 and not tiled
        out_shape.append(jax.ShapeDtypeStruct((D_IN // COL_TILE, D_MODEL, COL_TILE), BF16))
        out_specs.append(tile_spec)
    if tiled:
        k_parts = 1
        w_specs = [tile_spec]
    else:
        k_parts = F32_WEIGHT_K_PARTS
        w_specs = [pl.BlockSpec((D_MODEL // k_parts, COL_TILE), functools.partial(lambda p, i, j: (p, j), p))
                   for p in range(k_parts)]
    outs = pl.pallas_call(
        functools.partial(_proj_kernel, st, k_parts, emit_bf16),
        out_shape=out_shape,
        grid=(st.n_tiles, D_IN // COL_TILE),
        in_specs=[
            row_spec,
            _mod_spec(st, 3), _mod_spec(st, 4),
            pl.BlockSpec((N_SUBLAYERS, D_MODEL), lambda i, j: (0, 0)),
            *w_specs,
        ],
        out_specs=out_specs,
        scratch_shapes=[pltpu.VMEM((st.tile, D_MODEL), BF16)],
        compiler_params=pltpu.CompilerParams(
            dimension_semantics=("parallel", "arbitrary"), vmem_limit_bytes=V7X_VMEM_LIMIT_BYTES),
        name="in_proj",
    )(x, mod, mod, npre, *([w_in] * k_parts))
    return outs


def _attention(q, k2, v2, sinks_ref, first_valid_key):
    tq = q.shape[0]
    a_idx = lax.broadcasted_iota(jnp.int32, (tq, 2 * WINDOW), 0)
    b_idx = lax.broadcasted_iota(jnp.int32, (tq, 2 * WINDOW), 1)
    dist = WINDOW + a_idx - b_idx
    mask = (dist >= 0) & (dist <= WINDOW) & (b_idx >= first_valid_key)
    dist_f = jnp.where(mask, dist.astype(F32), jnp.inf)
    q = q * _ATT_SCALE
    outs = []
    for kv in range(ATT_KV_HEADS):
        cols = slice(kv * ATT_HEAD_DIM, (kv + 1) * ATT_HEAD_DIM)
        kk = k2[:, cols]
        vv = v2[:, cols]
        heads = range(kv * ATT_GROUP, (kv + 1) * ATT_GROUP)
        qg = jnp.concatenate(
            [q[:, h * ATT_HEAD_DIM:(h + 1) * ATT_HEAD_DIM] for h in heads], axis=0).astype(BF16)
        s_all = lax.dot_general(qg, kk, (((1,), (1,)), ((), ())), preferred_element_type=F32)
        probs = []
        for g, h in enumerate(heads):
            s = s_all[g * tq:(g + 1) * tq] - _ALIBI_SLOPES[h] * dist_f
            sink = sinks_ref[h]
            m = jnp.maximum(jnp.max(s, axis=-1, keepdims=True), sink)
            p = jnp.exp(s - m)
            inv = 1.0 / (jnp.sum(p, axis=-1, keepdims=True) + jnp.exp(sink - m))
            probs.append((p * inv).astype(BF16))
        o_all = jnp.dot(jnp.concatenate(probs, axis=0), vv, preferred_element_type=F32)
        outs.extend(o_all[g * tq:(g + 1) * tq] for g in range(ATT_GROUP))
    return jnp.concatenate(outs, axis=-1)


def _attn_prompt_kernel(sinks_ref, q_ref, kc_ref, vc_ref, kp_ref, vp_ref, o_ref):
    step = pl.program_id(1)
    k_tiles = [kp_ref[...]] + [kc_ref[r * WINDOW:(r + 1) * WINDOW, :] for r in range(ATTN_BLOCKS_PER_STEP)]
    v_tiles = [vp_ref[...]] + [vc_ref[r * WINDOW:(r + 1) * WINDOW, :] for r in range(ATTN_BLOCKS_PER_STEP)]
    for r in range(ATTN_BLOCKS_PER_STEP):
        rows = slice(r * WINDOW, (r + 1) * WINDOW)
        k2 = jnp.concatenate(k_tiles[r:r + 2], axis=0).astype(BF16)
        v2 = jnp.concatenate(v_tiles[r:r + 2], axis=0).astype(BF16)
        first_valid = jnp.where(step == 0, WINDOW, 0) if r == 0 else 0
        o_ref[rows, :] = _attention(q_ref[rows, :], k2, v2, sinks_ref, first_valid).astype(o_ref.dtype)


def _attn_prompt(qa, kva, sinks, batch, seq):
    per = ATTN_BLOCKS_PER_STEP
    ns = seq // (per * WINDOW)
    cur = lambda col: (lambda b, i: (b * ns + i, col))
    prev = lambda col: (lambda b, i: (b * ns * per + jnp.maximum(i * per - 1, 0), col))
    return pl.pallas_call(
        _attn_prompt_kernel,
        out_shape=jax.ShapeDtypeStruct((batch * seq, ATT_Q), BF16),
        grid=(batch, ns),
        in_specs=[
            pl.BlockSpec(memory_space=pltpu.SMEM),
            pl.BlockSpec((per * WINDOW, ATT_Q), lambda b, i: (b * ns + i, 0)),
            pl.BlockSpec((per * WINDOW, ATT_KV), cur(0)),
            pl.BlockSpec((per * WINDOW, ATT_KV), cur(1)),
            pl.BlockSpec((WINDOW, ATT_KV), prev(0)),
            pl.BlockSpec((WINDOW, ATT_KV), prev(1)),
        ],
        out_specs=pl.BlockSpec((per * WINDOW, ATT_Q), lambda b, i: (b * ns + i, 0)),
        compiler_params=pltpu.CompilerParams(
            dimension_semantics=("parallel", "arbitrary"), vmem_limit_bytes=V7X_VMEM_LIMIT_BYTES),
        name="attn_prompt",
    )(sinks, qa, kva, kva, kva, kva)


def _rotate(x, cos2, sin2):
    return x * cos2 + pltpu.roll(x, RET_DK // 2, axis=1) * sin2


def _ret_prompt_kernel(q_ref, k_ref, v_ref, gr_ref, cos_ref, sin_ref, o_ref, s_ref,
                       decay_ref, qw_ref, kw_ref):
    @pl.when(pl.program_id(1) == 0)
    def _():
        s_ref[...] = jnp.zeros_like(s_ref)
        row = lax.broadcasted_iota(jnp.int32, (RET_CHUNK, RET_CHUNK), 0).astype(F32)
        col = lax.broadcasted_iota(jnp.int32, (RET_CHUNK, RET_CHUNK), 1).astype(F32)
        diff = row - col
        for h in range(RET_HEADS):
            log_g = _RET_LOG_G[h]
            decay_ref[h] = jnp.where(diff >= 0, jnp.exp(jnp.maximum(diff, 0.0) * log_g), 0.0)
            qw_ref[h] = jnp.exp((row + 1.0) * log_g)
            kw_ref[h] = jnp.exp((RET_CHUNK - 1.0 - row) * log_g)

    for r in range(RET_CHUNKS_PER_STEP):
        rows = slice(r * RET_CHUNK, (r + 1) * RET_CHUNK)
        cos2, sin2 = cos_ref[rows, :], sin_ref[rows, :]
        for h in range(RET_HEADS):
            qk_cols = slice(h * RET_DK, (h + 1) * RET_DK)
            v_cols = slice(h * RET_DV, (h + 1) * RET_DV)
            qh = _rotate(q_ref[rows, qk_cols], cos2, sin2)
            kh = _rotate(k_ref[rows, qk_cols], cos2, sin2) * _RET_K_SCALE
            vb = v_ref[rows, v_cols].astype(BF16)
            s_prev = s_ref[0, h]
            scores = lax.dot_general(qh.astype(BF16), kh.astype(BF16), (((1,), (1,)), ((), ())),
                                     preferred_element_type=F32) * decay_ref[h]
            y = jnp.dot(scores.astype(BF16), vb, preferred_element_type=F32)
            y = y + jnp.dot((qh * qw_ref[h]).astype(BF16), s_prev.astype(BF16),
                            preferred_element_type=F32)
            kt = (kh * kw_ref[h]).T.astype(BF16)
            s_ref[0, h] = (math.exp(RET_CHUNK * _RET_LOG_G[h]) * s_prev
                           + jnp.dot(kt, vb, preferred_element_type=F32))
            gate = gr_ref[rows, v_cols].astype(F32)
            o_ref[rows, v_cols] = (_silu(gate) * _rms(y)).astype(o_ref.dtype)


def _ret_prompt(qkr, wide, cos2, sin2, batch, seq):
    step_rows = RET_CHUNKS_PER_STEP * RET_CHUNK
    nc = seq // step_rows
    rows = lambda col: (lambda b, c: (b * nc + c, col))
    return pl.pallas_call(
        _ret_prompt_kernel,
        out_shape=(
            jax.ShapeDtypeStruct((batch * seq, RET_V), BF16),
            jax.ShapeDtypeStruct((batch, RET_HEADS, RET_DK, RET_DV), F32),
        ),
        grid=(batch, nc),
        in_specs=[
            pl.BlockSpec((step_rows, RET_QK), rows(0)),
            pl.BlockSpec((step_rows, RET_QK), rows(1)),
            pl.BlockSpec((step_rows, RET_V), rows(0)),
            pl.BlockSpec((step_rows, RET_V), rows(1)),
            pl.BlockSpec((step_rows, RET_DK), lambda b, c: (c, 0)),
            pl.BlockSpec((step_rows, RET_DK), lambda b, c: (c, 0)),
        ],
        out_specs=(
            pl.BlockSpec((step_rows, RET_V), rows(0)),
            pl.BlockSpec((1, RET_HEADS, RET_DK, RET_DV), lambda b, c: (b, 0, 0, 0)),
        ),
        scratch_shapes=[pltpu.VMEM((RET_HEADS, RET_CHUNK, RET_CHUNK), F32)] * 3,
        compiler_params=pltpu.CompilerParams(
            dimension_semantics=("parallel", "arbitrary"), vmem_limit_bytes=V7X_VMEM_LIMIT_BYTES),
        name="ret_prompt",
    )(qkr, qkr, wide, wide, cos2, sin2)


def _attn_sample_kernel(t_new, sink_ref, qa_ref, kn_ref, vn_ref, ck_ref, cv_ref,
                        oa_ref, ko_ref, vo_ref):
    n_rows = ATT_HEADS * t_new
    pair = 2 * ATT_HEAD_DIM
    row = lax.broadcasted_iota(jnp.int32, (n_rows, 2 * WINDOW), 0)
    key = lax.broadcasted_iota(jnp.int32, (n_rows, 2 * WINDOW), 1)
    head = row // t_new
    dist = WINDOW + (row - head * t_new) - key
    slope = jnp.exp2((head.astype(F32) + 1.0) * (-8.0 / ATT_HEADS))
    bias = jnp.where((dist >= 0) & (dist <= WINDOW), -slope * dist.astype(F32), -jnp.inf)
    sink = sink_ref[...]
    lower = lax.broadcasted_iota(jnp.int32, (t_new, pair), 1) < ATT_HEAD_DIM
    zero_group = jnp.zeros((t_new, pair), F32)
    zero_keys = jnp.zeros((WINDOW - t_new, ATT_KV), F32)

    for n in range(SAMPLE_SEQS_PER_STEP):
        rows = slice(n * t_new, (n + 1) * t_new)
        kc, vc = ck_ref[n], cv_ref[n]
        kn, vn = kn_ref[rows, :], vn_ref[rows, :]
        ko_ref[n, :WINDOW - t_new, :] = kc[t_new:]
        ko_ref[n, WINDOW - t_new:, :] = kn
        vo_ref[n, :WINDOW - t_new, :] = vc[t_new:]
        vo_ref[n, WINDOW - t_new:, :] = vn
        k2 = jnp.concatenate([kc, kn, zero_keys], axis=0).astype(BF16)
        v2 = jnp.concatenate([vc, vn, zero_keys], axis=0).astype(BF16)

        q = qa_ref[rows, :] * _ATT_SCALE
        q_swapped = pltpu.roll(q, ATT_HEAD_DIM, axis=1)
        blocks = []
        for h in range(ATT_HEADS):
            kv = h // ATT_GROUP
            want_lower = kv % 2 == 0
            if (h % 2 == 0) == want_lower:
                src = q[:, (h // 2) * pair:(h // 2 + 1) * pair]
            else:
                g = (h + 1) // 2 % (ATT_HEADS // 2)
                src = q_swapped[:, g * pair:(g + 1) * pair]
            piece = jnp.where(lower if want_lower else ~lower, src, 0.0)
            blocks.append(jnp.concatenate(
                [piece, zero_group] if kv // 2 == 0 else [zero_group, piece], axis=1))
        q_bd = jnp.concatenate(blocks, axis=0).astype(BF16)

        s = lax.dot_general(q_bd, k2, (((1,), (1,)), ((), ())), preferred_element_type=F32) + bias
        m = jnp.maximum(jnp.max(s, axis=-1, keepdims=True), sink)
        p = jnp.exp(s - m)
        inv = 1.0 / (jnp.sum(p, axis=-1, keepdims=True) + jnp.exp(sink - m))
        o = jnp.dot((p * inv).astype(BF16), v2, preferred_element_type=F32)

        outs = []
        for g in range(ATT_HEADS // 2):
            h0 = 2 * g
            kv = h0 // ATT_GROUP
            cols = slice((kv // 2) * pair, (kv // 2 + 1) * pair)
            a = o[h0 * t_new:(h0 + 1) * t_new, cols]
            b = o[(h0 + 1) * t_new:(h0 + 2) * t_new, cols]
            if kv % 2 == 0:
                b = pltpu.roll(b, ATT_HEAD_DIM, axis=1)
            else:
                a = pltpu.roll(a, ATT_HEAD_DIM, axis=1)
            outs.append(jnp.where(lower, a, b))
        oa_ref[rows, :] = jnp.concatenate(outs, axis=1)


def _attn_sample(qa, kva, cache_k, cache_v, sink_rows, n_seq, t_new):
    nb = SAMPLE_SEQS_PER_STEP
    rows = lambda col: (lambda n: (n, col))
    cache_spec = pl.BlockSpec((nb, WINDOW, ATT_KV), lambda n: (n, 0, 0))
    return pl.pallas_call(
        functools.partial(_attn_sample_kernel, t_new),
        out_shape=(
            jax.ShapeDtypeStruct((n_seq * t_new, ATT_Q), F32),
            jax.ShapeDtypeStruct(cache_k.shape, F32),
            jax.ShapeDtypeStruct(cache_v.shape, F32),
        ),
        grid=(n_seq // nb,),
        in_specs=[
            pl.BlockSpec((ATT_HEADS * t_new, 1), lambda n: (0, 0)),
            pl.BlockSpec((nb * t_new, ATT_Q), rows(0)),
            pl.BlockSpec((nb * t_new, ATT_KV), rows(0)),
            pl.BlockSpec((nb * t_new, ATT_KV), rows(1)),
            cache_spec, cache_spec,
        ],
        out_specs=(pl.BlockSpec((nb * t_new, ATT_Q), rows(0)), cache_spec, cache_spec),
        compiler_params=pltpu.CompilerParams(
            dimension_semantics=("parallel",), vmem_limit_bytes=V7X_VMEM_LIMIT_BYTES),
        name="attn_sample",
    )(sink_rows, qa, kva, kva, cache_k, cache_v)


def _ret_sample_kernel(t_new, log_g_ref, g_chunk_ref, q_ref, k_ref, v_ref, gr_ref, cos_ref, sin_ref,
                       s_in_ref, o_ref, s_out_ref):
    n_seq = s_in_ref.shape[0]
    n_rows = n_seq * t_new
    cos2, sin2 = cos_ref[...], sin_ref[...]
    row = lax.broadcasted_iota(jnp.int32, (n_rows, n_rows), 0)
    col = lax.broadcasted_iota(jnp.int32, (n_rows, n_rows), 1)
    row_seq, col_seq = row // t_new, col // t_new
    same_chunk_causal = (row_seq == col_seq) & (row >= col)
    diff = jnp.maximum((row - col).astype(F32), 0.0)
    t = (row - row_seq * t_new).astype(F32)

    for hh in range(RET_SAMPLE_HEADS):
        h = pl.program_id(1) * RET_SAMPLE_HEADS + hh
        log_g = log_g_ref[h]
        g_chunk = g_chunk_ref[h]
        qk_cols = slice(hh * RET_DK, (hh + 1) * RET_DK)
        v_cols = slice(hh * RET_DV, (hh + 1) * RET_DV)
        q = _rotate(q_ref[:, qk_cols], cos2, sin2)
        k = _rotate(k_ref[:, qk_cols], cos2, sin2) * _RET_K_SCALE
        vb = v_ref[:, v_cols].astype(BF16)

        decay = jnp.where(same_chunk_causal, jnp.exp(diff * log_g), 0.0)
        scores = lax.dot_general(q.astype(BF16), k.astype(BF16), (((1,), (1,)), ((), ())),
                                 preferred_element_type=F32) * decay
        y = jnp.dot(scores.astype(BF16), vb, preferred_element_type=F32)

        qw = q * jnp.exp((t + 1.0) * log_g)
        y_cross = [jnp.dot(qw[n * t_new:(n + 1) * t_new].astype(BF16), s_in_ref[n, hh].astype(BF16),
                           preferred_element_type=F32) for n in range(n_seq)]
        y = y + jnp.concatenate(y_cross, axis=0)

        kt = (k * jnp.exp((t_new - 1.0 - t) * log_g)).T
        lhs = jnp.concatenate([jnp.where(col_seq == n, kt, 0.0) for n in range(n_seq)], axis=0)
        kv = jnp.dot(lhs.astype(BF16), vb, preferred_element_type=F32)
        for n in range(n_seq):
            s_out_ref[n, hh] = g_chunk * s_in_ref[n, hh] + kv[n * RET_DK:(n + 1) * RET_DK]

        o_ref[:, v_cols] = _silu(gr_ref[:, v_cols]) * _rms(y)


def _ret_sample(qkr, wide, cos2, sin2, state, n_seq, t_new):
    nb = RET_SAMPLE_SEQS
    nh = RET_SAMPLE_HEADS
    n_rows = nb * t_new
    assert n_rows == RET_DK
    log_g = jnp.asarray(_RET_LOG_G, F32)
    g_chunk = jnp.asarray([math.exp(t_new * g) for g in _RET_LOG_G], F32)
    smem = pl.BlockSpec(memory_space=pltpu.SMEM)
    table_spec = pl.BlockSpec((n_rows, RET_DK), lambda g, h: (0, 0))
    state_spec = pl.BlockSpec((nb, nh, RET_DK, RET_DV), lambda g, h: (g, h, 0, 0))
    return pl.pallas_call(
        functools.partial(_ret_sample_kernel, t_new),
        out_shape=(
            jax.ShapeDtypeStruct((n_seq * t_new, RET_V), F32),
            jax.ShapeDtypeStruct(state.shape, F32),
        ),
        grid=(n_seq // nb, RET_HEADS // nh),
        in_specs=[
            smem, smem,
            pl.BlockSpec((n_rows, nh * RET_DK), lambda g, h: (g, h)),
            pl.BlockSpec((n_rows, nh * RET_DK), lambda g, h: (g, RET_HEADS // nh + h)),
            pl.BlockSpec((n_rows, nh * RET_DV), lambda g, h: (g, h)),
            pl.BlockSpec((n_rows, nh * RET_DV), lambda g, h: (g, RET_HEADS // nh + h)),
            table_spec, table_spec,
            state_spec,
        ],
        out_specs=(pl.BlockSpec((n_rows, nh * RET_DV), lambda g, h: (g, h)), state_spec),
        compiler_params=pltpu.CompilerParams(
            dimension_semantics=("parallel", "arbitrary"), vmem_limit_bytes=V7X_VMEM_LIMIT_BYTES),
        name="ret_sample",
    )(log_g, g_chunk, qkr, qkr, wide, wide, jnp.tile(cos2, (nb, 1)), jnp.tile(sin2, (nb, 1)), state)


def _merge_kernel(st, x_ref, gate_ref, npost_ref, oa_ref, or_ref, ga_ref, gr_ref,
                  wpa_ref, wpr_ref, wo_ref, o_ref):
    i = pl.program_id(0)
    oa = oa_ref[...].astype(BF16)
    orr = or_ref[...].astype(BF16)
    for c in range(D_MODEL // COL_TILE):
        cols = slice(c * COL_TILE, (c + 1) * COL_TILE)
        a = jnp.dot(oa, wpa_ref[:, cols], preferred_element_type=F32)
        r = jnp.dot(orr, wpr_ref[:, cols], preferred_element_type=F32)
        merged = (jax.nn.sigmoid(ga_ref[:, cols].astype(F32)) * a
                  + jax.nn.sigmoid(gr_ref[:, cols].astype(F32)) * r)
        part = jnp.dot(merged.astype(BF16), wo_ref[cols, :], preferred_element_type=F32)
        if c == 0:
            o_ref[...] = part
        else:
            o_ref[...] += part
    _post_residual(st, i, x_ref, gate_ref, npost_ref[1:2, :], o_ref, 1.0)


def _merge(st, x, mod, npost, o_a, o_r, wide, w_pa, w_pr, w_o):
    gate_a_block = 2 * RET_V // D_MODEL
    row_spec = pl.BlockSpec((st.tile, D_MODEL), lambda i, j: (i, 0))
    resident = lambda shape: pl.BlockSpec(shape, lambda i, j: (0, 0), pipeline_mode=pl.Buffered(1))
    return pl.pallas_call(
        functools.partial(_merge_kernel, st),
        out_shape=jax.ShapeDtypeStruct((st.n_tokens, D_MODEL), F32),
        grid=(st.n_tiles, 1),
        in_specs=[
            row_spec,
            _mod_spec(st, 5),
            pl.BlockSpec((N_SUBLAYERS, D_MODEL), lambda i, j: (0, 0)),
            pl.BlockSpec((st.tile, ATT_Q), lambda i, j: (i, 0)),
            pl.BlockSpec((st.tile, RET_V), lambda i, j: (i, 0)),
            pl.BlockSpec((st.tile, D_MODEL), lambda i, j: (i, gate_a_block)),
            pl.BlockSpec((st.tile, D_MODEL), lambda i, j: (i, gate_a_block + 1)),
            resident((ATT_Q, D_MODEL)), resident((RET_V, D_MODEL)), resident((D_MODEL, D_MODEL)),
        ],
        out_specs=row_spec,
        compiler_params=pltpu.CompilerParams(
            dimension_semantics=("parallel", "arbitrary"), vmem_limit_bytes=V7X_VMEM_LIMIT_BYTES),
        name="merge_out",
    )(x, mod, npost, o_a, o_r, wide, wide, w_pa, w_pr, w_o)


def _rotation_tables(pos):
    half = RET_DK // 2
    inv_freq = ROPE_BASE ** (-jnp.linspace(0.0, 1.0, half, dtype=F32))
    ang = pos[:, None] * inv_freq[None, :]
    cos, sin = jnp.cos(ang), jnp.sin(ang)
    return jnp.concatenate([cos, cos], axis=-1), jnp.concatenate([-sin, sin], axis=-1)


def kernel(x_prompt, x_sample, cache_k_win, cache_v_win, state_ret, c_prompt, c_sample, w_ada, b_ada,
           norm_pre, norm_post, w_in, attn_sinks, w_pa, w_pr, w_o,
           ffn1_gate, ffn1_up, ffn1_down, ffn2_gate, ffn2_up, ffn2_down):
    batch, seq, _ = x_prompt.shape
    n_seq, t_new, _ = x_sample.shape
    assert w_ada.shape[0] == 1, "single-layer step"
    assert t_new == V7X_SUBLANES and seq % TOKEN_TILE == 0 and (n_seq * t_new) % TOKEN_TILE == 0
    assert seq % MERGE_TOKEN_TILE == 0 and (n_seq * t_new) % MERGE_TOKEN_TILE == 0
    assert batch <= MOD_PAD_ROWS and n_seq % MOD_PAD_ROWS == 0

    c_all = jnp.concatenate(
        [c_sample, c_prompt, jnp.zeros((MOD_PAD_ROWS - batch, D_MODEL), F32)], axis=0)
    mod = _ada(c_all, w_ada[0], b_ada[0])

    def prompt_stream(tile):
        return _Stream(batch * seq, tile, seq, MOD_PAD_ROWS, n_seq // MOD_PAD_ROWS)

    def sample_stream(tile):
        return _Stream(n_seq * t_new, tile, t_new, tile // t_new, 0)

    prompt, prompt_m = prompt_stream(TOKEN_TILE), prompt_stream(MERGE_TOKEN_TILE)
    sample, sample_m = sample_stream(TOKEN_TILE), sample_stream(MERGE_TOKEN_TILE // 2)

    npre, npost = norm_pre[0], norm_post[0]
    sinks = attn_sinks[0]
    bf = lambda w: w[0].astype(BF16)
    w_pa_b, w_pr_b, w_o_b = bf(w_pa), bf(w_pr), bf(w_o)

    xs = x_sample.reshape(n_seq * t_new, D_MODEL)
    xs, f1g, f1u, f1d = _ffn(sample, 0, xs, mod, npre, npost,
                             ffn1_gate[0], ffn1_up[0], ffn1_down[0], emit_bf16=True)
    qa_s, kva_s, qkr_s, wide_s, w_in_b = _proj(sample, xs, mod, npre, w_in[0], F32, emit_bf16=True)

    xp = x_prompt.reshape(batch * seq, D_MODEL)
    xp = _ffn(prompt, 0, xp, mod, npre, npost, f1g, f1u, f1d)
    qa, kva, qkr, wide = _proj(prompt, xp, mod, npre, w_in_b, BF16)
    o_a = _attn_prompt(qa, kva, sinks, batch, seq)
    cos_p, sin_p = _rotation_tables(jnp.arange(seq, dtype=F32))
    o_r, state_p = _ret_prompt(qkr, wide, cos_p, sin_p, batch, seq)
    xp = _merge(prompt_m, xp, mod, npost, o_a, o_r, wide, w_pa_b, w_pr_b, w_o_b)
    kva_p = kva.reshape(batch, seq, 2 * ATT_KV)[:, seq - WINDOW:]
    kv_shape = (1, batch, WINDOW, ATT_KV_HEADS, ATT_HEAD_DIM)
    k_win_p = kva_p[..., :ATT_KV].reshape(kv_shape)
    v_win_p = kva_p[..., ATT_KV:].reshape(kv_shape)

    cos_s, sin_s = _rotation_tables(jnp.arange(t_new, dtype=F32) + PAST_LEN)
    sink_rows = jnp.repeat(sinks, t_new)[:, None]
    o_a_s, k_s, v_s = _attn_sample(
        qa_s, kva_s,
        cache_k_win[0].reshape(n_seq, WINDOW, ATT_KV), cache_v_win[0].reshape(n_seq, WINDOW, ATT_KV),
        sink_rows, n_seq, t_new)
    o_r_s, state_s = _ret_sample(qkr_s, wide_s, cos_s, sin_s, state_ret[0], n_seq, t_new)
    xs = _merge(sample_m, xs, mod, npost, o_a_s, o_r_s, wide_s, w_pa_b, w_pr_b, w_o_b)
    xs, f2g, f2u, f2d = _ffn(sample, 2, xs, mod, npre, npost,
                             ffn2_gate[0], ffn2_up[0], ffn2_down[0], emit_bf16=True)
    xp = _ffn(prompt, 2, xp, mod, npre, npost, f2g, f2u, f2d)
    kvs_shape = (1, n_seq, WINDOW, ATT_KV_HEADS, ATT_HEAD_DIM)

    return (xp.reshape(batch, seq, D_MODEL), xs.reshape(n_seq, t_new, D_MODEL),
            k_win_p, v_win_p, state_p[None],
            k_s.reshape(kvs_shape), v_s.reshape(kvs_shape), state_s[None])
```

```python
import functools
import math
from typing import NamedTuple

import jax
import jax.numpy as jnp
import numpy as np
from jax import lax
from jax.experimental import pallas as pl
from jax.experimental.pallas import tpu as pltpu

F32 = jnp.float32
BF16 = jnp.bfloat16

D_MODEL = 2048
WINDOW = 128
ATT_HEADS = 16
ATT_KV_HEADS = 4
ATT_HEAD_DIM = 64
ATT_GROUP = ATT_HEADS // ATT_KV_HEADS
ATT_Q = ATT_HEADS * ATT_HEAD_DIM
ATT_KV = ATT_KV_HEADS * ATT_HEAD_DIM
RET_HEADS = 8
RET_DK = 128
RET_DV = 256
RET_CHUNK = 128
RET_QK = RET_HEADS * RET_DK
RET_V = RET_HEADS * RET_DV
ROPE_BASE = 10000.0
D_FF = 5632
NORM_EPS = 1e-6
N_SUBLAYERS = 3
PAST_LEN = 16384
D_IN = ATT_Q + 2 * ATT_KV + 2 * RET_QK + 2 * RET_V + 2 * D_MODEL

V7X_SUBLANES = 8
V7X_BF16_ROWS = 16
NORM_GROUP_ROWS = 128
V7X_VMEM_LIMIT_BYTES = 60 * 1024 * 1024

TOKEN_TILE = 1024
MERGE_TOKEN_TILE = 512
FF_SUBTILE = 256
FF_SUBTILES_BF16 = 2
F32_WEIGHT_K_PARTS = 4
COL_TILE = 512
ADA_COL_TILE = 1024
ATTN_BLOCKS_PER_STEP = 2
RET_CHUNKS_PER_STEP = 4
SAMPLE_SEQS_PER_STEP = 8
RET_SAMPLE_SEQS = 16
MOD_PAD_ROWS = 8

_QA_TILES = ATT_Q // COL_TILE
_KVA_TILES = 2 * ATT_KV // COL_TILE
_QKR_TILES = 2 * RET_QK // COL_TILE
_WIDE_COLS = 2 * RET_V + 2 * D_MODEL
_WIDE_TILES = _WIDE_COLS // COL_TILE
_QKR_START = _QA_TILES + _KVA_TILES
_WIDE_START = _QKR_START + _QKR_TILES
assert _WIDE_START + _WIDE_TILES == D_IN // COL_TILE and _KVA_TILES == 1

_ALIBI_SLOPES = [2.0 ** (-8.0 * (h + 1) / ATT_HEADS) for h in range(ATT_HEADS)]
_RET_LOG_G = [math.log(1.0 - 2.0 ** (-5.0 - h)) for h in range(RET_HEADS)]
_ATT_SCALE = ATT_HEAD_DIM ** -0.5
_RET_K_SCALE = RET_DK ** -0.5


class _Stream(NamedTuple):
    n_tokens: int
    tile: int
    rows_per_mod: int
    mod_block_rows: int
    mod_block_base: int

    @property
    def n_tiles(self):
        return self.n_tokens // self.tile

    @property
    def sub_rows(self):
        return min(self.rows_per_mod, self.tile)

    @property
    def group_rows(self):
        return max(self.sub_rows, NORM_GROUP_ROWS)

    @property
    def n_groups(self):
        return self.tile // self.group_rows

    @property
    def mods_per_group(self):
        return self.group_rows // self.sub_rows


def _rms(x):
    return x * lax.rsqrt(jnp.mean(x * x, axis=-1, keepdims=True) + NORM_EPS)


def _silu(x):
    return x * jax.nn.sigmoid(x)


def _for_groups(n_groups, fn):
    if n_groups == 1:
        fn(0)
    else:
        def body(g, carry):
            fn(g)
            return carry
        lax.fori_loop(0, n_groups, body, 0)


def _group_base(st, g):
    return 0 if st.n_groups == 1 else pl.multiple_of(g * st.group_rows, st.group_rows)


def _mod_row(st, i, g, s):
    if st.rows_per_mod >= st.tile:
        return (i * st.tile) // st.rows_per_mod
    return g * st.mods_per_group + s


def _pre_norm(st, i, x_ref, shift_ref, scale_ref, gain, h_ref):
    def group(g):
        base = _group_base(st, g)
        parts = []
        for s in range(st.mods_per_group):
            rows = pl.ds(base + s * st.sub_rows, st.sub_rows)
            m = _mod_row(st, i, g, s)
            x = x_ref[rows, :]
            row_gain = gain * (1.0 + scale_ref[pl.ds(m, 1), :])
            sh = shift_ref[pl.ds(m, 1), :]
            parts.append(_rms(x) * row_gain + sh)
        h = parts[0] if len(parts) == 1 else jnp.concatenate(parts, axis=0)
        h_ref[pl.ds(base, st.group_rows), :] = h.astype(BF16)
    _for_groups(st.n_groups, group)


def _post_residual(st, i, x_ref, gate_ref, gain, o_ref, coeff):
    def group(g):
        base = _group_base(st, g)
        results = []
        for s in range(st.mods_per_group):
            rows = pl.ds(base + s * st.sub_rows, st.sub_rows)
            m = _mod_row(st, i, g, s)
            row_gain = gate_ref[pl.ds(m, 1), :] * gain
            if coeff != 1.0:
                row_gain = coeff * row_gain
            results.append((rows, x_ref[rows, :] + _rms(o_ref[rows, :]) * row_gain))
        for rows, value in results:
            o_ref[rows, :] = value
    _for_groups(st.n_groups, group)


def _ada_kernel(c_ref, w_ref, b_ref, o_ref):
    a = _silu(c_ref[...]).astype(BF16)
    o_ref[0] = jnp.dot(a, w_ref[...].astype(BF16), preferred_element_type=F32) + b_ref[...]


def _ada(c_all, w_ada, b_ada):
    rows = c_all.shape[0]
    n_vec = N_SUBLAYERS * 3
    per_vec = D_MODEL // ADA_COL_TILE
    return pl.pallas_call(
        _ada_kernel,
        out_shape=jax.ShapeDtypeStruct((n_vec, rows, D_MODEL), F32),
        grid=(n_vec * per_vec,),
        in_specs=[
            pl.BlockSpec((rows, D_MODEL), lambda j: (0, 0)),
            pl.BlockSpec((D_MODEL, ADA_COL_TILE), lambda j: (0, j)),
            pl.BlockSpec((1, ADA_COL_TILE), lambda j: (0, j)),
        ],
        out_specs=pl.BlockSpec((1, rows, ADA_COL_TILE), lambda j: (j // per_vec, 0, j % per_vec)),
        compiler_params=pltpu.CompilerParams(
            dimension_semantics=("arbitrary",), vmem_limit_bytes=V7X_VMEM_LIMIT_BYTES),
        name="ada_mod",
    )(c_all, w_ada, b_ada.reshape(1, -1))


def _mod_spec(st, vec):
    if st.rows_per_mod >= st.tile:
        index = lambda i, j: (vec, st.mod_block_base, 0)
    else:
        index = lambda i, j: (vec, st.mod_block_base + i, 0)
    return pl.BlockSpec((None, st.mod_block_rows, D_MODEL), index)


def _ffn_kernel(st, sub, k_parts, emit_bf16, x_ref, shift_ref, scale_ref, gate_ref, npre_ref, npost_ref,
                *refs):
    i = pl.program_id(0)
    j = pl.program_id(1)
    n_up = max(k_parts, 1)
    wg_refs, wu_refs, wd_ref, o_ref = refs[:n_up], refs[n_up:2 * n_up], refs[2 * n_up], refs[2 * n_up + 1]
    rest = refs[2 * n_up + 2:]
    h_ref = rest[-1]

    def up_f32(h, w_refs, out_ref):
        rows = D_MODEL // k_parts
        acc = None
        for p, w_ref in enumerate(w_refs):
            w = w_ref[...].astype(BF16)
            if out_ref is not None:
                out_ref[p * rows:(p + 1) * rows, :] = w
            part = jnp.dot(h[:, p * rows:(p + 1) * rows], w, preferred_element_type=F32)
            acc = part if acc is None else acc + part
        return acc

    def partial_down():
        h = h_ref[...]
        if k_parts:
            wg_out_ref, wu_out_ref, wd_out_ref = rest[:3] if emit_bf16 else (None, None, None)
            wd = wd_ref[...].astype(BF16)
            if emit_bf16:
                wd_out_ref[...] = wd
            g = up_f32(h, wg_refs, wg_out_ref)
            u = up_f32(h, wu_refs, wu_out_ref)
            return jnp.dot((_silu(g) * u).astype(BF16), wd, preferred_element_type=F32)
        acts = []
        for t in range(wg_refs[0].shape[0]):
            g = jnp.dot(h, wg_refs[0][t], preferred_element_type=F32)
            u = jnp.dot(h, wu_refs[0][t], preferred_element_type=F32)
            acts.append((_silu(g) * u).astype(BF16))
        return jnp.dot(jnp.concatenate(acts, axis=1), wd_ref[...], preferred_element_type=F32)

    @pl.when(j == 0)
    def _():
        _pre_norm(st, i, x_ref, shift_ref, scale_ref, npre_ref[sub:sub + 1, :], h_ref)
        o_ref[...] = partial_down()

    last = pl.num_programs(1) - 1

    @pl.when((j > 0) & (j < last))
    def _():
        o_ref[...] += partial_down()

    @pl.when(j == last)
    def _():
        o_ref[...] += partial_down()
        _post_residual(st, i, x_ref, gate_ref, npost_ref[sub:sub + 1, :], o_ref, 0.5)


def _ffn(st, sub, x, mod, npre, npost, wg, wu, wd, emit_bf16=False):
    tiled = wg.ndim == 3
    n_sub = FF_SUBTILES_BF16 if tiled else 1
    ff_tile = n_sub * FF_SUBTILE
    row_spec = pl.BlockSpec((st.tile, D_MODEL), lambda i, j: (i, 0))
    full_spec = pl.BlockSpec((N_SUBLAYERS, D_MODEL), lambda i, j: (0, 0))
    tile_spec = pl.BlockSpec((None, D_MODEL, FF_SUBTILE), lambda i, j: (j, 0, 0))
    if tiled:
        k_parts = 0
        up_specs = [pl.BlockSpec((n_sub, D_MODEL, FF_SUBTILE), lambda i, j: (j, 0, 0))]
        up_args = lambda w: [w]
    else:
        k_parts = F32_WEIGHT_K_PARTS
        rows = D_MODEL // k_parts
        up_specs = [pl.BlockSpec((rows, ff_tile), functools.partial(lambda p, i, j: (p, j), p))
                    for p in range(k_parts)]
        up_args = lambda w: [w] * k_parts
    down_spec = pl.BlockSpec((ff_tile, D_MODEL), lambda i, j: (j, 0))
    out_shape = [jax.ShapeDtypeStruct((st.n_tokens, D_MODEL), F32)]
    out_specs = [row_spec]
    if emit_bf16:
        assert st.n_tiles == 1 and not tiled
        up_shape = jax.ShapeDtypeStruct((D_FF // FF_SUBTILE, D_MODEL, FF_SUBTILE), BF16)
        out_shape += [up_shape, up_shape, jax.ShapeDtypeStruct((D_FF, D_MODEL), BF16)]
        out_specs += [tile_spec, tile_spec, down_spec]
    outs = pl.pallas_call(
        functools.partial(_ffn_kernel, st, sub, k_parts, emit_bf16),
        out_shape=out_shape,
        grid=(st.n_tiles, D_FF // ff_tile),
        in_specs=[
            row_spec,
            _mod_spec(st, 3 * sub + 0), _mod_spec(st, 3 * sub + 1), _mod_spec(st, 3 * sub + 2),
            full_spec, full_spec,
            *up_specs, *up_specs, down_spec,
        ],
        out_specs=out_specs,
        scratch_shapes=[pltpu.VMEM((st.tile, D_MODEL), BF16)],
        compiler_params=pltpu.CompilerParams(
            dimension_semantics=("parallel", "arbitrary"), vmem_limit_bytes=V7X_VMEM_LIMIT_BYTES),
        name=f"ffn{sub}",
    )(x, mod, mod, mod, npre, npost, *up_args(wg), *up_args(wu), wd)
    return outs if emit_bf16 else outs[0]


_N_SIDE_IN = 9
_N_SIDE_OUT = 2


def _proj_kernel(st, k_parts, emit_bf16, side_t_new, x_ref, shift_ref, scale_ref, npre_ref, *refs):
    i = pl.program_id(0)
    j = pl.program_id(1)
    w_refs = refs[:k_parts]
    refs = refs[k_parts:]
    side_in = ()
    if side_t_new is not None:
        side_in, refs = refs[:_N_SIDE_IN], refs[_N_SIDE_IN:]
    qa_ref, kva_ref, qkr_ref, wide_ref = refs[:4]
    rest = refs[4:]
    h_ref = rest[-1]
    rows = D_MODEL // k_parts

    def project(dst_ref):
        h = h_ref[...]
        acc = None
        for p, w_ref in enumerate(w_refs):
            w = w_ref[...].astype(BF16)
            if emit_bf16:
                rest[0][p * rows:(p + 1) * rows, :] = w
            part = jnp.dot(h[:, p * rows:(p + 1) * rows], w, preferred_element_type=F32)
            acc = part if acc is None else acc + part
        dst_ref[...] = acc.astype(dst_ref.dtype)

    @pl.when(j == 0)
    def _():
        _pre_norm(st, i, x_ref, shift_ref, scale_ref, npre_ref[1:2, :], h_ref)
        project(qa_ref)

    @pl.when((j > 0) & (j < _QA_TILES))
    def _():
        project(qa_ref)

    @pl.when(j == _QA_TILES)
    def _():
        project(kva_ref)

    @pl.when((j >= _QKR_START) & (j < _WIDE_START))
    def _():
        project(qkr_ref)

    if side_t_new is None:
        @pl.when(j >= _WIDE_START)
        def _():
            project(wide_ref)
    else:
        side_out = rest[-1 - _N_SIDE_OUT:-1]

        wide_step = j - _WIDE_START
        is_side_step = (wide_step >= 0) & (wide_step % _SIDE_STRIDE == 0) & (wide_step < _SIDE_STRIDE * RET_HEADS)

        @pl.when(is_side_step)
        def _():
            project(wide_ref)
            log_g_ref, g_chunk_ref, *side_vmem = side_in
            _ret_sample_heads(side_t_new, wide_step // _SIDE_STRIDE, 1, log_g_ref, g_chunk_ref,
                              *side_vmem, *side_out)

        @pl.when((wide_step >= 0) & jnp.logical_not(is_side_step))
        def _():
            project(wide_ref)


_SIDE_STRIDE = _WIDE_TILES // RET_HEADS


def _side_head(j):
    return jnp.clip((j - _WIDE_START) // _SIDE_STRIDE, 0, RET_HEADS - 1)


def _proj(st, x, mod, npre, w_in, narrow_dtype, emit_bf16=False, side=None):
    n = st.n_tokens
    tiled = w_in.ndim == 3
    row_spec = pl.BlockSpec((st.tile, D_MODEL), lambda i, j: (i, 0))
    out_block = (st.tile, COL_TILE)
    tile_spec = pl.BlockSpec((None, D_MODEL, COL_TILE), lambda i, j: (j, 0, 0))
    out_shape = [
        jax.ShapeDtypeStruct((n, ATT_Q), narrow_dtype),
        jax.ShapeDtypeStruct((n, 2 * ATT_KV), F32),
        jax.ShapeDtypeStruct((n, 2 * RET_QK), F32),
        jax.ShapeDtypeStruct((n, _WIDE_COLS), narrow_dtype),
    ]
    out_specs = [
        pl.BlockSpec(out_block, lambda i, j: (i, jnp.minimum(j, _QA_TILES - 1))),
        pl.BlockSpec(out_block, lambda i, j: (i, 0)),
        pl.BlockSpec(out_block, lambda i, j: (i, jnp.clip(j - _QKR_START, 0, _QKR_TILES - 1))),
        pl.BlockSpec(out_block, lambda i, j: (i, jnp.maximum(j - _WIDE_START, 0))),
    ]
    if emit_bf16:
        assert st.n_tiles == 1 and not tiled
        out_shape.append(jax.ShapeDtypeStruct((D_IN // COL_TILE, D_MODEL, COL_TILE), BF16))
        out_specs.append(tile_spec)
    if tiled:
        k_parts = 1
        w_specs = [tile_spec]
    else:
        k_parts = F32_WEIGHT_K_PARTS
        w_specs = [pl.BlockSpec((D_MODEL // k_parts, COL_TILE), functools.partial(lambda p, i, j: (p, j), p))
                   for p in range(k_parts)]
    side_specs, side_args, side_t_new = [], [], None
    if side is not None:
        qkr_s, wide_s, cos2, sin2, state, side_t_new = side
        nb = RET_SAMPLE_SEQS
        n_rows = nb * side_t_new
        assert n_rows == RET_DK and state.shape[0] == st.n_tiles * nb and _WIDE_TILES >= RET_HEADS
        smem = pl.BlockSpec(memory_space=pltpu.SMEM)
        table_spec = pl.BlockSpec((n_rows, RET_DK), lambda i, j: (0, 0))
        state_spec = pl.BlockSpec((nb, 1, RET_DK, RET_DV), lambda i, j: (i, _side_head(j), 0, 0))
        side_specs = [
            smem, smem,
            pl.BlockSpec((n_rows, RET_DK), lambda i, j: (i, _side_head(j))),
            pl.BlockSpec((n_rows, RET_DK), lambda i, j: (i, RET_HEADS + _side_head(j))),
            pl.BlockSpec((n_rows, RET_DV), lambda i, j: (i, _side_head(j))),
            pl.BlockSpec((n_rows, RET_DV), lambda i, j: (i, RET_HEADS + _side_head(j))),
            table_spec, table_spec,
            state_spec,
        ]
        side_args = [
            jnp.asarray(_RET_LOG_G, F32),
            jnp.asarray([math.exp(side_t_new * g) for g in _RET_LOG_G], F32),
            qkr_s, qkr_s, wide_s, wide_s, jnp.tile(cos2, (nb, 1)), jnp.tile(sin2, (nb, 1)), state,
        ]
        assert len(side_specs) == _N_SIDE_IN
        out_shape += [jax.ShapeDtypeStruct((state.shape[0] * side_t_new, RET_V), F32),
                      jax.ShapeDtypeStruct(state.shape, F32)]
        out_specs += [pl.BlockSpec((n_rows, RET_DV), lambda i, j: (i, _side_head(j))), state_spec]
    outs = pl.pallas_call(
        functools.partial(_proj_kernel, st, k_parts, emit_bf16, side_t_new),
        out_shape=out_shape,
        grid=(st.n_tiles, D_IN // COL_TILE),
        in_specs=[
            row_spec,
            _mod_spec(st, 3), _mod_spec(st, 4),
            pl.BlockSpec((N_SUBLAYERS, D_MODEL), lambda i, j: (0, 0)),
            *w_specs,
            *side_specs,
        ],
        out_specs=out_specs,
        scratch_shapes=[pltpu.VMEM((st.tile, D_MODEL), BF16)],
        compiler_params=pltpu.CompilerParams(
            dimension_semantics=("parallel", "arbitrary"), vmem_limit_bytes=V7X_VMEM_LIMIT_BYTES),
        name="in_proj",
    )(x, mod, mod, npre, *([w_in] * k_parts), *side_args)
    return outs


def _attention(q, k2, v2, sinks_ref, first_valid_key):
    tq = q.shape[0]
    a_idx = lax.broadcasted_iota(jnp.int32, (tq, 2 * WINDOW), 0)
    b_idx = lax.broadcasted_iota(jnp.int32, (tq, 2 * WINDOW), 1)
    dist = WINDOW + a_idx - b_idx
    mask = (dist >= 0) & (dist <= WINDOW) & (b_idx >= first_valid_key)
    dist_f = jnp.where(mask, dist.astype(F32), jnp.inf)
    q = q * _ATT_SCALE
    outs = []
    for kv in range(ATT_KV_HEADS):
        cols = slice(kv * ATT_HEAD_DIM, (kv + 1) * ATT_HEAD_DIM)
        kk = k2[:, cols]
        vv = v2[:, cols]
        heads = range(kv * ATT_GROUP, (kv + 1) * ATT_GROUP)
        qg = jnp.concatenate(
            [q[:, h * ATT_HEAD_DIM:(h + 1) * ATT_HEAD_DIM] for h in heads], axis=0).astype(BF16)
        s_all = lax.dot_general(qg, kk, (((1,), (1,)), ((), ())), preferred_element_type=F32)
        probs = []
        for g, h in enumerate(heads):
            s = s_all[g * tq:(g + 1) * tq] - _ALIBI_SLOPES[h] * dist_f
            sink = sinks_ref[h]
            m = jnp.maximum(jnp.max(s, axis=-1, keepdims=True), sink)
            p = jnp.exp(s - m)
            inv = 1.0 / (jnp.sum(p, axis=-1, keepdims=True) + jnp.exp(sink - m))
            probs.append((p * inv).astype(BF16))
        o_all = jnp.dot(jnp.concatenate(probs, axis=0), vv, preferred_element_type=F32)
        outs.extend(o_all[g * tq:(g + 1) * tq] for g in range(ATT_GROUP))
    return jnp.concatenate(outs, axis=-1)


def _attn_prompt_kernel(sinks_ref, q_ref, kc_ref, vc_ref, kp_ref, vp_ref, o_ref):
    step = pl.program_id(1)
    k_tiles = [kp_ref[...]] + [kc_ref[r * WINDOW:(r + 1) * WINDOW, :] for r in range(ATTN_BLOCKS_PER_STEP)]
    v_tiles = [vp_ref[...]] + [vc_ref[r * WINDOW:(r + 1) * WINDOW, :] for r in range(ATTN_BLOCKS_PER_STEP)]
    for r in range(ATTN_BLOCKS_PER_STEP):
        rows = slice(r * WINDOW, (r + 1) * WINDOW)
        k2 = jnp.concatenate(k_tiles[r:r + 2], axis=0).astype(BF16)
        v2 = jnp.concatenate(v_tiles[r:r + 2], axis=0).astype(BF16)
        first_valid = jnp.where(step == 0, WINDOW, 0) if r == 0 else 0
        o_ref[rows, :] = _attention(q_ref[rows, :], k2, v2, sinks_ref, first_valid).astype(o_ref.dtype)


def _attn_prompt(qa, kva, sinks, batch, seq):
    per = ATTN_BLOCKS_PER_STEP
    ns = seq // (per * WINDOW)
    cur = lambda col: (lambda b, i: (b * ns + i, col))
    prev = lambda col: (lambda b, i: (b * ns * per + jnp.maximum(i * per - 1, 0), col))
    return pl.pallas_call(
        _attn_prompt_kernel,
        out_shape=jax.ShapeDtypeStruct((batch * seq, ATT_Q), BF16),
        grid=(batch, ns),
        in_specs=[
            pl.BlockSpec(memory_space=pltpu.SMEM),
            pl.BlockSpec((per * WINDOW, ATT_Q), lambda b, i: (b * ns + i, 0)),
            pl.BlockSpec((per * WINDOW, ATT_KV), cur(0)),
            pl.BlockSpec((per * WINDOW, ATT_KV), cur(1)),
            pl.BlockSpec((WINDOW, ATT_KV), prev(0)),
            pl.BlockSpec((WINDOW, ATT_KV), prev(1)),
        ],
        out_specs=pl.BlockSpec((per * WINDOW, ATT_Q), lambda b, i: (b * ns + i, 0)),
        compiler_params=pltpu.CompilerParams(
            dimension_semantics=("parallel", "arbitrary"), vmem_limit_bytes=V7X_VMEM_LIMIT_BYTES),
        name="attn_prompt",
    )(sinks, qa, kva, kva, kva, kva)


def _rotate(x, cos2, sin2):
    return x * cos2 + pltpu.roll(x, RET_DK // 2, axis=1) * sin2


def _ret_prompt_kernel(q_ref, k_ref, v_ref, gr_ref, cos_ref, sin_ref, o_ref, s_ref,
                       decay_ref, qw_ref, kw_ref):
    @pl.when(pl.program_id(1) == 0)
    def _():
        s_ref[...] = jnp.zeros_like(s_ref)
        row = lax.broadcasted_iota(jnp.int32, (RET_CHUNK, RET_CHUNK), 0).astype(F32)
        col = lax.broadcasted_iota(jnp.int32, (RET_CHUNK, RET_CHUNK), 1).astype(F32)
        diff = row - col
        for h in range(RET_HEADS):
            log_g = _RET_LOG_G[h]
            decay_ref[h] = jnp.where(diff >= 0, jnp.exp(jnp.maximum(diff, 0.0) * log_g), 0.0)
            qw_ref[h] = jnp.exp((row + 1.0) * log_g)
            kw_ref[h] = jnp.exp((RET_CHUNK - 1.0 - row) * log_g)

    for r in range(RET_CHUNKS_PER_STEP):
        rows = slice(r * RET_CHUNK, (r + 1) * RET_CHUNK)
        cos2, sin2 = cos_ref[rows, :], sin_ref[rows, :]
        for h in range(RET_HEADS):
            qk_cols = slice(h * RET_DK, (h + 1) * RET_DK)
            v_cols = slice(h * RET_DV, (h + 1) * RET_DV)
            qh = _rotate(q_ref[rows, qk_cols], cos2, sin2)
            kh = _rotate(k_ref[rows, qk_cols], cos2, sin2) * _RET_K_SCALE
            vb = v_ref[rows, v_cols].astype(BF16)
            s_prev = s_ref[0, h]
            scores = lax.dot_general(qh.astype(BF16), kh.astype(BF16), (((1,), (1,)), ((), ())),
                                     preferred_element_type=F32) * decay_ref[h]
            y = jnp.dot(scores.astype(BF16), vb, preferred_element_type=F32)
            y = y + jnp.dot((qh * qw_ref[h]).astype(BF16), s_prev.astype(BF16),
                            preferred_element_type=F32)
            kt = (kh * kw_ref[h]).T.astype(BF16)
            s_ref[0, h] = (math.exp(RET_CHUNK * _RET_LOG_G[h]) * s_prev
                           + jnp.dot(kt, vb, preferred_element_type=F32))
            gate = gr_ref[rows, v_cols].astype(F32)
            o_ref[rows, v_cols] = (_silu(gate) * _rms(y)).astype(o_ref.dtype)


def _ret_prompt(qkr, wide, cos2, sin2, batch, seq):
    step_rows = RET_CHUNKS_PER_STEP * RET_CHUNK
    nc = seq // step_rows
    rows = lambda col: (lambda b, c: (b * nc + c, col))
    return pl.pallas_call(
        _ret_prompt_kernel,
        out_shape=(
            jax.ShapeDtypeStruct((batch * seq, RET_V), BF16),
            jax.ShapeDtypeStruct((batch, RET_HEADS, RET_DK, RET_DV), F32),
        ),
        grid=(batch, nc),
        in_specs=[
            pl.BlockSpec((step_rows, RET_QK), rows(0)),
            pl.BlockSpec((step_rows, RET_QK), rows(1)),
            pl.BlockSpec((step_rows, RET_V), rows(0)),
            pl.BlockSpec((step_rows, RET_V), rows(1)),
            pl.BlockSpec((step_rows, RET_DK), lambda b, c: (c, 0)),
            pl.BlockSpec((step_rows, RET_DK), lambda b, c: (c, 0)),
        ],
        out_specs=(
            pl.BlockSpec((step_rows, RET_V), rows(0)),
            pl.BlockSpec((1, RET_HEADS, RET_DK, RET_DV), lambda b, c: (b, 0, 0, 0)),
        ),
        scratch_shapes=[pltpu.VMEM((RET_HEADS, RET_CHUNK, RET_CHUNK), F32)] * 3,
        compiler_params=pltpu.CompilerParams(
            dimension_semantics=("parallel", "arbitrary"), vmem_limit_bytes=V7X_VMEM_LIMIT_BYTES),
        name="ret_prompt",
    )(qkr, qkr, wide, wide, cos2, sin2)


def _attn_sample_kernel(t_new, sink_ref, qa_ref, kn_ref, vn_ref, ck_ref, cv_ref,
                        oa_ref, ko_ref, vo_ref):
    n_rows = ATT_HEADS * t_new
    pair = 2 * ATT_HEAD_DIM
    row = lax.broadcasted_iota(jnp.int32, (n_rows, 2 * WINDOW), 0)
    key = lax.broadcasted_iota(jnp.int32, (n_rows, 2 * WINDOW), 1)
    head = row // t_new
    dist = WINDOW + (row - head * t_new) - key
    slope = jnp.exp2((head.astype(F32) + 1.0) * (-8.0 / ATT_HEADS))
    bias = jnp.where((dist >= 0) & (dist <= WINDOW), -slope * dist.astype(F32), -jnp.inf)
    sink = sink_ref[...]
    lower = lax.broadcasted_iota(jnp.int32, (t_new, pair), 1) < ATT_HEAD_DIM
    zero_group = jnp.zeros((t_new, pair), F32)
    zero_keys = jnp.zeros((WINDOW - t_new, ATT_KV), F32)

    for n in range(SAMPLE_SEQS_PER_STEP):
        rows = slice(n * t_new, (n + 1) * t_new)
        kc, vc = ck_ref[n], cv_ref[n]
        kn, vn = kn_ref[rows, :], vn_ref[rows, :]
        ko_ref[n, :WINDOW - t_new, :] = kc[t_new:]
        ko_ref[n, WINDOW - t_new:, :] = kn
        vo_ref[n, :WINDOW - t_new, :] = vc[t_new:]
        vo_ref[n, WINDOW - t_new:, :] = vn
        k2 = jnp.concatenate([kc, kn, zero_keys], axis=0).astype(BF16)
        v2 = jnp.concatenate([vc, vn, zero_keys], axis=0).astype(BF16)

        q = qa_ref[rows, :] * _ATT_SCALE
        q_swapped = pltpu.roll(q, ATT_HEAD_DIM, axis=1)
        blocks = []
        for h in range(ATT_HEADS):
            kv = h // ATT_GROUP
            want_lower = kv % 2 == 0
            if (h % 2 == 0) == want_lower:
                src = q[:, (h // 2) * pair:(h // 2 + 1) * pair]
            else:
                g = (h + 1) // 2 % (ATT_HEADS // 2)
                src = q_swapped[:, g * pair:(g + 1) * pair]
            piece = jnp.where(lower if want_lower else ~lower, src, 0.0)
            blocks.append(jnp.concatenate(
                [piece, zero_group] if kv // 2 == 0 else [zero_group, piece], axis=1))
        q_bd = jnp.concatenate(blocks, axis=0).astype(BF16)

        s = lax.dot_general(q_bd, k2, (((1,), (1,)), ((), ())), preferred_element_type=F32) + bias
        m = jnp.maximum(jnp.max(s, axis=-1, keepdims=True), sink)
        p = jnp.exp(s - m)
        inv = 1.0 / (jnp.sum(p, axis=-1, keepdims=True) + jnp.exp(sink - m))
        o = jnp.dot((p * inv).astype(BF16), v2, preferred_element_type=F32)

        outs = []
        for g in range(ATT_HEADS // 2):
            h0 = 2 * g
            kv = h0 // ATT_GROUP
            cols = slice((kv // 2) * pair, (kv // 2 + 1) * pair)
            a = o[h0 * t_new:(h0 + 1) * t_new, cols]
            b = o[(h0 + 1) * t_new:(h0 + 2) * t_new, cols]
            if kv % 2 == 0:
                b = pltpu.roll(b, ATT_HEAD_DIM, axis=1)
            else:
                a = pltpu.roll(a, ATT_HEAD_DIM, axis=1)
            outs.append(jnp.where(lower, a, b))
        oa_ref[rows, :] = jnp.concatenate(outs, axis=1)


def _attn_sample(qa, kva, cache_k, cache_v, sink_rows, n_seq, t_new):
    nb = SAMPLE_SEQS_PER_STEP
    rows = lambda col: (lambda n: (n, col))
    cache_spec = pl.BlockSpec((nb, WINDOW, ATT_KV), lambda n: (n, 0, 0))
    return pl.pallas_call(
        functools.partial(_attn_sample_kernel, t_new),
        out_shape=(
            jax.ShapeDtypeStruct((n_seq * t_new, ATT_Q), F32),
            jax.ShapeDtypeStruct(cache_k.shape, F32),
            jax.ShapeDtypeStruct(cache_v.shape, F32),
        ),
        grid=(n_seq // nb,),
        in_specs=[
            pl.BlockSpec((ATT_HEADS * t_new, 1), lambda n: (0, 0)),
            pl.BlockSpec((nb * t_new, ATT_Q), rows(0)),
            pl.BlockSpec((nb * t_new, ATT_KV), rows(0)),
            pl.BlockSpec((nb * t_new, ATT_KV), rows(1)),
            cache_spec, cache_spec,
        ],
        out_specs=(pl.BlockSpec((nb * t_new, ATT_Q), rows(0)), cache_spec, cache_spec),
        compiler_params=pltpu.CompilerParams(
            dimension_semantics=("parallel",), vmem_limit_bytes=V7X_VMEM_LIMIT_BYTES),
        name="attn_sample",
    )(sink_rows, qa, kva, kva, cache_k, cache_v)


def _ret_sample_heads(t_new, first_head, n_heads, log_g_ref, g_chunk_ref, q_ref, k_ref, v_ref, gr_ref,
                      cos_ref, sin_ref, s_in_ref, o_ref, s_out_ref):
    n_seq = s_in_ref.shape[0]
    n_rows = n_seq * t_new
    cos2, sin2 = cos_ref[...], sin_ref[...]
    row = lax.broadcasted_iota(jnp.int32, (n_rows, n_rows), 0)
    col = lax.broadcasted_iota(jnp.int32, (n_rows, n_rows), 1)
    row_seq, col_seq = row // t_new, col // t_new
    same_chunk_causal = (row_seq == col_seq) & (row >= col)
    diff = jnp.maximum((row - col).astype(F32), 0.0)
    t = (row - row_seq * t_new).astype(F32)

    for hh in range(n_heads):
        h = first_head + hh
        log_g = log_g_ref[h]
        g_chunk = g_chunk_ref[h]
        qk_cols = slice(hh * RET_DK, (hh + 1) * RET_DK)
        v_cols = slice(hh * RET_DV, (hh + 1) * RET_DV)
        q = _rotate(q_ref[:, qk_cols], cos2, sin2)
        k = _rotate(k_ref[:, qk_cols], cos2, sin2) * _RET_K_SCALE
        vb = v_ref[:, v_cols].astype(BF16)

        decay = jnp.where(same_chunk_causal, jnp.exp(diff * log_g), 0.0)
        scores = lax.dot_general(q.astype(BF16), k.astype(BF16), (((1,), (1,)), ((), ())),
                                 preferred_element_type=F32) * decay
        y = jnp.dot(scores.astype(BF16), vb, preferred_element_type=F32)

        qw = q * jnp.exp((t + 1.0) * log_g)
        y_cross = [jnp.dot(qw[n * t_new:(n + 1) * t_new].astype(BF16), s_in_ref[n, hh].astype(BF16),
                           preferred_element_type=F32) for n in range(n_seq)]
        y = y + jnp.concatenate(y_cross, axis=0)

        kt = (k * jnp.exp((t_new - 1.0 - t) * log_g)).T
        lhs = jnp.concatenate([jnp.where(col_seq == n, kt, 0.0) for n in range(n_seq)], axis=0)
        kv = jnp.dot(lhs.astype(BF16), vb, preferred_element_type=F32)
        for n in range(n_seq):
            s_out_ref[n, hh] = g_chunk * s_in_ref[n, hh] + kv[n * RET_DK:(n + 1) * RET_DK]

        o_ref[:, v_cols] = _silu(gr_ref[:, v_cols]) * _rms(y)


def _merge_kernel(st, x_ref, gate_ref, npost_ref, oa_ref, or_ref, ga_ref, gr_ref,
                  wpa_ref, wpr_ref, wo_ref, o_ref, merged_ref):
    i = pl.program_id(0)
    oa = oa_ref[...].astype(BF16)
    orr = or_ref[...].astype(BF16)
    for c in range(D_MODEL // COL_TILE):
        cols = slice(c * COL_TILE, (c + 1) * COL_TILE)
        a = jnp.dot(oa, wpa_ref[:, cols], preferred_element_type=F32)
        r = jnp.dot(orr, wpr_ref[:, cols], preferred_element_type=F32)
        merged = (jax.nn.sigmoid(ga_ref[:, cols].astype(F32)) * a
                  + jax.nn.sigmoid(gr_ref[:, cols].astype(F32)) * r)
        merged_ref[:, cols] = merged.astype(BF16)
    o_ref[...] = jnp.dot(merged_ref[...], wo_ref[...], preferred_element_type=F32)
    _post_residual(st, i, x_ref, gate_ref, npost_ref[1:2, :], o_ref, 1.0)


def _merge(st, x, mod, npost, o_a, o_r, wide, w_pa, w_pr, w_o):
    gate_a_block = 2 * RET_V // D_MODEL
    row_spec = pl.BlockSpec((st.tile, D_MODEL), lambda i, j: (i, 0))
    resident = lambda shape: pl.BlockSpec(shape, lambda i, j: (0, 0), pipeline_mode=pl.Buffered(1))
    return pl.pallas_call(
        functools.partial(_merge_kernel, st),
        out_shape=jax.ShapeDtypeStruct((st.n_tokens, D_MODEL), F32),
        grid=(st.n_tiles, 1),
        in_specs=[
            row_spec,
            _mod_spec(st, 5),
            pl.BlockSpec((N_SUBLAYERS, D_MODEL), lambda i, j: (0, 0)),
            pl.BlockSpec((st.tile, ATT_Q), lambda i, j: (i, 0)),
            pl.BlockSpec((st.tile, RET_V), lambda i, j: (i, 0)),
            pl.BlockSpec((st.tile, D_MODEL), lambda i, j: (i, gate_a_block)),
            pl.BlockSpec((st.tile, D_MODEL), lambda i, j: (i, gate_a_block + 1)),
            resident((ATT_Q, D_MODEL)), resident((RET_V, D_MODEL)), resident((D_MODEL, D_MODEL)),
        ],
        out_specs=row_spec,
        scratch_shapes=[pltpu.VMEM((st.tile, D_MODEL), BF16)],
        compiler_params=pltpu.CompilerParams(
            dimension_semantics=("parallel", "arbitrary"), vmem_limit_bytes=V7X_VMEM_LIMIT_BYTES),
        name="merge_out",
    )(x, mod, npost, o_a, o_r, wide, wide, w_pa, w_pr, w_o)


def _rotation_tables(pos):
    half = RET_DK // 2
    inv_freq = ROPE_BASE ** (-jnp.linspace(0.0, 1.0, half, dtype=F32))
    ang = pos[:, None] * inv_freq[None, :]
    cos, sin = jnp.cos(ang), jnp.sin(ang)
    return jnp.concatenate([cos, cos], axis=-1), jnp.concatenate([-sin, sin], axis=-1)


def kernel(x_prompt, x_sample, cache_k_win, cache_v_win, state_ret, c_prompt, c_sample, w_ada, b_ada,
           norm_pre, norm_post, w_in, attn_sinks, w_pa, w_pr, w_o,
           ffn1_gate, ffn1_up, ffn1_down, ffn2_gate, ffn2_up, ffn2_down):
    batch, seq, _ = x_prompt.shape
    n_seq, t_new, _ = x_sample.shape
    assert w_ada.shape[0] == 1, "single-layer step"
    assert t_new == V7X_SUBLANES and seq % TOKEN_TILE == 0 and (n_seq * t_new) % TOKEN_TILE == 0
    assert seq % MERGE_TOKEN_TILE == 0 and (n_seq * t_new) % MERGE_TOKEN_TILE == 0
    assert batch <= MOD_PAD_ROWS and n_seq % MOD_PAD_ROWS == 0

    c_all = jnp.concatenate(
        [c_sample, c_prompt, jnp.zeros((MOD_PAD_ROWS - batch, D_MODEL), F32)], axis=0)
    mod = _ada(c_all, w_ada[0], b_ada[0])

    def prompt_stream(tile):
        return _Stream(batch * seq, tile, seq, MOD_PAD_ROWS, n_seq // MOD_PAD_ROWS)

    def sample_stream(tile):
        return _Stream(n_seq * t_new, tile, t_new, tile // t_new, 0)

    prompt, prompt_m = prompt_stream(TOKEN_TILE), prompt_stream(MERGE_TOKEN_TILE)
    sample, sample_m = sample_stream(TOKEN_TILE), sample_stream(MERGE_TOKEN_TILE // 2)

    npre, npost = norm_pre[0], norm_post[0]
    sinks = attn_sinks[0]
    bf = lambda w: w[0].astype(BF16)
    w_pa_b, w_pr_b, w_o_b = bf(w_pa), bf(w_pr), bf(w_o)

    xs = x_sample.reshape(n_seq * t_new, D_MODEL)
    xs, f1g, f1u, f1d = _ffn(sample, 0, xs, mod, npre, npost,
                             ffn1_gate[0], ffn1_up[0], ffn1_down[0], emit_bf16=True)
    qa_s, kva_s, qkr_s, wide_s, w_in_b = _proj(sample, xs, mod, npre, w_in[0], F32, emit_bf16=True)

    xp = x_prompt.reshape(batch * seq, D_MODEL)
    xp = _ffn(prompt, 0, xp, mod, npre, npost, f1g, f1u, f1d)
    cos_s, sin_s = _rotation_tables(jnp.arange(t_new, dtype=F32) + PAST_LEN)
    qa, kva, qkr, wide, o_r_s, state_s = _proj(
        prompt, xp, mod, npre, w_in_b, BF16, side=(qkr_s, wide_s, cos_s, sin_s, state_ret[0], t_new))
    o_a = _attn_prompt(qa, kva, sinks, batch, seq)
    cos_p, sin_p = _rotation_tables(jnp.arange(seq, dtype=F32))
    o_r, state_p = _ret_prompt(qkr, wide, cos_p, sin_p, batch, seq)
    xp = _merge(prompt_m, xp, mod, npost, o_a, o_r, wide, w_pa_b, w_pr_b, w_o_b)
    kva_p = kva.reshape(batch, seq, 2 * ATT_KV)[:, seq - WINDOW:]
    kv_shape = (1, batch, WINDOW, ATT_KV_HEADS, ATT_HEAD_DIM)
    k_win_p = kva_p[..., :ATT_KV].reshape(kv_shape)
    v_win_p = kva_p[..., ATT_KV:].reshape(kv_shape)

    sink_rows = jnp.repeat(sinks, t_new)[:, None]
    o_a_s, k_s, v_s = _attn_sample(
        qa_s, kva_s,
        cache_k_win[0].reshape(n_seq, WINDOW, ATT_KV), cache_v_win[0].reshape(n_seq, WINDOW, ATT_KV),
        sink_rows, n_seq, t_new)
    xs = _merge(sample_m, xs, mod, npost, o_a_s, o_r_s, wide_s, w_pa_b, w_pr_b, w_o_b)
    xs, f2g, f2u, f2d = _ffn(sample, 2, xs, mod, npre, npost,
                             ffn2_gate[0], ffn2_up[0], ffn2_down[0], emit_bf16=True)
    xp = _ffn(prompt, 2, xp, mod, npre, npost, f2g, f2u, f2d)
    kvs_shape = (1, n_seq, WINDOW, ATT_KV_HEADS, ATT_HEAD_DIM)

    return (xp.reshape(batch, seq, D_MODEL), xs.reshape(n_seq, t_new, D_MODEL),
            k_win_p, v_win_p, state_p[None],
            k_s.reshape(kvs_shape), v_s.reshape(kvs_shape), state_s[None])
```

```python
import functools
import math
from typing import NamedTuple

import jax
import jax.numpy as jnp
from jax import lax
from jax.experimental import pallas as pl
from jax.experimental.pallas import tpu as pltpu

F32 = jnp.float32
BF16 = jnp.bfloat16

D_MODEL = 2048
WINDOW = 128
ATT_HEADS = 16
ATT_KV_HEADS = 4
ATT_HEAD_DIM = 64
ATT_GROUP = ATT_HEADS // ATT_KV_HEADS
ATT_Q = ATT_HEADS * ATT_HEAD_DIM
ATT_KV = ATT_KV_HEADS * ATT_HEAD_DIM
RET_HEADS = 8
RET_DK = 128
RET_DV = 256
RET_CHUNK = 128
RET_QK = RET_HEADS * RET_DK
RET_V = RET_HEADS * RET_DV
ROPE_BASE = 10000.0
D_FF = 5632
NORM_EPS = 1e-6
N_SUBLAYERS = 3
PAST_LEN = 16384
D_IN = ATT_Q + 2 * ATT_KV + 2 * RET_QK + 2 * RET_V + 2 * D_MODEL

V7X_SUBLANES = 8
V7X_BF16_ROWS = 16
NORM_GROUP_ROWS = 128
V7X_VMEM_LIMIT_BYTES = 60 * 1024 * 1024

TOKEN_TILE = 1024
MERGE_TOKEN_TILE = 512
FF_SUBTILE = 256
FF_SUBTILES_BF16 = 2
F32_WEIGHT_K_PARTS = 4
COL_TILE = 512
ADA_COL_TILE = 1024
ATTN_BLOCKS_PER_STEP = 2
RET_CHUNKS_PER_STEP = 4
SAMPLE_SEQS_PER_STEP = 8
RET_SAMPLE_SEQS = 16
MOD_PAD_ROWS = 8

_QA_TILES = ATT_Q // COL_TILE
_KVA_TILES = 2 * ATT_KV // COL_TILE
_QKR_TILES = 2 * RET_QK // COL_TILE
_WIDE_COLS = 2 * RET_V + 2 * D_MODEL
_WIDE_TILES = _WIDE_COLS // COL_TILE
_QKR_START = _QA_TILES + _KVA_TILES
_WIDE_START = _QKR_START + _QKR_TILES
assert _WIDE_START + _WIDE_TILES == D_IN // COL_TILE and _KVA_TILES == 1

_ALIBI_SLOPES = [2.0 ** (-8.0 * (h + 1) / ATT_HEADS) for h in range(ATT_HEADS)]
_RET_LOG_G = [math.log(1.0 - 2.0 ** (-5.0 - h)) for h in range(RET_HEADS)]
_ATT_SCALE = ATT_HEAD_DIM ** -0.5
_RET_K_SCALE = RET_DK ** -0.5


class _Stream(NamedTuple):
    n_tokens: int
    tile: int
    rows_per_mod: int
    mod_block_rows: int
    mod_block_base: int

    @property
    def n_tiles(self):
        return self.n_tokens // self.tile

    @property
    def sub_rows(self):
        return min(self.rows_per_mod, self.tile)

    @property
    def group_rows(self):
        return max(self.sub_rows, NORM_GROUP_ROWS)

    @property
    def n_groups(self):
        return self.tile // self.group_rows

    @property
    def mods_per_group(self):
        return self.group_rows // self.sub_rows


def _rms(x):
    return x * lax.rsqrt(jnp.mean(x * x, axis=-1, keepdims=True) + NORM_EPS)


def _silu(x):
    return x * jax.nn.sigmoid(x)


def _for_groups(n_groups, fn):
    if n_groups == 1:
        fn(0)
    else:
        def body(g, carry):
            fn(g)
            return carry
        lax.fori_loop(0, n_groups, body, 0)


def _group_base(st, g):
    return 0 if st.n_groups == 1 else pl.multiple_of(g * st.group_rows, st.group_rows)


def _mod_row(st, i, g, s):
    if st.rows_per_mod >= st.tile:
        return (i * st.tile) // st.rows_per_mod
    return g * st.mods_per_group + s


def _pre_norm(st, i, x_ref, shift_ref, scale_ref, gain, h_ref):
    def group(g):
        base = _group_base(st, g)
        parts = []
        for s in range(st.mods_per_group):
            rows = pl.ds(base + s * st.sub_rows, st.sub_rows)
            m = _mod_row(st, i, g, s)
            x = x_ref[rows, :]
            row_gain = gain * (1.0 + scale_ref[pl.ds(m, 1), :])
            sh = shift_ref[pl.ds(m, 1), :]
            parts.append(_rms(x) * row_gain + sh)
        h = parts[0] if len(parts) == 1 else jnp.concatenate(parts, axis=0)
        h_ref[pl.ds(base, st.group_rows), :] = h.astype(BF16)
    _for_groups(st.n_groups, group)


def _post_residual(st, i, x_ref, gate_ref, gain, o_ref, coeff):
    def group(g):
        base = _group_base(st, g)
        results = []
        for s in range(st.mods_per_group):
            rows = pl.ds(base + s * st.sub_rows, st.sub_rows)
            m = _mod_row(st, i, g, s)
            row_gain = gate_ref[pl.ds(m, 1), :] * gain
            if coeff != 1.0:
                row_gain = coeff * row_gain
            results.append((rows, x_ref[rows, :] + _rms(o_ref[rows, :]) * row_gain))
        for rows, value in results:
            o_ref[rows, :] = value
    _for_groups(st.n_groups, group)


def _ada_kernel(c_ref, w_ref, b_ref, o_ref):
    a = _silu(c_ref[...]).astype(BF16)
    o_ref[0] = jnp.dot(a, w_ref[...].astype(BF16), preferred_element_type=F32) + b_ref[...]


def _ada(c_all, w_ada, b_ada):
    rows = c_all.shape[0]
    n_vec = N_SUBLAYERS * 3
    per_vec = D_MODEL // ADA_COL_TILE
    return pl.pallas_call(
        _ada_kernel,
        out_shape=jax.ShapeDtypeStruct((n_vec, rows, D_MODEL), F32),
        grid=(n_vec * per_vec,),
        in_specs=[
            pl.BlockSpec((rows, D_MODEL), lambda j: (0, 0)),
            pl.BlockSpec((D_MODEL, ADA_COL_TILE), lambda j: (0, j)),
            pl.BlockSpec((1, ADA_COL_TILE), lambda j: (0, j)),
        ],
        out_specs=pl.BlockSpec((1, rows, ADA_COL_TILE), lambda j: (j // per_vec, 0, j % per_vec)),
        compiler_params=pltpu.CompilerParams(
            dimension_semantics=("arbitrary",), vmem_limit_bytes=V7X_VMEM_LIMIT_BYTES),
        name="ada_mod",
    )(c_all, w_ada, b_ada.reshape(1, -1))


def _mod_spec(st, vec):
    if st.rows_per_mod >= st.tile:
        index = lambda i, j: (vec, st.mod_block_base, 0)
    else:
        index = lambda i, j: (vec, st.mod_block_base + i, 0)
    return pl.BlockSpec((None, st.mod_block_rows, D_MODEL), index)


def _ffn_kernel(st, sub, k_parts, emit_bf16, x_ref, shift_ref, scale_ref, gate_ref, npre_ref, npost_ref,
                *refs):
    i = pl.program_id(0)
    j = pl.program_id(1)
    n_up = max(k_parts, 1)
    wg_refs, wu_refs, wd_ref, o_ref = refs[:n_up], refs[n_up:2 * n_up], refs[2 * n_up], refs[2 * n_up + 1]
    rest = refs[2 * n_up + 2:]
    h_ref = rest[-1]

    def up_f32(h, w_refs, out_ref):
        rows = D_MODEL // k_parts
        acc = None
        for p, w_ref in enumerate(w_refs):
            w = w_ref[...].astype(BF16)
            if out_ref is not None:
                out_ref[p * rows:(p + 1) * rows, :] = w
            part = jnp.dot(h[:, p * rows:(p + 1) * rows], w, preferred_element_type=F32)
            acc = part if acc is None else acc + part
        return acc

    def partial_down():
        h = h_ref[...]
        if k_parts:
            wg_out_ref, wu_out_ref, wd_out_ref = rest[:3] if emit_bf16 else (None, None, None)
            wd = wd_ref[...].astype(BF16)
            if emit_bf16:
                wd_out_ref[...] = wd
            g = up_f32(h, wg_refs, wg_out_ref)
            u = up_f32(h, wu_refs, wu_out_ref)
            return jnp.dot((_silu(g) * u).astype(BF16), wd, preferred_element_type=F32)
        acts = []
        for t in range(wg_refs[0].shape[0]):
            g = jnp.dot(h, wg_refs[0][t], preferred_element_type=F32)
            u = jnp.dot(h, wu_refs[0][t], preferred_element_type=F32)
            acts.append((_silu(g) * u).astype(BF16))
        return jnp.dot(jnp.concatenate(acts, axis=1), wd_ref[...], preferred_element_type=F32)

    @pl.when(j == 0)
    def _():
        _pre_norm(st, i, x_ref, shift_ref, scale_ref, npre_ref[sub:sub + 1, :], h_ref)
        o_ref[...] = partial_down()

    last = pl.num_programs(1) - 1

    @pl.when((j > 0) & (j < last))
    def _():
        o_ref[...] += partial_down()

    @pl.when(j == last)
    def _():
        o_ref[...] += partial_down()
        _post_residual(st, i, x_ref, gate_ref, npost_ref[sub:sub + 1, :], o_ref, 0.5)


def _ffn(st, sub, x, mod, npre, npost, wg, wu, wd, emit_bf16=False):
    tiled = wg.ndim == 3
    n_sub = FF_SUBTILES_BF16 if tiled else 1
    ff_tile = n_sub * FF_SUBTILE
    row_spec = pl.BlockSpec((st.tile, D_MODEL), lambda i, j: (i, 0))
    full_spec = pl.BlockSpec((N_SUBLAYERS, D_MODEL), lambda i, j: (0, 0))
    tile_spec = pl.BlockSpec((None, D_MODEL, FF_SUBTILE), lambda i, j: (j, 0, 0))
    if tiled:
        k_parts = 0
        up_specs = [pl.BlockSpec((n_sub, D_MODEL, FF_SUBTILE), lambda i, j: (j, 0, 0))]
        up_args = lambda w: [w]
    else:
        k_parts = F32_WEIGHT_K_PARTS
        rows = D_MODEL // k_parts
        up_specs = [pl.BlockSpec((rows, ff_tile), functools.partial(lambda p, i, j: (p, j), p))
                    for p in range(k_parts)]
        up_args = lambda w: [w] * k_parts
    down_spec = pl.BlockSpec((ff_tile, D_MODEL), lambda i, j: (j, 0))
    out_shape = [jax.ShapeDtypeStruct((st.n_tokens, D_MODEL), F32)]
    out_specs = [row_spec]
    if emit_bf16:
        assert st.n_tiles == 1 and not tiled
        up_shape = jax.ShapeDtypeStruct((D_FF // FF_SUBTILE, D_MODEL, FF_SUBTILE), BF16)
        out_shape += [up_shape, up_shape, jax.ShapeDtypeStruct((D_FF, D_MODEL), BF16)]
        out_specs += [tile_spec, tile_spec, down_spec]
    outs = pl.pallas_call(
        functools.partial(_ffn_kernel, st, sub, k_parts, emit_bf16),
        out_shape=out_shape,
        grid=(st.n_tiles, D_FF // ff_tile),
        in_specs=[
            row_spec,
            _mod_spec(st, 3 * sub + 0), _mod_spec(st, 3 * sub + 1), _mod_spec(st, 3 * sub + 2),
            full_spec, full_spec,
            *up_specs, *up_specs, down_spec,
        ],
        out_specs=out_specs,
        scratch_shapes=[pltpu.VMEM((st.tile, D_MODEL), BF16)],
        compiler_params=pltpu.CompilerParams(
            dimension_semantics=("parallel", "arbitrary"), vmem_limit_bytes=V7X_VMEM_LIMIT_BYTES),
        name=f"ffn{sub}",
    )(x, mod, mod, mod, npre, npost, *up_args(wg), *up_args(wu), wd)
    return outs if emit_bf16 else outs[0]


_N_SIDE_IN = 9
_N_SIDE_OUT = 2


def _proj_kernel(st, k_parts, emit_bf16, side_t_new, x_ref, shift_ref, scale_ref, npre_ref, *refs):
    i = pl.program_id(0)
    j = pl.program_id(1)
    w_refs = refs[:k_parts]
    refs = refs[k_parts:]
    side_in = ()
    if side_t_new is not None:
        side_in, refs = refs[:_N_SIDE_IN], refs[_N_SIDE_IN:]
    qa_ref, kva_ref, qkr_ref, wide_ref = refs[:4]
    rest = refs[4:]
    h_ref = rest[-1]
    rows = D_MODEL // k_parts

    def project(dst_ref):
        h = h_ref[...]
        acc = None
        for p, w_ref in enumerate(w_refs):
            w = w_ref[...].astype(BF16)
            if emit_bf16:
                rest[0][p * rows:(p + 1) * rows, :] = w
            part = jnp.dot(h[:, p * rows:(p + 1) * rows], w, preferred_element_type=F32)
            acc = part if acc is None else acc + part
        dst_ref[...] = acc.astype(dst_ref.dtype)

    @pl.when(j == 0)
    def _():
        _pre_norm(st, i, x_ref, shift_ref, scale_ref, npre_ref[1:2, :], h_ref)
        project(qa_ref)

    @pl.when((j > 0) & (j < _QA_TILES))
    def _():
        project(qa_ref)

    @pl.when(j == _QA_TILES)
    def _():
        project(kva_ref)

    @pl.when((j >= _QKR_START) & (j < _WIDE_START))
    def _():
        project(qkr_ref)

    if side_t_new is None:
        @pl.when(j >= _WIDE_START)
        def _():
            project(wide_ref)
    else:
        side_out = rest[-1 - _N_SIDE_OUT:-1]

        wide_step = j - _WIDE_START
        is_side_step = (wide_step >= 0) & (wide_step % _SIDE_STRIDE == 0) & (wide_step < _SIDE_STRIDE * RET_HEADS)

        @pl.when(is_side_step)
        def _():
            project(wide_ref)
            log_g_ref, g_chunk_ref, *side_vmem = side_in
            _ret_sample_heads(side_t_new, wide_step // _SIDE_STRIDE, 1, log_g_ref, g_chunk_ref,
                              *side_vmem, *side_out)

        @pl.when((wide_step >= 0) & jnp.logical_not(is_side_step))
        def _():
            project(wide_ref)


_SIDE_STRIDE = _WIDE_TILES // RET_HEADS


def _side_head(j):
    return jnp.clip((j - _WIDE_START) // _SIDE_STRIDE, 0, RET_HEADS - 1)


def _proj(st, x, mod, npre, w_in, narrow_dtype, emit_bf16=False, side=None):
    n = st.n_tokens
    tiled = w_in.ndim == 3
    row_spec = pl.BlockSpec((st.tile, D_MODEL), lambda i, j: (i, 0))
    out_block = (st.tile, COL_TILE)
    tile_spec = pl.BlockSpec((None, D_MODEL, COL_TILE), lambda i, j: (j, 0, 0))
    out_shape = [
        jax.ShapeDtypeStruct((n, ATT_Q), narrow_dtype),
        jax.ShapeDtypeStruct((n, 2 * ATT_KV), F32),
        jax.ShapeDtypeStruct((n, 2 * RET_QK), F32),
        jax.ShapeDtypeStruct((n, _WIDE_COLS), narrow_dtype),
    ]
    out_specs = [
        pl.BlockSpec(out_block, lambda i, j: (i, jnp.minimum(j, _QA_TILES - 1))),
        pl.BlockSpec(out_block, lambda i, j: (i, 0)),
        pl.BlockSpec(out_block, lambda i, j: (i, jnp.clip(j - _QKR_START, 0, _QKR_TILES - 1))),
        pl.BlockSpec(out_block, lambda i, j: (i, jnp.maximum(j - _WIDE_START, 0))),
    ]
    if emit_bf16:
        assert st.n_tiles == 1 and not tiled
        out_shape.append(jax.ShapeDtypeStruct((D_IN // COL_TILE, D_MODEL, COL_TILE), BF16))
        out_specs.append(tile_spec)
    if tiled:
        k_parts = 1
        w_specs = [tile_spec]
    else:
        k_parts = F32_WEIGHT_K_PARTS
        w_specs = [pl.BlockSpec((D_MODEL // k_parts, COL_TILE), functools.partial(lambda p, i, j: (p, j), p))
                   for p in range(k_parts)]
    side_specs, side_args, side_t_new = [], [], None
    if side is not None:
        qkr_s, wide_s, cos2, sin2, state, side_t_new = side
        nb = RET_SAMPLE_SEQS
        n_rows = nb * side_t_new
        assert n_rows == RET_DK and state.shape[0] == st.n_tiles * nb and _WIDE_TILES >= RET_HEADS
        smem = pl.BlockSpec(memory_space=pltpu.SMEM)
        table_spec = pl.BlockSpec((n_rows, RET_DK), lambda i, j: (0, 0))
        state_spec = pl.BlockSpec((nb, 1, RET_DK, RET_DV), lambda i, j: (i, _side_head(j), 0, 0))
        side_specs = [
            smem, smem,
            pl.BlockSpec((n_rows, RET_DK), lambda i, j: (i, _side_head(j))),
            pl.BlockSpec((n_rows, RET_DK), lambda i, j: (i, RET_HEADS + _side_head(j))),
            pl.BlockSpec((n_rows, RET_DV), lambda i, j: (i, _side_head(j))),
            pl.BlockSpec((n_rows, RET_DV), lambda i, j: (i, RET_HEADS + _side_head(j))),
            table_spec, table_spec,
            state_spec,
        ]
        side_args = [
            jnp.asarray(_RET_LOG_G, F32),
            jnp.asarray([math.exp(side_t_new * g) for g in _RET_LOG_G], F32),
            qkr_s, qkr_s, wide_s, wide_s, jnp.tile(cos2, (nb, 1)), jnp.tile(sin2, (nb, 1)), state,
        ]
        assert len(side_specs) == _N_SIDE_IN
        out_shape += [jax.ShapeDtypeStruct((state.shape[0] * side_t_new, RET_V), F32),
                      jax.ShapeDtypeStruct(state.shape, F32)]
        out_specs += [pl.BlockSpec((n_rows, RET_DV), lambda i, j: (i, _side_head(j))), state_spec]
    outs = pl.pallas_call(
        functools.partial(_proj_kernel, st, k_parts, emit_bf16, side_t_new),
        out_shape=out_shape,
        grid=(st.n_tiles, D_IN // COL_TILE),
        in_specs=[
            row_spec,
            _mod_spec(st, 3), _mod_spec(st, 4),
            pl.BlockSpec((N_SUBLAYERS, D_MODEL), lambda i, j: (0, 0)),
            *w_specs,
            *side_specs,
        ],
        out_specs=out_specs,
        scratch_shapes=[pltpu.VMEM((st.tile, D_MODEL), BF16)],
        compiler_params=pltpu.CompilerParams(
            dimension_semantics=("parallel", "arbitrary"), vmem_limit_bytes=V7X_VMEM_LIMIT_BYTES),
        name="in_proj",
    )(x, mod, mod, npre, *([w_in] * k_parts), *side_args)
    return outs


def _attention(q, k2, v2, sinks_ref, first_valid_key):
    tq = q.shape[0]
    a_idx = lax.broadcasted_iota(jnp.int32, (tq, 2 * WINDOW), 0)
    b_idx = lax.broadcasted_iota(jnp.int32, (tq, 2 * WINDOW), 1)
    dist = WINDOW + a_idx - b_idx
    mask = (dist >= 0) & (dist <= WINDOW) & (b_idx >= first_valid_key)
    dist_f = jnp.where(mask, dist.astype(F32), jnp.inf)
    q = q * _ATT_SCALE
    outs = []
    for kv in range(ATT_KV_HEADS):
        cols = slice(kv * ATT_HEAD_DIM, (kv + 1) * ATT_HEAD_DIM)
        kk = k2[:, cols]
        vv = v2[:, cols]
        heads = range(kv * ATT_GROUP, (kv + 1) * ATT_GROUP)
        qg = jnp.concatenate(
            [q[:, h * ATT_HEAD_DIM:(h + 1) * ATT_HEAD_DIM] for h in heads], axis=0).astype(BF16)
        s_all = lax.dot_general(qg, kk, (((1,), (1,)), ((), ())), preferred_element_type=F32)
        probs = []
        for g, h in enumerate(heads):
            s = s_all[g * tq:(g + 1) * tq] - _ALIBI_SLOPES[h] * dist_f
            sink = sinks_ref[h]
            m = jnp.maximum(jnp.max(s, axis=-1, keepdims=True), sink)
            p = jnp.exp(s - m)
            inv = 1.0 / (jnp.sum(p, axis=-1, keepdims=True) + jnp.exp(sink - m))
            probs.append((p * inv).astype(BF16))
        o_all = jnp.dot(jnp.concatenate(probs, axis=0), vv, preferred_element_type=F32)
        outs.extend(o_all[g * tq:(g + 1) * tq] for g in range(ATT_GROUP))
    return jnp.concatenate(outs, axis=-1)


def _attn_prompt_kernel(sinks_ref, q_ref, kc_ref, vc_ref, kp_ref, vp_ref, o_ref):
    step = pl.program_id(1)
    k_tiles = [kp_ref[...]] + [kc_ref[r * WINDOW:(r + 1) * WINDOW, :] for r in range(ATTN_BLOCKS_PER_STEP)]
    v_tiles = [vp_ref[...]] + [vc_ref[r * WINDOW:(r + 1) * WINDOW, :] for r in range(ATTN_BLOCKS_PER_STEP)]
    for r in range(ATTN_BLOCKS_PER_STEP):
        rows = slice(r * WINDOW, (r + 1) * WINDOW)
        k2 = jnp.concatenate(k_tiles[r:r + 2], axis=0).astype(BF16)
        v2 = jnp.concatenate(v_tiles[r:r + 2], axis=0).astype(BF16)
        first_valid = jnp.where(step == 0, WINDOW, 0) if r == 0 else 0
        o_ref[rows, :] = _attention(q_ref[rows, :], k2, v2, sinks_ref, first_valid).astype(o_ref.dtype)


def _attn_prompt(qa, kva, sinks, batch, seq):
    per = ATTN_BLOCKS_PER_STEP
    ns = seq // (per * WINDOW)
    cur = lambda col: (lambda b, i: (b * ns + i, col))
    prev = lambda col: (lambda b, i: (b * ns * per + jnp.maximum(i * per - 1, 0), col))
    return pl.pallas_call(
        _attn_prompt_kernel,
        out_shape=jax.ShapeDtypeStruct((batch * seq, ATT_Q), BF16),
        grid=(batch, ns),
        in_specs=[
            pl.BlockSpec(memory_space=pltpu.SMEM),
            pl.BlockSpec((per * WINDOW, ATT_Q), lambda b, i: (b * ns + i, 0)),
            pl.BlockSpec((per * WINDOW, ATT_KV), cur(0)),
            pl.BlockSpec((per * WINDOW, ATT_KV), cur(1)),
            pl.BlockSpec((WINDOW, ATT_KV), prev(0)),
            pl.BlockSpec((WINDOW, ATT_KV), prev(1)),
        ],
        out_specs=pl.BlockSpec((per * WINDOW, ATT_Q), lambda b, i: (b * ns + i, 0)),
        compiler_params=pltpu.CompilerParams(
            dimension_semantics=("parallel", "arbitrary"), vmem_limit_bytes=V7X_VMEM_LIMIT_BYTES),
        name="attn_prompt",
    )(sinks, qa, kva, kva, kva, kva)


def _rotate(x, cos2, sin2):
    return x * cos2 + pltpu.roll(x, RET_DK // 2, axis=1) * sin2


def _ret_prompt_kernel(q_ref, k_ref, v_ref, gr_ref, cos_ref, sin_ref, o_ref, s_ref,
                       decay_ref, qw_ref, kw_ref):
    @pl.when(pl.program_id(1) == 0)
    def _():
        s_ref[...] = jnp.zeros_like(s_ref)
        row = lax.broadcasted_iota(jnp.int32, (RET_CHUNK, RET_CHUNK), 0).astype(F32)
        col = lax.broadcasted_iota(jnp.int32, (RET_CHUNK, RET_CHUNK), 1).astype(F32)
        diff = row - col
        for h in range(RET_HEADS):
            log_g = _RET_LOG_G[h]
            decay_ref[h] = jnp.where(diff >= 0, jnp.exp(jnp.maximum(diff, 0.0) * log_g), 0.0)
            qw_ref[h] = jnp.exp((row + 1.0) * log_g)
            kw_ref[h] = jnp.exp((RET_CHUNK - 1.0 - row) * log_g)

    for r in range(RET_CHUNKS_PER_STEP):
        rows = slice(r * RET_CHUNK, (r + 1) * RET_CHUNK)
        cos2, sin2 = cos_ref[rows, :], sin_ref[rows, :]
        for h in range(RET_HEADS):
            qk_cols = slice(h * RET_DK, (h + 1) * RET_DK)
            v_cols = slice(h * RET_DV, (h + 1) * RET_DV)
            qh = _rotate(q_ref[rows, qk_cols], cos2, sin2)
            kh = _rotate(k_ref[rows, qk_cols], cos2, sin2) * _RET_K_SCALE
            vb = v_ref[rows, v_cols].astype(BF16)
            s_prev = s_ref[0, h]
            scores = lax.dot_general(qh.astype(BF16), kh.astype(BF16), (((1,), (1,)), ((), ())),
                                     preferred_element_type=F32) * decay_ref[h]
            y = jnp.dot(scores.astype(BF16), vb, preferred_element_type=F32)
            y = y + jnp.dot((qh * qw_ref[h]).astype(BF16), s_prev.astype(BF16),
                            preferred_element_type=F32)
            kt = (kh * kw_ref[h]).T.astype(BF16)
            s_ref[0, h] = (math.exp(RET_CHUNK * _RET_LOG_G[h]) * s_prev
                           + jnp.dot(kt, vb, preferred_element_type=F32))
            gate = gr_ref[rows, v_cols].astype(F32)
            o_ref[rows, v_cols] = (_silu(gate) * _rms(y)).astype(o_ref.dtype)


def _ret_prompt(qkr, wide, cos2, sin2, batch, seq):
    step_rows = RET_CHUNKS_PER_STEP * RET_CHUNK
    nc = seq // step_rows
    rows = lambda col: (lambda b, c: (b * nc + c, col))
    return pl.pallas_call(
        _ret_prompt_kernel,
        out_shape=(
            jax.ShapeDtypeStruct((batch * seq, RET_V), BF16),
            jax.ShapeDtypeStruct((batch, RET_HEADS, RET_DK, RET_DV), F32),
        ),
        grid=(batch, nc),
        in_specs=[
            pl.BlockSpec((step_rows, RET_QK), rows(0)),
            pl.BlockSpec((step_rows, RET_QK), rows(1)),
            pl.BlockSpec((step_rows, RET_V), rows(0)),
            pl.BlockSpec((step_rows, RET_V), rows(1)),
            pl.BlockSpec((step_rows, RET_DK), lambda b, c: (c, 0)),
            pl.BlockSpec((step_rows, RET_DK), lambda b, c: (c, 0)),
        ],
        out_specs=(
            pl.BlockSpec((step_rows, RET_V), rows(0)),
            pl.BlockSpec((1, RET_HEADS, RET_DK, RET_DV), lambda b, c: (b, 0, 0, 0)),
        ),
        scratch_shapes=[pltpu.VMEM((RET_HEADS, RET_CHUNK, RET_CHUNK), F32)] * 3,
        compiler_params=pltpu.CompilerParams(
            dimension_semantics=("parallel", "arbitrary"), vmem_limit_bytes=V7X_VMEM_LIMIT_BYTES),
        name="ret_prompt",
    )(qkr, qkr, wide, wide, cos2, sin2)


def _attn_sample_kernel(t_new, sink_ref, qa_ref, kn_ref, vn_ref, ck_ref, cv_ref,
                        oa_ref, ko_ref, vo_ref):
    n_rows = ATT_HEADS * t_new
    pair = 2 * ATT_HEAD_DIM
    row = lax.broadcasted_iota(jnp.int32, (n_rows, 2 * WINDOW), 0)
    key = lax.broadcasted_iota(jnp.int32, (n_rows, 2 * WINDOW), 1)
    head = row // t_new
    dist = WINDOW + (row - head * t_new) - key
    slope = jnp.exp2((head.astype(F32) + 1.0) * (-8.0 / ATT_HEADS))
    bias = jnp.where((dist >= 0) & (dist <= WINDOW), -slope * dist.astype(F32), -jnp.inf)
    sink = sink_ref[...]
    lower = lax.broadcasted_iota(jnp.int32, (t_new, pair), 1) < ATT_HEAD_DIM
    zero_group = jnp.zeros((t_new, pair), F32)
    zero_keys = jnp.zeros((WINDOW - t_new, ATT_KV), F32)

    for n in range(SAMPLE_SEQS_PER_STEP):
        rows = slice(n * t_new, (n + 1) * t_new)
        kc, vc = ck_ref[n], cv_ref[n]
        kn, vn = kn_ref[rows, :], vn_ref[rows, :]
        ko_ref[n, :WINDOW - t_new, :] = kc[t_new:]
        ko_ref[n, WINDOW - t_new:, :] = kn
        vo_ref[n, :WINDOW - t_new, :] = vc[t_new:]
        vo_ref[n, WINDOW - t_new:, :] = vn
        k2 = jnp.concatenate([kc, kn, zero_keys], axis=0).astype(BF16)
        v2 = jnp.concatenate([vc, vn, zero_keys], axis=0).astype(BF16)

        q = qa_ref[rows, :] * _ATT_SCALE
        q_swapped = pltpu.roll(q, ATT_HEAD_DIM, axis=1)
        blocks = []
        for h in range(ATT_HEADS):
            kv = h // ATT_GROUP
            want_lower = kv % 2 == 0
            if (h % 2 == 0) == want_lower:
                src = q[:, (h // 2) * pair:(h // 2 + 1) * pair]
            else:
                g = (h + 1) // 2 % (ATT_HEADS // 2)
                src = q_swapped[:, g * pair:(g + 1) * pair]
            piece = jnp.where(lower if want_lower else ~lower, src, 0.0)
            blocks.append(jnp.concatenate(
                [piece, zero_group] if kv // 2 == 0 else [zero_group, piece], axis=1))
        q_bd = jnp.concatenate(blocks, axis=0).astype(BF16)

        s = lax.dot_general(q_bd, k2, (((1,), (1,)), ((), ())), preferred_element_type=F32) + bias
        m = jnp.maximum(jnp.max(s, axis=-1, keepdims=True), sink)
        p = jnp.exp(s - m)
        inv = 1.0 / (jnp.sum(p, axis=-1, keepdims=True) + jnp.exp(sink - m))
        o = jnp.dot((p * inv).astype(BF16), v2, preferred_element_type=F32)

        outs = []
        for g in range(ATT_HEADS // 2):
            h0 = 2 * g
            kv = h0 // ATT_GROUP
            cols = slice((kv // 2) * pair, (kv // 2 + 1) * pair)
            a = o[h0 * t_new:(h0 + 1) * t_new, cols]
            b = o[(h0 + 1) * t_new:(h0 + 2) * t_new, cols]
            if kv % 2 == 0:
                b = pltpu.roll(b, ATT_HEAD_DIM, axis=1)
            else:
                a = pltpu.roll(a, ATT_HEAD_DIM, axis=1)
            outs.append(jnp.where(lower, a, b))
        oa_ref[rows, :] = jnp.concatenate(outs, axis=1)


def _attn_sample(qa, kva, cache_k, cache_v, sink_rows, n_seq, t_new):
    nb = SAMPLE_SEQS_PER_STEP
    rows = lambda col: (lambda n: (n, col))
    cache_spec = pl.BlockSpec((nb, WINDOW, ATT_KV), lambda n: (n, 0, 0))
    return pl.pallas_call(
        functools.partial(_attn_sample_kernel, t_new),
        out_shape=(
            jax.ShapeDtypeStruct((n_seq * t_new, ATT_Q), F32),
            jax.ShapeDtypeStruct(cache_k.shape, F32),
            jax.ShapeDtypeStruct(cache_v.shape, F32),
        ),
        grid=(n_seq // nb,),
        in_specs=[
            pl.BlockSpec((ATT_HEADS * t_new, 1), lambda n: (0, 0)),
            pl.BlockSpec((nb * t_new, ATT_Q), rows(0)),
            pl.BlockSpec((nb * t_new, ATT_KV), rows(0)),
            pl.BlockSpec((nb * t_new, ATT_KV), rows(1)),
            cache_spec, cache_spec,
        ],
        out_specs=(pl.BlockSpec((nb * t_new, ATT_Q), rows(0)), cache_spec, cache_spec),
        compiler_params=pltpu.CompilerParams(
            dimension_semantics=("parallel",), vmem_limit_bytes=V7X_VMEM_LIMIT_BYTES),
        name="attn_sample",
    )(sink_rows, qa, kva, kva, cache_k, cache_v)


def _ret_sample_heads(t_new, first_head, n_heads, log_g_ref, g_chunk_ref, q_ref, k_ref, v_ref, gr_ref,
                      cos_ref, sin_ref, s_in_ref, o_ref, s_out_ref):
    n_seq = s_in_ref.shape[0]
    n_rows = n_seq * t_new
    cos2, sin2 = cos_ref[...], sin_ref[...]
    row = lax.broadcasted_iota(jnp.int32, (n_rows, n_rows), 0)
    col = lax.broadcasted_iota(jnp.int32, (n_rows, n_rows), 1)
    row_seq, col_seq = row // t_new, col // t_new
    same_chunk_causal = (row_seq == col_seq) & (row >= col)
    diff = jnp.maximum((row - col).astype(F32), 0.0)
    t = (row - row_seq * t_new).astype(F32)

    for hh in range(n_heads):
        h = first_head + hh
        log_g = log_g_ref[h]
        g_chunk = g_chunk_ref[h]
        qk_cols = slice(hh * RET_DK, (hh + 1) * RET_DK)
        v_cols = slice(hh * RET_DV, (hh + 1) * RET_DV)
        q = _rotate(q_ref[:, qk_cols], cos2, sin2)
        k = _rotate(k_ref[:, qk_cols], cos2, sin2) * _RET_K_SCALE
        vb = v_ref[:, v_cols].astype(BF16)

        decay = jnp.where(same_chunk_causal, jnp.exp(diff * log_g), 0.0)
        scores = lax.dot_general(q.astype(BF16), k.astype(BF16), (((1,), (1,)), ((), ())),
                                 preferred_element_type=F32) * decay
        y = jnp.dot(scores.astype(BF16), vb, preferred_element_type=F32)

        qw = q * jnp.exp((t + 1.0) * log_g)
        y_cross = [jnp.dot(qw[n * t_new:(n + 1) * t_new].astype(BF16), s_in_ref[n, hh].astype(BF16),
                           preferred_element_type=F32) for n in range(n_seq)]
        y = y + jnp.concatenate(y_cross, axis=0)

        kt = (k * jnp.exp((t_new - 1.0 - t) * log_g)).T
        lhs = jnp.concatenate([jnp.where(col_seq == n, kt, 0.0) for n in range(n_seq)], axis=0)
        kv = jnp.dot(lhs.astype(BF16), vb, preferred_element_type=F32)
        for n in range(n_seq):
            s_out_ref[n, hh] = g_chunk * s_in_ref[n, hh] + kv[n * RET_DK:(n + 1) * RET_DK]

        o_ref[:, v_cols] = _silu(gr_ref[:, v_cols]) * _rms(y)


def _merge_kernel(st, x_ref, gate_ref, npost_ref, oa_ref, or_ref, ga_ref, gr_ref,
                  wpa_ref, wpr_ref, wo_ref, o_ref, merged_ref):
    i = pl.program_id(0)
    oa = oa_ref[...].astype(BF16)
    orr = or_ref[...].astype(BF16)
    for c in range(D_MODEL // COL_TILE):
        cols = slice(c * COL_TILE, (c + 1) * COL_TILE)
        a = jnp.dot(oa, wpa_ref[:, cols], preferred_element_type=F32)
        r = jnp.dot(orr, wpr_ref[:, cols], preferred_element_type=F32)
        merged = (jax.nn.sigmoid(ga_ref[:, cols].astype(F32)) * a
                  + jax.nn.sigmoid(gr_ref[:, cols].astype(F32)) * r)
        merged_ref[:, cols] = merged.astype(BF16)
    o_ref[...] = jnp.dot(merged_ref[...], wo_ref[...], preferred_element_type=F32)
    _post_residual(st, i, x_ref, gate_ref, npost_ref[1:2, :], o_ref, 1.0)


def _merge(st, x, mod, npost, o_a, o_r, wide, w_pa, w_pr, w_o):
    gate_a_block = 2 * RET_V // D_MODEL
    row_spec = pl.BlockSpec((st.tile, D_MODEL), lambda i, j: (i, 0))
    resident = lambda shape: pl.BlockSpec(shape, lambda i, j: (0, 0), pipeline_mode=pl.Buffered(1))
    return pl.pallas_call(
        functools.partial(_merge_kernel, st),
        out_shape=jax.ShapeDtypeStruct((st.n_tokens, D_MODEL), F32),
        grid=(st.n_tiles, 1),
        in_specs=[
            row_spec,
            _mod_spec(st, 5),
            pl.BlockSpec((N_SUBLAYERS, D_MODEL), lambda i, j: (0, 0)),
            pl.BlockSpec((st.tile, ATT_Q), lambda i, j: (i, 0)),
            pl.BlockSpec((st.tile, RET_V), lambda i, j: (i, 0)),
            pl.BlockSpec((st.tile, D_MODEL), lambda i, j: (i, gate_a_block)),
            pl.BlockSpec((st.tile, D_MODEL), lambda i, j: (i, gate_a_block + 1)),
            resident((ATT_Q, D_MODEL)), resident((RET_V, D_MODEL)), resident((D_MODEL, D_MODEL)),
        ],
        out_specs=row_spec,
        scratch_shapes=[pltpu.VMEM((st.tile, D_MODEL), BF16)],
        compiler_params=pltpu.CompilerParams(
            dimension_semantics=("parallel", "arbitrary"), vmem_limit_bytes=V7X_VMEM_LIMIT_BYTES),
        name="merge_out",
    )(x, mod, npost, o_a, o_r, wide, wide, w_pa, w_pr, w_o)


def _rotation_tables(pos):
    half = RET_DK // 2
    inv_freq = ROPE_BASE ** (-jnp.linspace(0.0, 1.0, half, dtype=F32))
    ang = pos[:, None] * inv_freq[None, :]
    cos, sin = jnp.cos(ang), jnp.sin(ang)
    return jnp.concatenate([cos, cos], axis=-1), jnp.concatenate([-sin, sin], axis=-1)


def kernel(x_prompt, x_sample, cache_k_win, cache_v_win, state_ret, c_prompt, c_sample, w_ada, b_ada,
           norm_pre, norm_post, w_in, attn_sinks, w_pa, w_pr, w_o,
           ffn1_gate, ffn1_up, ffn1_down, ffn2_gate, ffn2_up, ffn2_down):
    batch, seq, _ = x_prompt.shape
    n_seq, t_new, _ = x_sample.shape
    assert w_ada.shape[0] == 1, "single-layer step"
    assert t_new == V7X_SUBLANES and seq % TOKEN_TILE == 0 and (n_seq * t_new) % TOKEN_TILE == 0
    assert seq % MERGE_TOKEN_TILE == 0 and (n_seq * t_new) % MERGE_TOKEN_TILE == 0
    assert batch <= MOD_PAD_ROWS and n_seq % MOD_PAD_ROWS == 0

    c_all = jnp.concatenate(
        [c_sample, c_prompt, jnp.zeros((MOD_PAD_ROWS - batch, D_MODEL), F32)], axis=0)
    mod = _ada(c_all, w_ada[0], b_ada[0])

    def prompt_stream(tile):
        return _Stream(batch * seq, tile, seq, MOD_PAD_ROWS, n_seq // MOD_PAD_ROWS)

    def sample_stream(tile):
        return _Stream(n_seq * t_new, tile, t_new, tile // t_new, 0)

    prompt, prompt_m = prompt_stream(TOKEN_TILE), prompt_stream(MERGE_TOKEN_TILE)
    sample, sample_m = sample_stream(TOKEN_TILE), sample_stream(MERGE_TOKEN_TILE // 2)

    npre, npost = norm_pre[0], norm_post[0]
    sinks = attn_sinks[0]
    bf = lambda w: w[0].astype(BF16)
    w_pa_b, w_pr_b, w_o_b = bf(w_pa), bf(w_pr), bf(w_o)

    xs = x_sample.reshape(n_seq * t_new, D_MODEL)
    xs, f1g, f1u, f1d = _ffn(sample, 0, xs, mod, npre, npost,
                             ffn1_gate[0], ffn1_up[0], ffn1_down[0], emit_bf16=True)
    qa_s, kva_s, qkr_s, wide_s, w_in_b = _proj(sample, xs, mod, npre, w_in[0], F32, emit_bf16=True)

    xp = x_prompt.reshape(batch * seq, D_MODEL)
    xp = _ffn(prompt, 0, xp, mod, npre, npost, f1g, f1u, f1d)
    cos_s, sin_s = _rotation_tables(jnp.arange(t_new, dtype=F32) + PAST_LEN)
    qa, kva, qkr, wide, o_r_s, state_s = _proj(
        prompt, xp, mod, npre, w_in_b, BF16, side=(qkr_s, wide_s, cos_s, sin_s, state_ret[0], t_new))
    o_a = _attn_prompt(qa, kva, sinks, batch, seq)
    cos_p, sin_p = _rotation_tables(jnp.arange(seq, dtype=F32))
    o_r, state_p = _ret_prompt(qkr, wide, cos_p, sin_p, batch, seq)
    xp = _merge(prompt_m, xp, mod, npost, o_a, o_r, wide, w_pa_b, w_pr_b, w_o_b)
    kva_p = kva.reshape(batch, seq, 2 * ATT_KV)[:, seq - WINDOW:]
    kv_shape = (1, batch, WINDOW, ATT_KV_HEADS, ATT_HEAD_DIM)
    k_win_p = kva_p[..., :ATT_KV].reshape(kv_shape)
    v_win_p = kva_p[..., ATT_KV:].reshape(kv_shape)

    sink_rows = jnp.repeat(sinks, t_new)[:, None]
    o_a_s, k_s, v_s = _attn_sample(
        qa_s, kva_s,
        cache_k_win[0].reshape(n_seq, WINDOW, ATT_KV), cache_v_win[0].reshape(n_seq, WINDOW, ATT_KV),
        sink_rows, n_seq, t_new)
    xs = _merge(sample_m, xs, mod, npost, o_a_s, o_r_s, wide_s, w_pa_b, w_pr_b, w_o_b)
    xs, f2g, f2u, f2d = _ffn(sample, 2, xs, mod, npre, npost,
                             ffn2_gate[0], ffn2_up[0], ffn2_down[0], emit_bf16=True)
    xp = _ffn(prompt, 2, xp, mod, npre, npost, f2g, f2u, f2d)
    kvs_shape = (1, n_seq, WINDOW, ATT_KV_HEADS, ATT_HEAD_DIM)

    return (xp.reshape(batch, seq, D_MODEL), xs.reshape(n_seq, t_new, D_MODEL),
            k_win_p, v_win_p, state_p[None],
            k_s.reshape(kvs_shape), v_s.reshape(kvs_shape), state_s[None])
```

```python
import functools
import math
from typing import NamedTuple

import jax
import jax.numpy as jnp
from jax import lax
from jax.experimental import pallas as pl
from jax.experimental.pallas import tpu as pltpu

F32 = jnp.float32
BF16 = jnp.bfloat16

D_MODEL = 2048
WINDOW = 128
ATT_HEADS = 16
ATT_KV_HEADS = 4
ATT_HEAD_DIM = 64
ATT_GROUP = ATT_HEADS // ATT_KV_HEADS
ATT_Q = ATT_HEADS * ATT_HEAD_DIM
ATT_KV = ATT_KV_HEADS * ATT_HEAD_DIM
RET_HEADS = 8
RET_DK = 128
RET_DV = 256
RET_CHUNK = 128
RET_QK = RET_HEADS * RET_DK
RET_V = RET_HEADS * RET_DV
ROPE_BASE = 10000.0
D_FF = 5632
NORM_EPS = 1e-6
N_SUBLAYERS = 3
PAST_LEN = 16384
D_IN = ATT_Q + 2 * ATT_KV + 2 * RET_QK + 2 * RET_V + 2 * D_MODEL

V7X_SUBLANES = 8
V7X_BF16_ROWS = 16
NORM_GROUP_ROWS = 128
V7X_VMEM_LIMIT_BYTES = 60 * 1024 * 1024

TOKEN_TILE = 1024
MERGE_TOKEN_TILE = 512
FF_SUBTILE = 256
FF_SUBTILES_BF16 = 2
F32_WEIGHT_K_PARTS = 4
COL_TILE = 512
ADA_COL_TILE = 1024
ATTN_BLOCKS_PER_STEP = 2
RET_CHUNKS_PER_STEP = 4
SAMPLE_SEQS_PER_STEP = 8
RET_SAMPLE_SEQS = 16
MOD_PAD_ROWS = 8

_QA_TILES = ATT_Q // COL_TILE
_KVA_TILES = 2 * ATT_KV // COL_TILE
_QKR_TILES = 2 * RET_QK // COL_TILE
_WIDE_COLS = 2 * RET_V + 2 * D_MODEL
_WIDE_TILES = _WIDE_COLS // COL_TILE
_QKR_START = _QA_TILES + _KVA_TILES
_WIDE_START = _QKR_START + _QKR_TILES
assert _WIDE_START + _WIDE_TILES == D_IN // COL_TILE and _KVA_TILES == 1

_ALIBI_SLOPES = [2.0 ** (-8.0 * (h + 1) / ATT_HEADS) for h in range(ATT_HEADS)]
_RET_LOG_G = [math.log(1.0 - 2.0 ** (-5.0 - h)) for h in range(RET_HEADS)]
_ATT_SCALE = ATT_HEAD_DIM ** -0.5
_RET_K_SCALE = RET_DK ** -0.5


class _Stream(NamedTuple):
    n_tokens: int
    tile: int
    rows_per_mod: int
    mod_block_rows: int
    mod_block_base: int

    @property
    def n_tiles(self):
        return self.n_tokens // self.tile

    @property
    def sub_rows(self):
        return min(self.rows_per_mod, self.tile)

    @property
    def group_rows(self):
        return max(self.sub_rows, NORM_GROUP_ROWS)

    @property
    def n_groups(self):
        return self.tile // self.group_rows

    @property
    def mods_per_group(self):
        return self.group_rows // self.sub_rows


def _rms(x):
    return x * lax.rsqrt(jnp.mean(x * x, axis=-1, keepdims=True) + NORM_EPS)


def _silu(x):
    return x * jax.nn.sigmoid(x)


def _for_groups(n_groups, fn):
    if n_groups == 1:
        fn(0)
    else:
        def body(g, carry):
            fn(g)
            return carry
        lax.fori_loop(0, n_groups, body, 0)


def _group_base(st, g):
    return 0 if st.n_groups == 1 else pl.multiple_of(g * st.group_rows, st.group_rows)


def _mod_row(st, i, g, s):
    if st.rows_per_mod >= st.tile:
        return (i * st.tile) // st.rows_per_mod
    return g * st.mods_per_group + s


def _pre_norm(st, i, x_ref, shift_ref, scale_ref, gain, h_ref):
    def group(g):
        base = _group_base(st, g)
        parts = []
        for s in range(st.mods_per_group):
            rows = pl.ds(base + s * st.sub_rows, st.sub_rows)
            m = _mod_row(st, i, g, s)
            x = x_ref[rows, :]
            row_gain = gain * (1.0 + scale_ref[pl.ds(m, 1), :])
            sh = shift_ref[pl.ds(m, 1), :]
            parts.append(_rms(x) * row_gain + sh)
        h = parts[0] if len(parts) == 1 else jnp.concatenate(parts, axis=0)
        h_ref[pl.ds(base, st.group_rows), :] = h.astype(BF16)
    _for_groups(st.n_groups, group)


def _post_residual(st, i, x_ref, gate_ref, gain, o_ref, coeff):
    def group(g):
        base = _group_base(st, g)
        results = []
        for s in range(st.mods_per_group):
            rows = pl.ds(base + s * st.sub_rows, st.sub_rows)
            m = _mod_row(st, i, g, s)
            row_gain = gate_ref[pl.ds(m, 1), :] * gain
            if coeff != 1.0:
                row_gain = coeff * row_gain
            results.append((rows, x_ref[rows, :] + _rms(o_ref[rows, :]) * row_gain))
        for rows, value in results:
            o_ref[rows, :] = value
    _for_groups(st.n_groups, group)


def _ada_kernel(c_ref, w_ref, b_ref, o_ref):
    a = _silu(c_ref[...]).astype(BF16)
    o_ref[0] = jnp.dot(a, w_ref[...].astype(BF16), preferred_element_type=F32) + b_ref[...]


def _ada(c_all, w_ada, b_ada):
    rows = c_all.shape[0]
    n_vec = N_SUBLAYERS * 3
    per_vec = D_MODEL // ADA_COL_TILE
    return pl.pallas_call(
        _ada_kernel,
        out_shape=jax.ShapeDtypeStruct((n_vec, rows, D_MODEL), F32),
        grid=(n_vec * per_vec,),
        in_specs=[
            pl.BlockSpec((rows, D_MODEL), lambda j: (0, 0)),
            pl.BlockSpec((D_MODEL, ADA_COL_TILE), lambda j: (0, j)),
            pl.BlockSpec((1, ADA_COL_TILE), lambda j: (0, j)),
        ],
        out_specs=pl.BlockSpec((1, rows, ADA_COL_TILE), lambda j: (j // per_vec, 0, j % per_vec)),
        compiler_params=pltpu.CompilerParams(
            dimension_semantics=("arbitrary",), vmem_limit_bytes=V7X_VMEM_LIMIT_BYTES),
        name="ada_mod",
    )(c_all, w_ada, b_ada.reshape(1, -1))


def _mod_spec(st, vec):
    if st.rows_per_mod >= st.tile:
        index = lambda i, j: (vec, st.mod_block_base, 0)
    else:
        index = lambda i, j: (vec, st.mod_block_base + i, 0)
    return pl.BlockSpec((None, st.mod_block_rows, D_MODEL), index)


def _ffn_kernel(st, sub, k_parts, emit_bf16, x_ref, shift_ref, scale_ref, gate_ref, npre_ref, npost_ref,
                *refs):
    i = pl.program_id(0)
    j = pl.program_id(1)
    n_up = max(k_parts, 1)
    wg_refs, wu_refs, wd_ref, o_ref = refs[:n_up], refs[n_up:2 * n_up], refs[2 * n_up], refs[2 * n_up + 1]
    rest = refs[2 * n_up + 2:]
    h_ref = rest[-1]

    def up_f32(h, w_refs, out_ref):
        rows = D_MODEL // k_parts
        acc = None
        for p, w_ref in enumerate(w_refs):
            w = w_ref[...].astype(BF16)
            if out_ref is not None:
                out_ref[p * rows:(p + 1) * rows, :] = w
            part = jnp.dot(h[:, p * rows:(p + 1) * rows], w, preferred_element_type=F32)
            acc = part if acc is None else acc + part
        return acc

    def partial_down():
        h = h_ref[...]
        if k_parts:
            wg_out_ref, wu_out_ref, wd_out_ref = rest[:3] if emit_bf16 else (None, None, None)
            wd = wd_ref[...].astype(BF16)
            if emit_bf16:
                wd_out_ref[...] = wd
            g = up_f32(h, wg_refs, wg_out_ref)
            u = up_f32(h, wu_refs, wu_out_ref)
            return jnp.dot((_silu(g) * u).astype(BF16), wd, preferred_element_type=F32)
        acts = []
        for t in range(wg_refs[0].shape[0]):
            g = jnp.dot(h, wg_refs[0][t], preferred_element_type=F32)
            u = jnp.dot(h, wu_refs[0][t], preferred_element_type=F32)
            acts.append((_silu(g) * u).astype(BF16))
        return jnp.dot(jnp.concatenate(acts, axis=1), wd_ref[...], preferred_element_type=F32)

    @pl.when(j == 0)
    def _():
        _pre_norm(st, i, x_ref, shift_ref, scale_ref, npre_ref[sub:sub + 1, :], h_ref)
        o_ref[...] = partial_down()

    last = pl.num_programs(1) - 1

    @pl.when((j > 0) & (j < last))
    def _():
        o_ref[...] += partial_down()

    @pl.when(j == last)
    def _():
        o_ref[...] += partial_down()
        _post_residual(st, i, x_ref, gate_ref, npost_ref[sub:sub + 1, :], o_ref, 0.5)


def _ffn(st, sub, x, mod, npre, npost, wg, wu, wd, emit_bf16=False):
    tiled = wg.ndim == 3
    n_sub = FF_SUBTILES_BF16 if tiled else 1
    ff_tile = n_sub * FF_SUBTILE
    row_spec = pl.BlockSpec((st.tile, D_MODEL), lambda i, j: (i, 0))
    full_spec = pl.BlockSpec((N_SUBLAYERS, D_MODEL), lambda i, j: (0, 0))
    tile_spec = pl.BlockSpec((None, D_MODEL, FF_SUBTILE), lambda i, j: (j, 0, 0))
    if tiled:
        k_parts = 0
        up_specs = [pl.BlockSpec((n_sub, D_MODEL, FF_SUBTILE), lambda i, j: (j, 0, 0))]
        up_args = lambda w: [w]
    else:
        k_parts = F32_WEIGHT_K_PARTS
        rows = D_MODEL // k_parts
        up_specs = [pl.BlockSpec((rows, ff_tile), functools.partial(lambda p, i, j: (p, j), p))
                    for p in range(k_parts)]
        up_args = lambda w: [w] * k_parts
    down_spec = pl.BlockSpec((ff_tile, D_MODEL), lambda i, j: (j, 0))
    out_shape = [jax.ShapeDtypeStruct((st.n_tokens, D_MODEL), F32)]
    out_specs = [row_spec]
    if emit_bf16:
        assert st.n_tiles == 1 and not tiled
        up_shape = jax.ShapeDtypeStruct((D_FF // FF_SUBTILE, D_MODEL, FF_SUBTILE), BF16)
        out_shape += [up_shape, up_shape, jax.ShapeDtypeStruct((D_FF, D_MODEL), BF16)]
        out_specs += [tile_spec, tile_spec, down_spec]
    outs = pl.pallas_call(
        functools.partial(_ffn_kernel, st, sub, k_parts, emit_bf16),
        out_shape=out_shape,
        grid=(st.n_tiles, D_FF // ff_tile),
        in_specs=[
            row_spec,
            _mod_spec(st, 3 * sub + 0), _mod_spec(st, 3 * sub + 1), _mod_spec(st, 3 * sub + 2),
            full_spec, full_spec,
            *up_specs, *up_specs, down_spec,
        ],
        out_specs=out_specs,
        scratch_shapes=[pltpu.VMEM((st.tile, D_MODEL), BF16)],
        compiler_params=pltpu.CompilerParams(
            dimension_semantics=("parallel", "arbitrary"), vmem_limit_bytes=V7X_VMEM_LIMIT_BYTES),
        name=f"ffn{sub}",
    )(x, mod, mod, mod, npre, npost, *up_args(wg), *up_args(wu), wd)
    return outs if emit_bf16 else outs[0]


_N_SIDE_IN = 9
_N_SIDE_OUT = 2


def _proj_kernel(st, k_parts, emit_bf16, side_t_new, x_ref, shift_ref, scale_ref, npre_ref, *refs):
    i = pl.program_id(0)
    j = pl.program_id(1)
    w_refs = refs[:k_parts]
    refs = refs[k_parts:]
    side_in = ()
    if side_t_new is not None:
        side_in, refs = refs[:_N_SIDE_IN], refs[_N_SIDE_IN:]
    qa_ref, kva_ref, qkr_ref, wide_ref = refs[:4]
    rest = refs[4:]
    h_ref = rest[-1]
    rows = D_MODEL // k_parts

    def project(dst_ref):
        h = h_ref[...]
        acc = None
        for p, w_ref in enumerate(w_refs):
            w = w_ref[...].astype(BF16)
            if emit_bf16:
                rest[0][p * rows:(p + 1) * rows, :] = w
            part = jnp.dot(h[:, p * rows:(p + 1) * rows], w, preferred_element_type=F32)
            acc = part if acc is None else acc + part
        dst_ref[...] = acc.astype(dst_ref.dtype)

    @pl.when(j == 0)
    def _():
        _pre_norm(st, i, x_ref, shift_ref, scale_ref, npre_ref[1:2, :], h_ref)
        project(qa_ref)

    @pl.when((j > 0) & (j < _QA_TILES))
    def _():
        project(qa_ref)

    @pl.when(j == _QA_TILES)
    def _():
        project(kva_ref)

    @pl.when((j >= _QKR_START) & (j < _WIDE_START))
    def _():
        project(qkr_ref)

    if side_t_new is None:
        @pl.when(j >= _WIDE_START)
        def _():
            project(wide_ref)
    else:
        side_out = rest[-1 - _N_SIDE_OUT:-1]

        wide_step = j - _WIDE_START
        is_side_step = (wide_step >= 0) & (wide_step % _SIDE_STRIDE == 0) & (wide_step < _SIDE_STRIDE * RET_HEADS)

        @pl.when(is_side_step)
        def _():
            project(wide_ref)
            log_g_ref, g_chunk_ref, *side_vmem = side_in
            _ret_sample_heads(side_t_new, wide_step // _SIDE_STRIDE, 1, log_g_ref, g_chunk_ref,
                              *side_vmem, *side_out)

        @pl.when((wide_step >= 0) & jnp.logical_not(is_side_step))
        def _():
            project(wide_ref)


_SIDE_STRIDE = _WIDE_TILES // RET_HEADS


def _side_head(j):
    return jnp.clip((j - _WIDE_START) // _SIDE_STRIDE, 0, RET_HEADS - 1)


def _proj(st, x, mod, npre, w_in, narrow_dtype, emit_bf16=False, side=None):
    n = st.n_tokens
    tiled = w_in.ndim == 3
    row_spec = pl.BlockSpec((st.tile, D_MODEL), lambda i, j: (i, 0))
    out_block = (st.tile, COL_TILE)
    tile_spec = pl.BlockSpec((None, D_MODEL, COL_TILE), lambda i, j: (j, 0, 0))
    out_shape = [
        jax.ShapeDtypeStruct((n, ATT_Q), narrow_dtype),
        jax.ShapeDtypeStruct((n, 2 * ATT_KV), F32),
        jax.ShapeDtypeStruct((n, 2 * RET_QK), F32),
        jax.ShapeDtypeStruct((n, _WIDE_COLS), narrow_dtype),
    ]
    out_specs = [
        pl.BlockSpec(out_block, lambda i, j: (i, jnp.minimum(j, _QA_TILES - 1))),
        pl.BlockSpec(out_block, lambda i, j: (i, 0)),
        pl.BlockSpec(out_block, lambda i, j: (i, jnp.clip(j - _QKR_START, 0, _QKR_TILES - 1))),
        pl.BlockSpec(out_block, lambda i, j: (i, jnp.maximum(j - _WIDE_START, 0))),
    ]
    if emit_bf16:
        assert st.n_tiles == 1 and not tiled
        out_shape.append(jax.ShapeDtypeStruct((D_IN // COL_TILE, D_MODEL, COL_TILE), BF16))
        out_specs.append(tile_spec)
    if tiled:
        k_parts = 1
        w_specs = [tile_spec]
    else:
        k_parts = F32_WEIGHT_K_PARTS
        w_specs = [pl.BlockSpec((D_MODEL // k_parts, COL_TILE), functools.partial(lambda p, i, j: (p, j), p))
                   for p in range(k_parts)]
    side_specs, side_args, side_t_new = [], [], None
    if side is not None:
        qkr_s, wide_s, cos2, sin2, state, side_t_new = side
        nb = RET_SAMPLE_SEQS
        n_rows = nb * side_t_new
        assert n_rows == RET_DK and state.shape[0] == st.n_tiles * nb and _WIDE_TILES >= RET_HEADS
        smem = pl.BlockSpec(memory_space=pltpu.SMEM)
        table_spec = pl.BlockSpec((n_rows, RET_DK), lambda i, j: (0, 0))
        state_spec = pl.BlockSpec((nb, 1, RET_DK, RET_DV), lambda i, j: (i, _side_head(j), 0, 0))
        side_specs = [
            smem, smem,
            pl.BlockSpec((n_rows, RET_DK), lambda i, j: (i, _side_head(j))),
            pl.BlockSpec((n_rows, RET_DK), lambda i, j: (i, RET_HEADS + _side_head(j))),
            pl.BlockSpec((n_rows, RET_DV), lambda i, j: (i, _side_head(j))),
            pl.BlockSpec((n_rows, RET_DV), lambda i, j: (i, RET_HEADS + _side_head(j))),
            table_spec, table_spec,
            state_spec,
        ]
        side_args = [
            jnp.asarray(_RET_LOG_G, F32),
            jnp.asarray([math.exp(side_t_new * g) for g in _RET_LOG_G], F32),
            qkr_s, qkr_s, wide_s, wide_s, jnp.tile(cos2, (nb, 1)), jnp.tile(sin2, (nb, 1)), state,
        ]
        assert len(side_specs) == _N_SIDE_IN
        out_shape += [jax.ShapeDtypeStruct((state.shape[0] * side_t_new, RET_V), F32),
                      jax.ShapeDtypeStruct(state.shape, F32)]
        out_specs += [pl.BlockSpec((n_rows, RET_DV), lambda i, j: (i, _side_head(j))), state_spec]
    outs = pl.pallas_call(
        functools.partial(_proj_kernel, st, k_parts, emit_bf16, side_t_new),
        out_shape=out_shape,
        grid=(st.n_tiles, D_IN // COL_TILE),
        in_specs=[
            row_spec,
            _mod_spec(st, 3), _mod_spec(st, 4),
            pl.BlockSpec((N_SUBLAYERS, D_MODEL), lambda i, j: (0, 0)),
            *w_specs,
            *side_specs,
        ],
        out_specs=out_specs,
        scratch_shapes=[pltpu.VMEM((st.tile, D_MODEL), BF16)],
        compiler_params=pltpu.CompilerParams(
            dimension_semantics=("parallel", "arbitrary"), vmem_limit_bytes=V7X_VMEM_LIMIT_BYTES),
        name="in_proj",
    )(x, mod, mod, npre, *([w_in] * k_parts), *side_args)
    return outs


def _attention(q, k2, v2, sinks_ref, first_valid_key):
    tq = q.shape[0]
    a_idx = lax.broadcasted_iota(jnp.int32, (tq, 2 * WINDOW), 0)
    b_idx = lax.broadcasted_iota(jnp.int32, (tq, 2 * WINDOW), 1)
    dist = WINDOW + a_idx - b_idx
    mask = (dist >= 0) & (dist <= WINDOW) & (b_idx >= first_valid_key)
    dist_f = jnp.where(mask, dist.astype(F32), jnp.inf)
    q = q * _ATT_SCALE
    outs = []
    for kv in range(ATT_KV_HEADS):
        cols = slice(kv * ATT_HEAD_DIM, (kv + 1) * ATT_HEAD_DIM)
        kk = k2[:, cols]
        vv = v2[:, cols]
        heads = range(kv * ATT_GROUP, (kv + 1) * ATT_GROUP)
        qg = jnp.concatenate(
            [q[:, h * ATT_HEAD_DIM:(h + 1) * ATT_HEAD_DIM] for h in heads], axis=0).astype(BF16)
        s_all = lax.dot_general(qg, kk, (((1,), (1,)), ((), ())), preferred_element_type=F32)
        probs = []
        for g, h in enumerate(heads):
            s = s_all[g * tq:(g + 1) * tq] - _ALIBI_SLOPES[h] * dist_f
            sink = sinks_ref[h]
            m = jnp.maximum(jnp.max(s, axis=-1, keepdims=True), sink)
            p = jnp.exp(s - m)
            inv = 1.0 / (jnp.sum(p, axis=-1, keepdims=True) + jnp.exp(sink - m))
            probs.append((p * inv).astype(BF16))
        o_all = jnp.dot(jnp.concatenate(probs, axis=0), vv, preferred_element_type=F32)
        outs.extend(o_all[g * tq:(g + 1) * tq] for g in range(ATT_GROUP))
    return jnp.concatenate(outs, axis=-1)


def _attn_prompt_kernel(n_cast, sinks_ref, q_ref, kc_ref, vc_ref, kp_ref, vp_ref, *rest):
    cast_in, o_ref, cast_out = rest[:n_cast], rest[n_cast], rest[n_cast + 1:]
    for src_ref, dst_ref in zip(cast_in, cast_out):
        dst_ref[...] = src_ref[...].astype(BF16)
    step = pl.program_id(1)
    k_tiles = [kp_ref[...]] + [kc_ref[r * WINDOW:(r + 1) * WINDOW, :] for r in range(ATTN_BLOCKS_PER_STEP)]
    v_tiles = [vp_ref[...]] + [vc_ref[r * WINDOW:(r + 1) * WINDOW, :] for r in range(ATTN_BLOCKS_PER_STEP)]
    for r in range(ATTN_BLOCKS_PER_STEP):
        rows = slice(r * WINDOW, (r + 1) * WINDOW)
        k2 = jnp.concatenate(k_tiles[r:r + 2], axis=0).astype(BF16)
        v2 = jnp.concatenate(v_tiles[r:r + 2], axis=0).astype(BF16)
        first_valid = jnp.where(step == 0, WINDOW, 0) if r == 0 else 0
        o_ref[rows, :] = _attention(q_ref[rows, :], k2, v2, sinks_ref, first_valid).astype(o_ref.dtype)


def _attn_prompt(qa, kva, sinks, batch, seq, cast_weights):
    per = ATTN_BLOCKS_PER_STEP
    ns = seq // (per * WINDOW)
    n_steps = batch * ns
    cur = lambda col: (lambda b, i: (b * ns + i, col))
    prev = lambda col: (lambda b, i: (b * ns * per + jnp.maximum(i * per - 1, 0), col))
    cast_specs = []
    for w in cast_weights:
        slab = w.shape[0] // n_steps
        assert slab * n_steps == w.shape[0] and slab % V7X_BF16_ROWS == 0
        cast_specs.append(pl.BlockSpec((slab, w.shape[1]), lambda b, i: (b * ns + i, 0)))
    return pl.pallas_call(
        functools.partial(_attn_prompt_kernel, len(cast_weights)),
        out_shape=[jax.ShapeDtypeStruct((batch * seq, ATT_Q), BF16)]
        + [jax.ShapeDtypeStruct(w.shape, BF16) for w in cast_weights],
        grid=(batch, ns),
        in_specs=[
            pl.BlockSpec(memory_space=pltpu.SMEM),
            pl.BlockSpec((per * WINDOW, ATT_Q), lambda b, i: (b * ns + i, 0)),
            pl.BlockSpec((per * WINDOW, ATT_KV), cur(0)),
            pl.BlockSpec((per * WINDOW, ATT_KV), cur(1)),
            pl.BlockSpec((WINDOW, ATT_KV), prev(0)),
            pl.BlockSpec((WINDOW, ATT_KV), prev(1)),
            *cast_specs,
        ],
        out_specs=[pl.BlockSpec((per * WINDOW, ATT_Q), lambda b, i: (b * ns + i, 0)), *cast_specs],
        compiler_params=pltpu.CompilerParams(
            dimension_semantics=("parallel", "arbitrary"), vmem_limit_bytes=V7X_VMEM_LIMIT_BYTES),
        name="attn_prompt",
    )(sinks, qa, kva, kva, kva, kva, *cast_weights)


def _rotate(x, cos2, sin2):
    return x * cos2 + pltpu.roll(x, RET_DK // 2, axis=1) * sin2


def _ret_prompt_kernel(q_ref, k_ref, v_ref, gr_ref, cos_ref, sin_ref, o_ref, s_ref,
                       decay_ref, qw_ref, kw_ref):
    @pl.when(pl.program_id(1) == 0)
    def _():
        s_ref[...] = jnp.zeros_like(s_ref)
        row = lax.broadcasted_iota(jnp.int32, (RET_CHUNK, RET_CHUNK), 0).astype(F32)
        col = lax.broadcasted_iota(jnp.int32, (RET_CHUNK, RET_CHUNK), 1).astype(F32)
        diff = row - col
        for h in range(RET_HEADS):
            log_g = _RET_LOG_G[h]
            decay_ref[h] = jnp.where(diff >= 0, jnp.exp(jnp.maximum(diff, 0.0) * log_g), 0.0)
            qw_ref[h] = jnp.exp((row + 1.0) * log_g)
            kw_ref[h] = jnp.exp((RET_CHUNK - 1.0 - row) * log_g)

    for r in range(RET_CHUNKS_PER_STEP):
        rows = slice(r * RET_CHUNK, (r + 1) * RET_CHUNK)
        cos2, sin2 = cos_ref[rows, :], sin_ref[rows, :]
        for h in range(RET_HEADS):
            qk_cols = slice(h * RET_DK, (h + 1) * RET_DK)
            v_cols = slice(h * RET_DV, (h + 1) * RET_DV)
            qh = _rotate(q_ref[rows, qk_cols], cos2, sin2)
            kh = _rotate(k_ref[rows, qk_cols], cos2, sin2) * _RET_K_SCALE
            vb = v_ref[rows, v_cols].astype(BF16)
            s_prev = s_ref[0, h]
            scores = lax.dot_general(qh.astype(BF16), kh.astype(BF16), (((1,), (1,)), ((), ())),
                                     preferred_element_type=F32) * decay_ref[h]
            y = jnp.dot(scores.astype(BF16), vb, preferred_element_type=F32)
            y = y + jnp.dot((qh * qw_ref[h]).astype(BF16), s_prev.astype(BF16),
                            preferred_element_type=F32)
            kt = (kh * kw_ref[h]).T.astype(BF16)
            s_ref[0, h] = (math.exp(RET_CHUNK * _RET_LOG_G[h]) * s_prev
                           + jnp.dot(kt, vb, preferred_element_type=F32))
            gate = gr_ref[rows, v_cols].astype(F32)
            o_ref[rows, v_cols] = (_silu(gate) * _rms(y)).astype(o_ref.dtype)


def _ret_prompt(qkr, wide, cos2, sin2, batch, seq):
    step_rows = RET_CHUNKS_PER_STEP * RET_CHUNK
    nc = seq // step_rows
    rows = lambda col: (lambda b, c: (b * nc + c, col))
    return pl.pallas_call(
        _ret_prompt_kernel,
        out_shape=(
            jax.ShapeDtypeStruct((batch * seq, RET_V), BF16),
            jax.ShapeDtypeStruct((batch, RET_HEADS, RET_DK, RET_DV), F32),
        ),
        grid=(batch, nc),
        in_specs=[
            pl.BlockSpec((step_rows, RET_QK), rows(0)),
            pl.BlockSpec((step_rows, RET_QK), rows(1)),
            pl.BlockSpec((step_rows, RET_V), rows(0)),
            pl.BlockSpec((step_rows, RET_V), rows(1)),
            pl.BlockSpec((step_rows, RET_DK), lambda b, c: (c, 0)),
            pl.BlockSpec((step_rows, RET_DK), lambda b, c: (c, 0)),
        ],
        out_specs=(
            pl.BlockSpec((step_rows, RET_V), rows(0)),
            pl.BlockSpec((1, RET_HEADS, RET_DK, RET_DV), lambda b, c: (b, 0, 0, 0)),
        ),
        scratch_shapes=[pltpu.VMEM((RET_HEADS, RET_CHUNK, RET_CHUNK), F32)] * 3,
        compiler_params=pltpu.CompilerParams(
            dimension_semantics=("parallel", "arbitrary"), vmem_limit_bytes=V7X_VMEM_LIMIT_BYTES),
        name="ret_prompt",
    )(qkr, qkr, wide, wide, cos2, sin2)


def _attn_sample_kernel(t_new, sink_ref, qa_ref, kn_ref, vn_ref, ck_ref, cv_ref,
                        oa_ref, ko_ref, vo_ref):
    n_rows = ATT_HEADS * t_new
    pair = 2 * ATT_HEAD_DIM
    row = lax.broadcasted_iota(jnp.int32, (n_rows, 2 * WINDOW), 0)
    key = lax.broadcasted_iota(jnp.int32, (n_rows, 2 * WINDOW), 1)
    head = row // t_new
    dist = WINDOW + (row - head * t_new) - key
    slope = jnp.exp2((head.astype(F32) + 1.0) * (-8.0 / ATT_HEADS))
    bias = jnp.where((dist >= 0) & (dist <= WINDOW), -slope * dist.astype(F32), -jnp.inf)
    sink = sink_ref[...]
    lower = lax.broadcasted_iota(jnp.int32, (t_new, pair), 1) < ATT_HEAD_DIM
    zero_group = jnp.zeros((t_new, pair), F32)
    zero_keys = jnp.zeros((WINDOW - t_new, ATT_KV), F32)

    for n in range(SAMPLE_SEQS_PER_STEP):
        rows = slice(n * t_new, (n + 1) * t_new)
        kc, vc = ck_ref[n], cv_ref[n]
        kn, vn = kn_ref[rows, :], vn_ref[rows, :]
        ko_ref[n, :WINDOW - t_new, :] = kc[t_new:]
        ko_ref[n, WINDOW - t_new:, :] = kn
        vo_ref[n, :WINDOW - t_new, :] = vc[t_new:]
        vo_ref[n, WINDOW - t_new:, :] = vn
        k2 = jnp.concatenate([kc, kn, zero_keys], axis=0).astype(BF16)
        v2 = jnp.concatenate([vc, vn, zero_keys], axis=0).astype(BF16)

        q = qa_ref[rows, :] * _ATT_SCALE
        q_swapped = pltpu.roll(q, ATT_HEAD_DIM, axis=1)
        blocks = []
        for h in range(ATT_HEADS):
            kv = h // ATT_GROUP
            want_lower = kv % 2 == 0
            if (h % 2 == 0) == want_lower:
                src = q[:, (h // 2) * pair:(h // 2 + 1) * pair]
            else:
                g = (h + 1) // 2 % (ATT_HEADS // 2)
                src = q_swapped[:, g * pair:(g + 1) * pair]
            piece = jnp.where(lower if want_lower else ~lower, src, 0.0)
            blocks.append(jnp.concatenate(
                [piece, zero_group] if kv // 2 == 0 else [zero_group, piece], axis=1))
        q_bd = jnp.concatenate(blocks, axis=0).astype(BF16)

        s = lax.dot_general(q_bd, k2, (((1,), (1,)), ((), ())), preferred_element_type=F32) + bias
        m = jnp.maximum(jnp.max(s, axis=-1, keepdims=True), sink)
        p = jnp.exp(s - m)
        inv = 1.0 / (jnp.sum(p, axis=-1, keepdims=True) + jnp.exp(sink - m))
        o = jnp.dot((p * inv).astype(BF16), v2, preferred_element_type=F32)

        outs = []
        for g in range(ATT_HEADS // 2):
            h0 = 2 * g
            kv = h0 // ATT_GROUP
            cols = slice((kv // 2) * pair, (kv // 2 + 1) * pair)
            a = o[h0 * t_new:(h0 + 1) * t_new, cols]
            b = o[(h0 + 1) * t_new:(h0 + 2) * t_new, cols]
            if kv % 2 == 0:
                b = pltpu.roll(b, ATT_HEAD_DIM, axis=1)
            else:
                a = pltpu.roll(a, ATT_HEAD_DIM, axis=1)
            outs.append(jnp.where(lower, a, b))
        oa_ref[rows, :] = jnp.concatenate(outs, axis=1)


def _attn_sample(qa, kva, cache_k, cache_v, sink_rows, n_seq, t_new):
    nb = SAMPLE_SEQS_PER_STEP
    rows = lambda col: (lambda n: (n, col))
    cache_spec = pl.BlockSpec((nb, WINDOW, ATT_KV), lambda n: (n, 0, 0))
    return pl.pallas_call(
        functools.partial(_attn_sample_kernel, t_new),
        out_shape=(
            jax.ShapeDtypeStruct((n_seq * t_new, ATT_Q), F32),
            jax.ShapeDtypeStruct(cache_k.shape, F32),
            jax.ShapeDtypeStruct(cache_v.shape, F32),
        ),
        grid=(n_seq // nb,),
        in_specs=[
            pl.BlockSpec((ATT_HEADS * t_new, 1), lambda n: (0, 0)),
            pl.BlockSpec((nb * t_new, ATT_Q), rows(0)),
            pl.BlockSpec((nb * t_new, ATT_KV), rows(0)),
            pl.BlockSpec((nb * t_new, ATT_KV), rows(1)),
            cache_spec, cache_spec,
        ],
        out_specs=(pl.BlockSpec((nb * t_new, ATT_Q), rows(0)), cache_spec, cache_spec),
        compiler_params=pltpu.CompilerParams(
            dimension_semantics=("parallel",), vmem_limit_bytes=V7X_VMEM_LIMIT_BYTES),
        name="attn_sample",
    )(sink_rows, qa, kva, kva, cache_k, cache_v)


def _ret_sample_heads(t_new, first_head, n_heads, log_g_ref, g_chunk_ref, q_ref, k_ref, v_ref, gr_ref,
                      cos_ref, sin_ref, s_in_ref, o_ref, s_out_ref):
    n_seq = s_in_ref.shape[0]
    n_rows = n_seq * t_new
    cos2, sin2 = cos_ref[...], sin_ref[...]
    row = lax.broadcasted_iota(jnp.int32, (n_rows, n_rows), 0)
    col = lax.broadcasted_iota(jnp.int32, (n_rows, n_rows), 1)
    row_seq, col_seq = row // t_new, col // t_new
    same_chunk_causal = (row_seq == col_seq) & (row >= col)
    diff = jnp.maximum((row - col).astype(F32), 0.0)
    t = (row - row_seq * t_new).astype(F32)

    for hh in range(n_heads):
        h = first_head + hh
        log_g = log_g_ref[h]
        g_chunk = g_chunk_ref[h]
        qk_cols = slice(hh * RET_DK, (hh + 1) * RET_DK)
        v_cols = slice(hh * RET_DV, (hh + 1) * RET_DV)
        q = _rotate(q_ref[:, qk_cols], cos2, sin2)
        k = _rotate(k_ref[:, qk_cols], cos2, sin2) * _RET_K_SCALE
        vb = v_ref[:, v_cols].astype(BF16)

        decay = jnp.where(same_chunk_causal, jnp.exp(diff * log_g), 0.0)
        scores = lax.dot_general(q.astype(BF16), k.astype(BF16), (((1,), (1,)), ((), ())),
                                 preferred_element_type=F32) * decay
        y = jnp.dot(scores.astype(BF16), vb, preferred_element_type=F32)

        qw = q * jnp.exp((t + 1.0) * log_g)
        y_cross = [jnp.dot(qw[n * t_new:(n + 1) * t_new].astype(BF16), s_in_ref[n, hh].astype(BF16),
                           preferred_element_type=F32) for n in range(n_seq)]
        y = y + jnp.concatenate(y_cross, axis=0)

        kt = (k * jnp.exp((t_new - 1.0 - t) * log_g)).T
        lhs = jnp.concatenate([jnp.where(col_seq == n, kt, 0.0) for n in range(n_seq)], axis=0)
        kv = jnp.dot(lhs.astype(BF16), vb, preferred_element_type=F32)
        for n in range(n_seq):
            s_out_ref[n, hh] = g_chunk * s_in_ref[n, hh] + kv[n * RET_DK:(n + 1) * RET_DK]

        o_ref[:, v_cols] = _silu(gr_ref[:, v_cols]) * _rms(y)


def _merge_kernel(st, x_ref, gate_ref, npost_ref, oa_ref, or_ref, ga_ref, gr_ref,
                  wpa_ref, wpr_ref, wo_ref, o_ref, merged_ref):
    i = pl.program_id(0)
    oa = oa_ref[...].astype(BF16)
    orr = or_ref[...].astype(BF16)
    for c in range(D_MODEL // COL_TILE):
        cols = slice(c * COL_TILE, (c + 1) * COL_TILE)
        a = jnp.dot(oa, wpa_ref[:, cols], preferred_element_type=F32)
        r = jnp.dot(orr, wpr_ref[:, cols], preferred_element_type=F32)
        merged = (jax.nn.sigmoid(ga_ref[:, cols].astype(F32)) * a
                  + jax.nn.sigmoid(gr_ref[:, cols].astype(F32)) * r)
        merged_ref[:, cols] = merged.astype(BF16)
    o_ref[...] = jnp.dot(merged_ref[...], wo_ref[...], preferred_element_type=F32)
    _post_residual(st, i, x_ref, gate_ref, npost_ref[1:2, :], o_ref, 1.0)


def _merge(st, x, mod, npost, o_a, o_r, wide, w_pa, w_pr, w_o):
    gate_a_block = 2 * RET_V // D_MODEL
    row_spec = pl.BlockSpec((st.tile, D_MODEL), lambda i, j: (i, 0))
    resident = lambda shape: pl.BlockSpec(shape, lambda i, j: (0, 0), pipeline_mode=pl.Buffered(1))
    return pl.pallas_call(
        functools.partial(_merge_kernel, st),
        out_shape=jax.ShapeDtypeStruct((st.n_tokens, D_MODEL), F32),
        grid=(st.n_tiles, 1),
        in_specs=[
            row_spec,
            _mod_spec(st, 5),
            pl.BlockSpec((N_SUBLAYERS, D_MODEL), lambda i, j: (0, 0)),
            pl.BlockSpec((st.tile, ATT_Q), lambda i, j: (i, 0)),
            pl.BlockSpec((st.tile, RET_V), lambda i, j: (i, 0)),
            pl.BlockSpec((st.tile, D_MODEL), lambda i, j: (i, gate_a_block)),
            pl.BlockSpec((st.tile, D_MODEL), lambda i, j: (i, gate_a_block + 1)),
            resident((ATT_Q, D_MODEL)), resident((RET_V, D_MODEL)), resident((D_MODEL, D_MODEL)),
        ],
        out_specs=row_spec,
        scratch_shapes=[pltpu.VMEM((st.tile, D_MODEL), BF16)],
        compiler_params=pltpu.CompilerParams(
            dimension_semantics=("parallel", "arbitrary"), vmem_limit_bytes=V7X_VMEM_LIMIT_BYTES),
        name="merge_out",
    )(x, mod, npost, o_a, o_r, wide, wide, w_pa, w_pr, w_o)


def _rotation_tables(pos):
    half = RET_DK // 2
    inv_freq = ROPE_BASE ** (-jnp.linspace(0.0, 1.0, half, dtype=F32))
    ang = pos[:, None] * inv_freq[None, :]
    cos, sin = jnp.cos(ang), jnp.sin(ang)
    return jnp.concatenate([cos, cos], axis=-1), jnp.concatenate([-sin, sin], axis=-1)


def kernel(x_prompt, x_sample, cache_k_win, cache_v_win, state_ret, c_prompt, c_sample, w_ada, b_ada,
           norm_pre, norm_post, w_in, attn_sinks, w_pa, w_pr, w_o,
           ffn1_gate, ffn1_up, ffn1_down, ffn2_gate, ffn2_up, ffn2_down):
    batch, seq, _ = x_prompt.shape
    n_seq, t_new, _ = x_sample.shape
    assert w_ada.shape[0] == 1, "single-layer step"
    assert t_new == V7X_SUBLANES and seq % TOKEN_TILE == 0 and (n_seq * t_new) % TOKEN_TILE == 0
    assert seq % MERGE_TOKEN_TILE == 0 and (n_seq * t_new) % MERGE_TOKEN_TILE == 0
    assert batch <= MOD_PAD_ROWS and n_seq % MOD_PAD_ROWS == 0

    c_all = jnp.concatenate(
        [c_sample, c_prompt, jnp.zeros((MOD_PAD_ROWS - batch, D_MODEL), F32)], axis=0)
    mod = _ada(c_all, w_ada[0], b_ada[0])

    def prompt_stream(tile):
        return _Stream(batch * seq, tile, seq, MOD_PAD_ROWS, n_seq // MOD_PAD_ROWS)

    def sample_stream(tile):
        return _Stream(n_seq * t_new, tile, t_new, tile // t_new, 0)

    prompt, prompt_m = prompt_stream(TOKEN_TILE), prompt_stream(MERGE_TOKEN_TILE)
    sample, sample_m = sample_stream(TOKEN_TILE), sample_stream(MERGE_TOKEN_TILE // 2)

    npre, npost = norm_pre[0], norm_post[0]
    sinks = attn_sinks[0]

    xs = x_sample.reshape(n_seq * t_new, D_MODEL)
    xs, f1g, f1u, f1d = _ffn(sample, 0, xs, mod, npre, npost,
                             ffn1_gate[0], ffn1_up[0], ffn1_down[0], emit_bf16=True)
    qa_s, kva_s, qkr_s, wide_s, w_in_b = _proj(sample, xs, mod, npre, w_in[0], F32, emit_bf16=True)

    xp = x_prompt.reshape(batch * seq, D_MODEL)
    xp = _ffn(prompt, 0, xp, mod, npre, npost, f1g, f1u, f1d)
    cos_s, sin_s = _rotation_tables(jnp.arange(t_new, dtype=F32) + PAST_LEN)
    qa, kva, qkr, wide, o_r_s, state_s = _proj(
        prompt, xp, mod, npre, w_in_b, BF16, side=(qkr_s, wide_s, cos_s, sin_s, state_ret[0], t_new))
    o_a, w_pa_b, w_pr_b, w_o_b = _attn_prompt(qa, kva, sinks, batch, seq, (w_pa[0], w_pr[0], w_o[0]))
    cos_p, sin_p = _rotation_tables(jnp.arange(seq, dtype=F32))
    o_r, state_p = _ret_prompt(qkr, wide, cos_p, sin_p, batch, seq)
    xp = _merge(prompt_m, xp, mod, npost, o_a, o_r, wide, w_pa_b, w_pr_b, w_o_b)
    kva_p = kva.reshape(batch, seq, 2 * ATT_KV)[:, seq - WINDOW:]
    kv_shape = (1, batch, WINDOW, ATT_KV_HEADS, ATT_HEAD_DIM)
    k_win_p = kva_p[..., :ATT_KV].reshape(kv_shape)
    v_win_p = kva_p[..., ATT_KV:].reshape(kv_shape)

    sink_rows = jnp.repeat(sinks, t_new)[:, None]
    o_a_s, k_s, v_s = _attn_sample(
        qa_s, kva_s,
        cache_k_win[0].reshape(n_seq, WINDOW, ATT_KV), cache_v_win[0].reshape(n_seq, WINDOW, ATT_KV),
        sink_rows, n_seq, t_new)
    xs = _merge(sample_m, xs, mod, npost, o_a_s, o_r_s, wide_s, w_pa_b, w_pr_b, w_o_b)
    xs, f2g, f2u, f2d = _ffn(sample, 2, xs, mod, npre, npost,
                             ffn2_gate[0], ffn2_up[0], ffn2_down[0], emit_bf16=True)
    xp = _ffn(prompt, 2, xp, mod, npre, npost, f2g, f2u, f2d)
    kvs_shape = (1, n_seq, WINDOW, ATT_KV_HEADS, ATT_HEAD_DIM)

    return (xp.reshape(batch, seq, D_MODEL), xs.reshape(n_seq, t_new, D_MODEL),
            k_win_p, v_win_p, state_p[None],
            k_s.reshape(kvs_shape), v_s.reshape(kvs_shape), state_s[None])
```

```python
import functools
import math
from typing import NamedTuple

import jax
import jax.numpy as jnp
from jax import lax
from jax.experimental import pallas as pl
from jax.experimental.pallas import tpu as pltpu

F32 = jnp.float32
BF16 = jnp.bfloat16

D_MODEL = 2048
WINDOW = 128
ATT_HEADS = 16
ATT_KV_HEADS = 4
ATT_HEAD_DIM = 64
ATT_GROUP = ATT_HEADS // ATT_KV_HEADS
ATT_Q = ATT_HEADS * ATT_HEAD_DIM
ATT_KV = ATT_KV_HEADS * ATT_HEAD_DIM
RET_HEADS = 8
RET_DK = 128
RET_DV = 256
RET_CHUNK = 128
RET_QK = RET_HEADS * RET_DK
RET_V = RET_HEADS * RET_DV
ROPE_BASE = 10000.0
D_FF = 5632
NORM_EPS = 1e-6
N_SUBLAYERS = 3
PAST_LEN = 16384
D_IN = ATT_Q + 2 * ATT_KV + 2 * RET_QK + 2 * RET_V + 2 * D_MODEL

V7X_SUBLANES = 8
V7X_BF16_ROWS = 16
NORM_GROUP_ROWS = 128
V7X_VMEM_LIMIT_BYTES = 60 * 1024 * 1024

TOKEN_TILE = 1024
MERGE_TOKEN_TILE = 512
FF_SUBTILE = 256
FF_SUBTILES_BF16 = 2
F32_WEIGHT_K_PARTS = 4
COL_TILE = 512
ADA_COL_TILE = 1024
ATTN_BLOCKS_PER_STEP = 2
RET_CHUNKS_PER_STEP = 4
SAMPLE_SEQS_PER_STEP = 8
RET_SAMPLE_SEQS = 16
MOD_PAD_ROWS = 8

_QA_TILES = ATT_Q // COL_TILE
_KVA_TILES = 2 * ATT_KV // COL_TILE
_QKR_TILES = 2 * RET_QK // COL_TILE
_WIDE_COLS = 2 * RET_V + 2 * D_MODEL
_WIDE_TILES = _WIDE_COLS // COL_TILE
_QKR_START = _QA_TILES + _KVA_TILES
_WIDE_START = _QKR_START + _QKR_TILES
assert _WIDE_START + _WIDE_TILES == D_IN // COL_TILE and _KVA_TILES == 1

_ALIBI_SLOPES = [2.0 ** (-8.0 * (h + 1) / ATT_HEADS) for h in range(ATT_HEADS)]
_RET_LOG_G = [math.log(1.0 - 2.0 ** (-5.0 - h)) for h in range(RET_HEADS)]
_ATT_SCALE = ATT_HEAD_DIM ** -0.5
_RET_K_SCALE = RET_DK ** -0.5


class _Stream(NamedTuple):
    n_tokens: int
    tile: int
    rows_per_mod: int
    mod_block_rows: int
    mod_block_base: int

    @property
    def n_tiles(self):
        return self.n_tokens // self.tile

    @property
    def sub_rows(self):
        return min(self.rows_per_mod, self.tile)

    @property
    def group_rows(self):
        return max(self.sub_rows, NORM_GROUP_ROWS)

    @property
    def n_groups(self):
        return self.tile // self.group_rows

    @property
    def mods_per_group(self):
        return self.group_rows // self.sub_rows


def _rms(x):
    return x * lax.rsqrt(jnp.mean(x * x, axis=-1, keepdims=True) + NORM_EPS)


def _silu(x):
    return x * jax.nn.sigmoid(x)


def _for_groups(n_groups, fn):
    if n_groups == 1:
        fn(0)
    else:
        def body(g, carry):
            fn(g)
            return carry
        lax.fori_loop(0, n_groups, body, 0)


def _group_base(st, g):
    return 0 if st.n_groups == 1 else pl.multiple_of(g * st.group_rows, st.group_rows)


def _mod_row(st, i, g, s):
    if st.rows_per_mod >= st.tile:
        return (i * st.tile) // st.rows_per_mod
    return g * st.mods_per_group + s


def _pre_norm(st, i, x_ref, shift_ref, scale_ref, gain, h_ref):
    def group(g):
        base = _group_base(st, g)
        parts = []
        for s in range(st.mods_per_group):
            rows = pl.ds(base + s * st.sub_rows, st.sub_rows)
            m = _mod_row(st, i, g, s)
            x = x_ref[rows, :]
            row_gain = gain * (1.0 + scale_ref[pl.ds(m, 1), :])
            sh = shift_ref[pl.ds(m, 1), :]
            parts.append(_rms(x) * row_gain + sh)
        h = parts[0] if len(parts) == 1 else jnp.concatenate(parts, axis=0)
        h_ref[pl.ds(base, st.group_rows), :] = h.astype(BF16)
    _for_groups(st.n_groups, group)


def _post_residual(st, i, x_ref, gate_ref, gain, o_ref, coeff):
    def group(g):
        base = _group_base(st, g)
        results = []
        for s in range(st.mods_per_group):
            rows = pl.ds(base + s * st.sub_rows, st.sub_rows)
            m = _mod_row(st, i, g, s)
            row_gain = gate_ref[pl.ds(m, 1), :] * gain
            if coeff != 1.0:
                row_gain = coeff * row_gain
            results.append((rows, x_ref[rows, :] + _rms(o_ref[rows, :]) * row_gain))
        for rows, value in results:
            o_ref[rows, :] = value
    _for_groups(st.n_groups, group)


def _ada_kernel(c_ref, w_ref, b_ref, o_ref):
    a = _silu(c_ref[...]).astype(BF16)
    o_ref[0] = jnp.dot(a, w_ref[...].astype(BF16), preferred_element_type=F32) + b_ref[...]


def _ada(c_all, w_ada, b_ada):
    rows = c_all.shape[0]
    n_vec = N_SUBLAYERS * 3
    per_vec = D_MODEL // ADA_COL_TILE
    return pl.pallas_call(
        _ada_kernel,
        out_shape=jax.ShapeDtypeStruct((n_vec, rows, D_MODEL), F32),
        grid=(n_vec * per_vec,),
        in_specs=[
            pl.BlockSpec((rows, D_MODEL), lambda j: (0, 0)),
            pl.BlockSpec((D_MODEL, ADA_COL_TILE), lambda j: (0, j)),
            pl.BlockSpec((1, ADA_COL_TILE), lambda j: (0, j)),
        ],
        out_specs=pl.BlockSpec((1, rows, ADA_COL_TILE), lambda j: (j // per_vec, 0, j % per_vec)),
        compiler_params=pltpu.CompilerParams(
            dimension_semantics=("arbitrary",), vmem_limit_bytes=V7X_VMEM_LIMIT_BYTES),
        name="ada_mod",
    )(c_all, w_ada, b_ada.reshape(1, -1))


def _mod_spec(st, vec):
    if st.rows_per_mod >= st.tile:
        index = lambda i, j: (vec, st.mod_block_base, 0)
    else:
        index = lambda i, j: (vec, st.mod_block_base + i, 0)
    return pl.BlockSpec((None, st.mod_block_rows, D_MODEL), index)


def _ffn_kernel(st, sub, k_parts, emit_bf16, x_ref, shift_ref, scale_ref, gate_ref, npre_ref, npost_ref,
                *refs):
    i = pl.program_id(0)
    j = pl.program_id(1)
    n_up = max(k_parts, 1)
    wg_refs, wu_refs, wd_ref, o_ref = refs[:n_up], refs[n_up:2 * n_up], refs[2 * n_up], refs[2 * n_up + 1]
    rest = refs[2 * n_up + 2:]
    h_ref = rest[-1]

    def up_f32(h, w_refs, out_ref):
        rows = D_MODEL // k_parts
        acc = None
        for p, w_ref in enumerate(w_refs):
            w = w_ref[...].astype(BF16)
            if out_ref is not None:
                out_ref[p * rows:(p + 1) * rows, :] = w
            part = jnp.dot(h[:, p * rows:(p + 1) * rows], w, preferred_element_type=F32)
            acc = part if acc is None else acc + part
        return acc

    def partial_down():
        h = h_ref[...]
        if k_parts:
            wg_out_ref, wu_out_ref, wd_out_ref = rest[:3] if emit_bf16 else (None, None, None)
            wd = wd_ref[...].astype(BF16)
            if emit_bf16:
                wd_out_ref[...] = wd
            g = up_f32(h, wg_refs, wg_out_ref)
            u = up_f32(h, wu_refs, wu_out_ref)
            return jnp.dot((_silu(g) * u).astype(BF16), wd, preferred_element_type=F32)
        acts = []
        for t in range(wg_refs[0].shape[0]):
            g = jnp.dot(h, wg_refs[0][t], preferred_element_type=F32)
            u = jnp.dot(h, wu_refs[0][t], preferred_element_type=F32)
            acts.append((_silu(g) * u).astype(BF16))
        return jnp.dot(jnp.concatenate(acts, axis=1), wd_ref[...], preferred_element_type=F32)

    @pl.when(j == 0)
    def _():
        _pre_norm(st, i, x_ref, shift_ref, scale_ref, npre_ref[sub:sub + 1, :], h_ref)
        o_ref[...] = partial_down()

    last = pl.num_programs(1) - 1

    @pl.when((j > 0) & (j < last))
    def _():
        o_ref[...] += partial_down()

    @pl.when(j == last)
    def _():
        o_ref[...] += partial_down()
        _post_residual(st, i, x_ref, gate_ref, npost_ref[sub:sub + 1, :], o_ref, 0.5)


def _ffn(st, sub, x, mod, npre, npost, wg, wu, wd, emit_bf16=False):
    tiled = wg.ndim == 3
    n_sub = FF_SUBTILES_BF16 if tiled else 1
    ff_tile = n_sub * FF_SUBTILE
    row_spec = pl.BlockSpec((st.tile, D_MODEL), lambda i, j: (i, 0))
    full_spec = pl.BlockSpec((N_SUBLAYERS, D_MODEL), lambda i, j: (0, 0))
    tile_spec = pl.BlockSpec((None, D_MODEL, FF_SUBTILE), lambda i, j: (j, 0, 0))
    if tiled:
        k_parts = 0
        up_specs = [pl.BlockSpec((n_sub, D_MODEL, FF_SUBTILE), lambda i, j: (j, 0, 0))]
        up_args = lambda w: [w]
    else:
        k_parts = F32_WEIGHT_K_PARTS
        rows = D_MODEL // k_parts
        up_specs = [pl.BlockSpec((rows, ff_tile), functools.partial(lambda p, i, j: (p, j), p))
                    for p in range(k_parts)]
        up_args = lambda w: [w] * k_parts
    down_spec = pl.BlockSpec((ff_tile, D_MODEL), lambda i, j: (j, 0))
    out_shape = [jax.ShapeDtypeStruct((st.n_tokens, D_MODEL), F32)]
    out_specs = [row_spec]
    if emit_bf16:
        assert st.n_tiles == 1 and not tiled
        up_shape = jax.ShapeDtypeStruct((D_FF // FF_SUBTILE, D_MODEL, FF_SUBTILE), BF16)
        out_shape += [up_shape, up_shape, jax.ShapeDtypeStruct((D_FF, D_MODEL), BF16)]
        out_specs += [tile_spec, tile_spec, down_spec]
    outs = pl.pallas_call(
        functools.partial(_ffn_kernel, st, sub, k_parts, emit_bf16),
        out_shape=out_shape,
        grid=(st.n_tiles, D_FF // ff_tile),
        in_specs=[
            row_spec,
            _mod_spec(st, 3 * sub + 0), _mod_spec(st, 3 * sub + 1), _mod_spec(st, 3 * sub + 2),
            full_spec, full_spec,
            *up_specs, *up_specs, down_spec,
        ],
        out_specs=out_specs,
        scratch_shapes=[pltpu.VMEM((st.tile, D_MODEL), BF16)],
        compiler_params=pltpu.CompilerParams(
            dimension_semantics=("parallel", "arbitrary"), vmem_limit_bytes=V7X_VMEM_LIMIT_BYTES),
        name=f"ffn{sub}",
    )(x, mod, mod, mod, npre, npost, *up_args(wg), *up_args(wu), wd)
    return outs if emit_bf16 else outs[0]


_N_SIDE_IN = 9
_N_SIDE_OUT = 2


def _proj_kernel(st, k_parts, emit_bf16, with_wide, x_ref, shift_ref, scale_ref, npre_ref, *refs):
    i = pl.program_id(0)
    j = pl.program_id(1)
    w_refs = refs[:k_parts]
    refs = refs[k_parts:]
    n_main = 4 if with_wide else 3
    qa_ref, kva_ref, qkr_ref = refs[:3]
    wide_ref = refs[3] if with_wide else None
    rest = refs[n_main:]
    h_ref = rest[-1]
    rows = D_MODEL // k_parts

    def project(dst_ref):
        h = h_ref[...]
        acc = None
        for p, w_ref in enumerate(w_refs):
            w = w_ref[...].astype(BF16)
            if emit_bf16:
                rest[0][p * rows:(p + 1) * rows, :] = w
            part = jnp.dot(h[:, p * rows:(p + 1) * rows], w, preferred_element_type=F32)
            acc = part if acc is None else acc + part
        dst_ref[...] = acc.astype(dst_ref.dtype)

    @pl.when(j == 0)
    def _():
        _pre_norm(st, i, x_ref, shift_ref, scale_ref, npre_ref[1:2, :], h_ref)
        project(qa_ref)

    @pl.when((j > 0) & (j < _QA_TILES))
    def _():
        project(qa_ref)

    @pl.when(j == _QA_TILES)
    def _():
        project(kva_ref)

    @pl.when((j >= _QKR_START) & (j < _WIDE_START))
    def _():
        project(qkr_ref)

    if with_wide:
        @pl.when(j >= _WIDE_START)
        def _():
            project(wide_ref)


def _proj_wide_kernel(st, side_t_new, x_ref, shift_ref, scale_ref, npre_ref, w_lo_ref, w_hi_ref, *refs):
    i = pl.program_id(0)
    j = pl.program_id(1)
    side_in, wide_ref = refs[:_N_SIDE_IN], refs[_N_SIDE_IN]
    side_out, h_ref = refs[_N_SIDE_IN + 1:_N_SIDE_IN + 1 + _N_SIDE_OUT], refs[-1]

    def step():
        h = h_ref[...]
        y = jnp.concatenate([jnp.dot(h, w_lo_ref[...], preferred_element_type=F32),
                             jnp.dot(h, w_hi_ref[...], preferred_element_type=F32)], axis=1)
        wide_ref[...] = y.astype(wide_ref.dtype)
        log_g_ref, g_chunk_ref, *side_vmem = side_in
        _ret_sample_heads(side_t_new, j, 1, log_g_ref, g_chunk_ref, *side_vmem, *side_out)

    @pl.when(j == 0)
    def _():
        _pre_norm(st, i, x_ref, shift_ref, scale_ref, npre_ref[1:2, :], h_ref)
        step()

    @pl.when(j > 0)
    def _():
        step()


def _proj_wide(st, x, mod, npre, w_tiles, narrow_dtype, side):
    qkr_s, wide_s, cos2, sin2, state, t_new = side
    nb = RET_SAMPLE_SEQS
    n_rows = nb * t_new
    per = 2
    n_steps = _WIDE_TILES // per
    assert n_rows == RET_DK and state.shape[0] == st.n_tiles * nb and n_steps == RET_HEADS
    row_spec = pl.BlockSpec((st.tile, D_MODEL), lambda i, j: (i, 0))
    smem = pl.BlockSpec(memory_space=pltpu.SMEM)
    table_spec = pl.BlockSpec((n_rows, RET_DK), lambda i, j: (0, 0))
    state_spec = pl.BlockSpec((nb, 1, RET_DK, RET_DV), lambda i, j: (i, j, 0, 0))
    return pl.pallas_call(
        functools.partial(_proj_wide_kernel, st, t_new),
        out_shape=[jax.ShapeDtypeStruct((st.n_tokens, _WIDE_COLS), narrow_dtype),
                   jax.ShapeDtypeStruct((state.shape[0] * t_new, RET_V), F32),
                   jax.ShapeDtypeStruct(state.shape, F32)],
        grid=(st.n_tiles, n_steps),
        in_specs=[
            row_spec,
            _mod_spec(st, 3), _mod_spec(st, 4),
            pl.BlockSpec((N_SUBLAYERS, D_MODEL), lambda i, j: (0, 0)),
            pl.BlockSpec((None, D_MODEL, COL_TILE), lambda i, j: (_WIDE_START + per * j, 0, 0)),
            pl.BlockSpec((None, D_MODEL, COL_TILE), lambda i, j: (_WIDE_START + per * j + 1, 0, 0)),
            smem, smem,
            pl.BlockSpec((n_rows, RET_DK), lambda i, j: (i, j)),
            pl.BlockSpec((n_rows, RET_DK), lambda i, j: (i, RET_HEADS + j)),
            pl.BlockSpec((n_rows, RET_DV), lambda i, j: (i, j)),
            pl.BlockSpec((n_rows, RET_DV), lambda i, j: (i, RET_HEADS + j)),
            table_spec, table_spec,
            state_spec,
        ],
        out_specs=[pl.BlockSpec((st.tile, per * COL_TILE), lambda i, j: (i, j)),
                   pl.BlockSpec((n_rows, RET_DV), lambda i, j: (i, j)), state_spec],
        scratch_shapes=[pltpu.VMEM((st.tile, D_MODEL), BF16)],
        compiler_params=pltpu.CompilerParams(
            dimension_semantics=("parallel", "arbitrary"), vmem_limit_bytes=V7X_VMEM_LIMIT_BYTES),
        name="in_proj_wide",
    )(x, mod, mod, npre, w_tiles, w_tiles,
      jnp.asarray(_RET_LOG_G, F32), jnp.asarray([math.exp(t_new * g) for g in _RET_LOG_G], F32),
      qkr_s, qkr_s, wide_s, wide_s, jnp.tile(cos2, (nb, 1)), jnp.tile(sin2, (nb, 1)), state)


def _proj(st, x, mod, npre, w_in, narrow_dtype, emit_bf16=False, with_wide=True):
    n = st.n_tokens
    tiled = w_in.ndim == 3
    row_spec = pl.BlockSpec((st.tile, D_MODEL), lambda i, j: (i, 0))
    out_block = (st.tile, COL_TILE)
    tile_spec = pl.BlockSpec((None, D_MODEL, COL_TILE), lambda i, j: (j, 0, 0))
    out_shape = [
        jax.ShapeDtypeStruct((n, ATT_Q), narrow_dtype),
        jax.ShapeDtypeStruct((n, 2 * ATT_KV), F32),
        jax.ShapeDtypeStruct((n, 2 * RET_QK), F32),
        jax.ShapeDtypeStruct((n, _WIDE_COLS), narrow_dtype),
    ]
    out_specs = [
        pl.BlockSpec(out_block, lambda i, j: (i, jnp.minimum(j, _QA_TILES - 1))),
        pl.BlockSpec(out_block, lambda i, j: (i, 0)),
        pl.BlockSpec(out_block, lambda i, j: (i, jnp.clip(j - _QKR_START, 0, _QKR_TILES - 1))),
        pl.BlockSpec(out_block, lambda i, j: (i, jnp.maximum(j - _WIDE_START, 0))),
    ]
    if not with_wide:
        assert not emit_bf16
        out_shape, out_specs = out_shape[:3], out_specs[:3]
    if emit_bf16:
        assert st.n_tiles == 1 and not tiled
        out_shape.append(jax.ShapeDtypeStruct((D_IN // COL_TILE, D_MODEL, COL_TILE), BF16))
        out_specs.append(tile_spec)
    if tiled:
        k_parts = 1
        w_specs = [tile_spec]
    else:
        k_parts = F32_WEIGHT_K_PARTS
        w_specs = [pl.BlockSpec((D_MODEL // k_parts, COL_TILE), functools.partial(lambda p, i, j: (p, j), p))
                   for p in range(k_parts)]
    return pl.pallas_call(
        functools.partial(_proj_kernel, st, k_parts, emit_bf16, with_wide),
        out_shape=out_shape,
        grid=(st.n_tiles, D_IN // COL_TILE if with_wide else _WIDE_START),
        in_specs=[
            row_spec,
            _mod_spec(st, 3), _mod_spec(st, 4),
            pl.BlockSpec((N_SUBLAYERS, D_MODEL), lambda i, j: (0, 0)),
            *w_specs,
        ],
        out_specs=out_specs,
        scratch_shapes=[pltpu.VMEM((st.tile, D_MODEL), BF16)],
        compiler_params=pltpu.CompilerParams(
            dimension_semantics=("parallel", "arbitrary"), vmem_limit_bytes=V7X_VMEM_LIMIT_BYTES),
        name="in_proj",
    )(x, mod, mod, npre, *([w_in] * k_parts))


def _attention(q, k2, v2, sinks_ref, first_valid_key):
    tq = q.shape[0]
    a_idx = lax.broadcasted_iota(jnp.int32, (tq, 2 * WINDOW), 0)
    b_idx = lax.broadcasted_iota(jnp.int32, (tq, 2 * WINDOW), 1)
    dist = WINDOW + a_idx - b_idx
    mask = (dist >= 0) & (dist <= WINDOW) & (b_idx >= first_valid_key)
    dist_f = jnp.where(mask, dist.astype(F32), jnp.inf)
    q = q * _ATT_SCALE
    outs = []
    for kv in range(ATT_KV_HEADS):
        cols = slice(kv * ATT_HEAD_DIM, (kv + 1) * ATT_HEAD_DIM)
        kk = k2[:, cols]
        vv = v2[:, cols]
        heads = range(kv * ATT_GROUP, (kv + 1) * ATT_GROUP)
        qg = jnp.concatenate(
            [q[:, h * ATT_HEAD_DIM:(h + 1) * ATT_HEAD_DIM] for h in heads], axis=0).astype(BF16)
        s_all = lax.dot_general(qg, kk, (((1,), (1,)), ((), ())), preferred_element_type=F32)
        probs = []
        for g, h in enumerate(heads):
            s = s_all[g * tq:(g + 1) * tq] - _ALIBI_SLOPES[h] * dist_f
            sink = sinks_ref[h]
            m = jnp.maximum(jnp.max(s, axis=-1, keepdims=True), sink)
            p = jnp.exp(s - m)
            inv = 1.0 / (jnp.sum(p, axis=-1, keepdims=True) + jnp.exp(sink - m))
            probs.append((p * inv).astype(BF16))
        o_all = jnp.dot(jnp.concatenate(probs, axis=0), vv, preferred_element_type=F32)
        outs.extend(o_all[g * tq:(g + 1) * tq] for g in range(ATT_GROUP))
    return jnp.concatenate(outs, axis=-1)


def _attn_prompt_kernel(n_cast, sinks_ref, q_ref, kc_ref, vc_ref, kp_ref, vp_ref, *rest):
    cast_in, o_ref, cast_out = rest[:n_cast], rest[n_cast], rest[n_cast + 1:]
    for src_ref, dst_ref in zip(cast_in, cast_out):
        dst_ref[...] = src_ref[...].astype(BF16)
    step = pl.program_id(1)
    k_tiles = [kp_ref[...]] + [kc_ref[r * WINDOW:(r + 1) * WINDOW, :] for r in range(ATTN_BLOCKS_PER_STEP)]
    v_tiles = [vp_ref[...]] + [vc_ref[r * WINDOW:(r + 1) * WINDOW, :] for r in range(ATTN_BLOCKS_PER_STEP)]
    for r in range(ATTN_BLOCKS_PER_STEP):
        rows = slice(r * WINDOW, (r + 1) * WINDOW)
        k2 = jnp.concatenate(k_tiles[r:r + 2], axis=0).astype(BF16)
        v2 = jnp.concatenate(v_tiles[r:r + 2], axis=0).astype(BF16)
        first_valid = jnp.where(step == 0, WINDOW, 0) if r == 0 else 0
        o_ref[rows, :] = _attention(q_ref[rows, :], k2, v2, sinks_ref, first_valid).astype(o_ref.dtype)


def _attn_prompt(qa, kva, sinks, batch, seq, cast_weights):
    per = ATTN_BLOCKS_PER_STEP
    ns = seq // (per * WINDOW)
    n_steps = batch * ns
    cur = lambda col: (lambda b, i: (b * ns + i, col))
    prev = lambda col: (lambda b, i: (b * ns * per + jnp.maximum(i * per - 1, 0), col))
    cast_specs = []
    for w in cast_weights:
        slab = w.shape[0] // n_steps
        assert slab * n_steps == w.shape[0] and slab % V7X_BF16_ROWS == 0
        cast_specs.append(pl.BlockSpec((slab, w.shape[1]), lambda b, i: (b * ns + i, 0)))
    return pl.pallas_call(
        functools.partial(_attn_prompt_kernel, len(cast_weights)),
        out_shape=[jax.ShapeDtypeStruct((batch * seq, ATT_Q), BF16)]
        + [jax.ShapeDtypeStruct(w.shape, BF16) for w in cast_weights],
        grid=(batch, ns),
        in_specs=[
            pl.BlockSpec(memory_space=pltpu.SMEM),
            pl.BlockSpec((per * WINDOW, ATT_Q), lambda b, i: (b * ns + i, 0)),
            pl.BlockSpec((per * WINDOW, ATT_KV), cur(0)),
            pl.BlockSpec((per * WINDOW, ATT_KV), cur(1)),
            pl.BlockSpec((WINDOW, ATT_KV), prev(0)),
            pl.BlockSpec((WINDOW, ATT_KV), prev(1)),
            *cast_specs,
        ],
        out_specs=[pl.BlockSpec((per * WINDOW, ATT_Q), lambda b, i: (b * ns + i, 0)), *cast_specs],
        compiler_params=pltpu.CompilerParams(
            dimension_semantics=("parallel", "arbitrary"), vmem_limit_bytes=V7X_VMEM_LIMIT_BYTES),
        name="attn_prompt",
    )(sinks, qa, kva, kva, kva, kva, *cast_weights)


def _rotate(x, cos2, sin2):
    return x * cos2 + pltpu.roll(x, RET_DK // 2, axis=1) * sin2


def _ret_prompt_kernel(q_ref, k_ref, v_ref, gr_ref, cos_ref, sin_ref, o_ref, s_ref,
                       decay_ref, qw_ref, kw_ref):
    @pl.when(pl.program_id(1) == 0)
    def _():
        s_ref[...] = jnp.zeros_like(s_ref)
        row = lax.broadcasted_iota(jnp.int32, (RET_CHUNK, RET_CHUNK), 0).astype(F32)
        col = lax.broadcasted_iota(jnp.int32, (RET_CHUNK, RET_CHUNK), 1).astype(F32)
        diff = row - col
        for h in range(RET_HEADS):
            log_g = _RET_LOG_G[h]
            decay_ref[h] = jnp.where(diff >= 0, jnp.exp(jnp.maximum(diff, 0.0) * log_g), 0.0)
            qw_ref[h] = jnp.exp((row + 1.0) * log_g)
            kw_ref[h] = jnp.exp((RET_CHUNK - 1.0 - row) * log_g)

    for r in range(RET_CHUNKS_PER_STEP):
        rows = slice(r * RET_CHUNK, (r + 1) * RET_CHUNK)
        cos2, sin2 = cos_ref[rows, :], sin_ref[rows, :]
        for h in range(RET_HEADS):
            qk_cols = slice(h * RET_DK, (h + 1) * RET_DK)
            v_cols = slice(h * RET_DV, (h + 1) * RET_DV)
            qh = _rotate(q_ref[rows, qk_cols], cos2, sin2)
            kh = _rotate(k_ref[rows, qk_cols], cos2, sin2) * _RET_K_SCALE
            vb = v_ref[rows, v_cols].astype(BF16)
            s_prev = s_ref[0, h]
            scores = lax.dot_general(qh.astype(BF16), kh.astype(BF16), (((1,), (1,)), ((), ())),
                                     preferred_element_type=F32) * decay_ref[h]
            y = jnp.dot(scores.astype(BF16), vb, preferred_element_type=F32)
            y = y + jnp.dot((qh * qw_ref[h]).astype(BF16), s_prev.astype(BF16),
                            preferred_element_type=F32)
            kt = (kh * kw_ref[h]).T.astype(BF16)
            s_ref[0, h] = (math.exp(RET_CHUNK * _RET_LOG_G[h]) * s_prev
                           + jnp.dot(kt, vb, preferred_element_type=F32))
            gate = gr_ref[rows, v_cols].astype(F32)
            o_ref[rows, v_cols] = (_silu(gate) * _rms(y)).astype(o_ref.dtype)


def _ret_prompt(qkr, wide, cos2, sin2, batch, seq):
    step_rows = RET_CHUNKS_PER_STEP * RET_CHUNK
    nc = seq // step_rows
    rows = lambda col: (lambda b, c: (b * nc + c, col))
    return pl.pallas_call(
        _ret_prompt_kernel,
        out_shape=(
            jax.ShapeDtypeStruct((batch * seq, RET_V), BF16),
            jax.ShapeDtypeStruct((batch, RET_HEADS, RET_DK, RET_DV), F32),
        ),
        grid=(batch, nc),
        in_specs=[
            pl.BlockSpec((step_rows, RET_QK), rows(0)),
            pl.BlockSpec((step_rows, RET_QK), rows(1)),
            pl.BlockSpec((step_rows, RET_V), rows(0)),
            pl.BlockSpec((step_rows, RET_V), rows(1)),
            pl.BlockSpec((step_rows, RET_DK), lambda b, c: (c, 0)),
            pl.BlockSpec((step_rows, RET_DK), lambda b, c: (c, 0)),
        ],
        out_specs=(
            pl.BlockSpec((step_rows, RET_V), rows(0)),
            pl.BlockSpec((1, RET_HEADS, RET_DK, RET_DV), lambda b, c: (b, 0, 0, 0)),
        ),
        scratch_shapes=[pltpu.VMEM((RET_HEADS, RET_CHUNK, RET_CHUNK), F32)] * 3,
        compiler_params=pltpu.CompilerParams(
            dimension_semantics=("parallel", "arbitrary"), vmem_limit_bytes=V7X_VMEM_LIMIT_BYTES),
        name="ret_prompt",
    )(qkr, qkr, wide, wide, cos2, sin2)


def _attn_sample_kernel(t_new, sink_ref, qa_ref, kn_ref, vn_ref, ck_ref, cv_ref,
                        oa_ref, ko_ref, vo_ref):
    n_rows = ATT_HEADS * t_new
    pair = 2 * ATT_HEAD_DIM
    row = lax.broadcasted_iota(jnp.int32, (n_rows, 2 * WINDOW), 0)
    key = lax.broadcasted_iota(jnp.int32, (n_rows, 2 * WINDOW), 1)
    head = row // t_new
    dist = WINDOW + (row - head * t_new) - key
    slope = jnp.exp2((head.astype(F32) + 1.0) * (-8.0 / ATT_HEADS))
    bias = jnp.where((dist >= 0) & (dist <= WINDOW), -slope * dist.astype(F32), -jnp.inf)
    sink = sink_ref[...]
    lower = lax.broadcasted_iota(jnp.int32, (t_new, pair), 1) < ATT_HEAD_DIM
    zero_group = jnp.zeros((t_new, pair), F32)
    zero_keys = jnp.zeros((WINDOW - t_new, ATT_KV), F32)

    for n in range(SAMPLE_SEQS_PER_STEP):
        rows = slice(n * t_new, (n + 1) * t_new)
        kc, vc = ck_ref[n], cv_ref[n]
        kn, vn = kn_ref[rows, :], vn_ref[rows, :]
        ko_ref[n, :WINDOW - t_new, :] = kc[t_new:]
        ko_ref[n, WINDOW - t_new:, :] = kn
        vo_ref[n, :WINDOW - t_new, :] = vc[t_new:]
        vo_ref[n, WINDOW - t_new:, :] = vn
        k2 = jnp.concatenate([kc, kn, zero_keys], axis=0).astype(BF16)
        v2 = jnp.concatenate([vc, vn, zero_keys], axis=0).astype(BF16)

        q = qa_ref[rows, :] * _ATT_SCALE
        q_swapped = pltpu.roll(q, ATT_HEAD_DIM, axis=1)
        blocks = []
        for h in range(ATT_HEADS):
            kv = h // ATT_GROUP
            want_lower = kv % 2 == 0
            if (h % 2 == 0) == want_lower:
                src = q[:, (h // 2) * pair:(h // 2 + 1) * pair]
            else:
                g = (h + 1) // 2 % (ATT_HEADS // 2)
                src = q_swapped[:, g * pair:(g + 1) * pair]
            piece = jnp.where(lower if want_lower else ~lower, src, 0.0)
            blocks.append(jnp.concatenate(
                [piece, zero_group] if kv // 2 == 0 else [zero_group, piece], axis=1))
        q_bd = jnp.concatenate(blocks, axis=0).astype(BF16)

        s = lax.dot_general(q_bd, k2, (((1,), (1,)), ((), ())), preferred_element_type=F32) + bias
        m = jnp.maximum(jnp.max(s, axis=-1, keepdims=True), sink)
        p = jnp.exp(s - m)
        inv = 1.0 / (jnp.sum(p, axis=-1, keepdims=True) + jnp.exp(sink - m))
        o = jnp.dot((p * inv).astype(BF16), v2, preferred_element_type=F32)

        outs = []
        for g in range(ATT_HEADS // 2):
            h0 = 2 * g
            kv = h0 // ATT_GROUP
            cols = slice((kv // 2) * pair, (kv // 2 + 1) * pair)
            a = o[h0 * t_new:(h0 + 1) * t_new, cols]
            b = o[(h0 + 1) * t_new:(h0 + 2) * t_new, cols]
            if kv % 2 == 0:
                b = pltpu.roll(b, ATT_HEAD_DIM, axis=1)
            else:
                a = pltpu.roll(a, ATT_HEAD_DIM, axis=1)
            outs.append(jnp.where(lower, a, b))
        oa_ref[rows, :] = jnp.concatenate(outs, axis=1)


def _attn_sample(qa, kva, cache_k, cache_v, sink_rows, n_seq, t_new):
    nb = SAMPLE_SEQS_PER_STEP
    rows = lambda col: (lambda n: (n, col))
    cache_spec = pl.BlockSpec((nb, WINDOW, ATT_KV), lambda n: (n, 0, 0))
    return pl.pallas_call(
        functools.partial(_attn_sample_kernel, t_new),
        out_shape=(
            jax.ShapeDtypeStruct((n_seq * t_new, ATT_Q), F32),
            jax.ShapeDtypeStruct(cache_k.shape, F32),
            jax.ShapeDtypeStruct(cache_v.shape, F32),
        ),
        grid=(n_seq // nb,),
        in_specs=[
            pl.BlockSpec((ATT_HEADS * t_new, 1), lambda n: (0, 0)),
            pl.BlockSpec((nb * t_new, ATT_Q), rows(0)),
            pl.BlockSpec((nb * t_new, ATT_KV), rows(0)),
            pl.BlockSpec((nb * t_new, ATT_KV), rows(1)),
            cache_spec, cache_spec,
        ],
        out_specs=(pl.BlockSpec((nb * t_new, ATT_Q), rows(0)), cache_spec, cache_spec),
        compiler_params=pltpu.CompilerParams(
            dimension_semantics=("parallel",), vmem_limit_bytes=V7X_VMEM_LIMIT_BYTES),
        name="attn_sample",
    )(sink_rows, qa, kva, kva, cache_k, cache_v)


def _ret_sample_heads(t_new, first_head, n_heads, log_g_ref, g_chunk_ref, q_ref, k_ref, v_ref, gr_ref,
                      cos_ref, sin_ref, s_in_ref, o_ref, s_out_ref):
    n_seq = s_in_ref.shape[0]
    n_rows = n_seq * t_new
    cos2, sin2 = cos_ref[...], sin_ref[...]
    row = lax.broadcasted_iota(jnp.int32, (n_rows, n_rows), 0)
    col = lax.broadcasted_iota(jnp.int32, (n_rows, n_rows), 1)
    row_seq, col_seq = row // t_new, col // t_new
    same_chunk_causal = (row_seq == col_seq) & (row >= col)
    diff = jnp.maximum((row - col).astype(F32), 0.0)
    t = (row - row_seq * t_new).astype(F32)

    for hh in range(n_heads):
        h = first_head + hh
        log_g = log_g_ref[h]
        g_chunk = g_chunk_ref[h]
        qk_cols = slice(hh * RET_DK, (hh + 1) * RET_DK)
        v_cols = slice(hh * RET_DV, (hh + 1) * RET_DV)
        q = _rotate(q_ref[:, qk_cols], cos2, sin2)
        k = _rotate(k_ref[:, qk_cols], cos2, sin2) * _RET_K_SCALE
        vb = v_ref[:, v_cols].astype(BF16)

        decay = jnp.where(same_chunk_causal, jnp.exp(diff * log_g), 0.0)
        scores = lax.dot_general(q.astype(BF16), k.astype(BF16), (((1,), (1,)), ((), ())),
                                 preferred_element_type=F32) * decay
        y = jnp.dot(scores.astype(BF16), vb, preferred_element_type=F32)

        qw = q * jnp.exp((t + 1.0) * log_g)
        y_cross = [jnp.dot(qw[n * t_new:(n + 1) * t_new].astype(BF16), s_in_ref[n, hh].astype(BF16),
                           preferred_element_type=F32) for n in range(n_seq)]
        y = y + jnp.concatenate(y_cross, axis=0)

        kt = (k * jnp.exp((t_new - 1.0 - t) * log_g)).T
        lhs = jnp.concatenate([jnp.where(col_seq == n, kt, 0.0) for n in range(n_seq)], axis=0)
        kv = jnp.dot(lhs.astype(BF16), vb, preferred_element_type=F32)
        for n in range(n_seq):
            s_out_ref[n, hh] = g_chunk * s_in_ref[n, hh] + kv[n * RET_DK:(n + 1) * RET_DK]

        o_ref[:, v_cols] = _silu(gr_ref[:, v_cols]) * _rms(y)


def _merge_kernel(st, x_ref, gate_ref, npost_ref, oa_ref, or_ref, ga_ref, gr_ref,
                  wpa_ref, wpr_ref, wo_ref, o_ref, merged_ref):
    i = pl.program_id(0)
    oa = oa_ref[...].astype(BF16)
    orr = or_ref[...].astype(BF16)
    for c in range(D_MODEL // COL_TILE):
        cols = slice(c * COL_TILE, (c + 1) * COL_TILE)
        a = jnp.dot(oa, wpa_ref[:, cols], preferred_element_type=F32)
        r = jnp.dot(orr, wpr_ref[:, cols], preferred_element_type=F32)
        merged = (jax.nn.sigmoid(ga_ref[:, cols].astype(F32)) * a
                  + jax.nn.sigmoid(gr_ref[:, cols].astype(F32)) * r)
        merged_ref[:, cols] = merged.astype(BF16)
    o_ref[...] = jnp.dot(merged_ref[...], wo_ref[...], preferred_element_type=F32)
    _post_residual(st, i, x_ref, gate_ref, npost_ref[1:2, :], o_ref, 1.0)


def _merge(st, x, mod, npost, o_a, o_r, wide, w_pa, w_pr, w_o):
    gate_a_block = 2 * RET_V // D_MODEL
    row_spec = pl.BlockSpec((st.tile, D_MODEL), lambda i, j: (i, 0))
    resident = lambda shape: pl.BlockSpec(shape, lambda i, j: (0, 0), pipeline_mode=pl.Buffered(1))
    return pl.pallas_call(
        functools.partial(_merge_kernel, st),
        out_shape=jax.ShapeDtypeStruct((st.n_tokens, D_MODEL), F32),
        grid=(st.n_tiles, 1),
        in_specs=[
            row_spec,
            _mod_spec(st, 5),
            pl.BlockSpec((N_SUBLAYERS, D_MODEL), lambda i, j: (0, 0)),
            pl.BlockSpec((st.tile, ATT_Q), lambda i, j: (i, 0)),
            pl.BlockSpec((st.tile, RET_V), lambda i, j: (i, 0)),
            pl.BlockSpec((st.tile, D_MODEL), lambda i, j: (i, gate_a_block)),
            pl.BlockSpec((st.tile, D_MODEL), lambda i, j: (i, gate_a_block + 1)),
            resident((ATT_Q, D_MODEL)), resident((RET_V, D_MODEL)), resident((D_MODEL, D_MODEL)),
        ],
        out_specs=row_spec,
        scratch_shapes=[pltpu.VMEM((st.tile, D_MODEL), BF16)],
        compiler_params=pltpu.CompilerParams(
            dimension_semantics=("parallel", "arbitrary"), vmem_limit_bytes=V7X_VMEM_LIMIT_BYTES),
        name="merge_out",
    )(x, mod, npost, o_a, o_r, wide, wide, w_pa, w_pr, w_o)


def _rotation_tables(pos):
    half = RET_DK // 2
    inv_freq = ROPE_BASE ** (-jnp.linspace(0.0, 1.0, half, dtype=F32))
    ang = pos[:, None] * inv_freq[None, :]
    cos, sin = jnp.cos(ang), jnp.sin(ang)
    return jnp.concatenate([cos, cos], axis=-1), jnp.concatenate([-sin, sin], axis=-1)


def kernel(x_prompt, x_sample, cache_k_win, cache_v_win, state_ret, c_prompt, c_sample, w_ada, b_ada,
           norm_pre, norm_post, w_in, attn_sinks, w_pa, w_pr, w_o,
           ffn1_gate, ffn1_up, ffn1_down, ffn2_gate, ffn2_up, ffn2_down):
    batch, seq, _ = x_prompt.shape
    n_seq, t_new, _ = x_sample.shape
    assert w_ada.shape[0] == 1, "single-layer step"
    assert t_new == V7X_SUBLANES and seq % TOKEN_TILE == 0 and (n_seq * t_new) % TOKEN_TILE == 0
    assert seq % MERGE_TOKEN_TILE == 0 and (n_seq * t_new) % MERGE_TOKEN_TILE == 0
    assert batch <= MOD_PAD_ROWS and n_seq % MOD_PAD_ROWS == 0

    c_all = jnp.concatenate(
        [c_sample, c_prompt, jnp.zeros((MOD_PAD_ROWS - batch, D_MODEL), F32)], axis=0)
    mod = _ada(c_all, w_ada[0], b_ada[0])

    def prompt_stream(tile):
        return _Stream(batch * seq, tile, seq, MOD_PAD_ROWS, n_seq // MOD_PAD_ROWS)

    def sample_stream(tile):
        return _Stream(n_seq * t_new, tile, t_new, tile // t_new, 0)

    prompt, prompt_m = prompt_stream(TOKEN_TILE), prompt_stream(MERGE_TOKEN_TILE)
    sample, sample_m = sample_stream(TOKEN_TILE), sample_stream(MERGE_TOKEN_TILE // 2)

    npre, npost = norm_pre[0], norm_post[0]
    sinks = attn_sinks[0]

    xs = x_sample.reshape(n_seq * t_new, D_MODEL)
    xs, f1g, f1u, f1d = _ffn(sample, 0, xs, mod, npre, npost,
                             ffn1_gate[0], ffn1_up[0], ffn1_down[0], emit_bf16=True)
    qa_s, kva_s, qkr_s, wide_s, w_in_b = _proj(sample, xs, mod, npre, w_in[0], F32, emit_bf16=True)

    xp = x_prompt.reshape(batch * seq, D_MODEL)
    xp = _ffn(prompt, 0, xp, mod, npre, npost, f1g, f1u, f1d)
    cos_s, sin_s = _rotation_tables(jnp.arange(t_new, dtype=F32) + PAST_LEN)
    qa, kva, qkr = _proj(prompt, xp, mod, npre, w_in_b, BF16, with_wide=False)
    wide, o_r_s, state_s = _proj_wide(
        prompt, xp, mod, npre, w_in_b, BF16, side=(qkr_s, wide_s, cos_s, sin_s, state_ret[0], t_new))
    o_a, w_pa_b, w_pr_b, w_o_b = _attn_prompt(qa, kva, sinks, batch, seq, (w_pa[0], w_pr[0], w_o[0]))
    cos_p, sin_p = _rotation_tables(jnp.arange(seq, dtype=F32))
    o_r, state_p = _ret_prompt(qkr, wide, cos_p, sin_p, batch, seq)
    xp = _merge(prompt_m, xp, mod, npost, o_a, o_r, wide, w_pa_b, w_pr_b, w_o_b)
    kva_p = kva.reshape(batch, seq, 2 * ATT_KV)[:, seq - WINDOW:]
    kv_shape = (1, batch, WINDOW, ATT_KV_HEADS, ATT_HEAD_DIM)
    k_win_p = kva_p[..., :ATT_KV].reshape(kv_shape)
    v_win_p = kva_p[..., ATT_KV:].reshape(kv_shape)

    sink_rows = jnp.repeat(sinks, t_new)[:, None]
    o_a_s, k_s, v_s = _attn_sample(
        qa_s, kva_s,
        cache_k_win[0].reshape(n_seq, WINDOW, ATT_KV), cache_v_win[0].reshape(n_seq, WINDOW, ATT_KV),
        sink_rows, n_seq, t_new)
    xs = _merge(sample_m, xs, mod, npost, o_a_s, o_r_s, wide_s, w_pa_b, w_pr_b, w_o_b)
    xs, f2g, f2u, f2d = _ffn(sample, 2, xs, mod, npre, npost,
                             ffn2_gate[0], ffn2_up[0], ffn2_down[0], emit_bf16=True)
    xp = _ffn(prompt, 2, xp, mod, npre, npost, f2g, f2u, f2d)
    kvs_shape = (1, n_seq, WINDOW, ATT_KV_HEADS, ATT_HEAD_DIM)

    return (xp.reshape(batch, seq, D_MODEL), xs.reshape(n_seq, t_new, D_MODEL),
            k_win_p, v_win_p, state_p[None],
            k_s.reshape(kvs_shape), v_s.reshape(kvs_shape), state_s[None])
```

```python
import functools
import math
from typing import NamedTuple

import jax
import jax.numpy as jnp
from jax import lax
from jax.experimental import pallas as pl
from jax.experimental.pallas import tpu as pltpu

F32 = jnp.float32
BF16 = jnp.bfloat16

D_MODEL = 2048
WINDOW = 128
ATT_HEADS = 16
ATT_KV_HEADS = 4
ATT_HEAD_DIM = 64
ATT_GROUP = ATT_HEADS // ATT_KV_HEADS
ATT_Q = ATT_HEADS * ATT_HEAD_DIM
ATT_KV = ATT_KV_HEADS * ATT_HEAD_DIM
RET_HEADS = 8
RET_DK = 128
RET_DV = 256
RET_CHUNK = 128
RET_QK = RET_HEADS * RET_DK
RET_V = RET_HEADS * RET_DV
ROPE_BASE = 10000.0
D_FF = 5632
NORM_EPS = 1e-6
N_SUBLAYERS = 3
PAST_LEN = 16384
D_IN = ATT_Q + 2 * ATT_KV + 2 * RET_QK + 2 * RET_V + 2 * D_MODEL

V7X_SUBLANES = 8
V7X_BF16_ROWS = 16
NORM_GROUP_ROWS = 128
V7X_VMEM_LIMIT_BYTES = 60 * 1024 * 1024

TOKEN_TILE = 1024
MERGE_TOKEN_TILE = 512
FF_SUBTILE = 256
FF_SUBTILES_BF16 = 2
F32_WEIGHT_K_PARTS = 4
COL_TILE = 512
ADA_COL_TILE = 1024
ADA_BUFFERS = 3
ATTN_BLOCKS_PER_STEP = 2
RET_CHUNKS_PER_STEP = 4
SAMPLE_SEQS_PER_STEP = 8
RET_SAMPLE_SEQS = 16
MOD_PAD_ROWS = 8

_QA_TILES = ATT_Q // COL_TILE
_KVA_TILES = 2 * ATT_KV // COL_TILE
_QKR_TILES = 2 * RET_QK // COL_TILE
_WIDE_COLS = 2 * RET_V + 2 * D_MODEL
_WIDE_TILES = _WIDE_COLS // COL_TILE
_QKR_START = _QA_TILES + _KVA_TILES
_WIDE_START = _QKR_START + _QKR_TILES
assert _WIDE_START + _WIDE_TILES == D_IN // COL_TILE and _KVA_TILES == 1

_ALIBI_SLOPES = [2.0 ** (-8.0 * (h + 1) / ATT_HEADS) for h in range(ATT_HEADS)]
_RET_LOG_G = [math.log(1.0 - 2.0 ** (-5.0 - h)) for h in range(RET_HEADS)]
_ATT_SCALE = ATT_HEAD_DIM ** -0.5
_RET_K_SCALE = RET_DK ** -0.5


class _Stream(NamedTuple):
    n_tokens: int
    tile: int
    rows_per_mod: int
    mod_block_rows: int
    mod_block_base: int

    @property
    def n_tiles(self):
        return self.n_tokens // self.tile

    @property
    def sub_rows(self):
        return min(self.rows_per_mod, self.tile)

    @property
    def group_rows(self):
        return max(self.sub_rows, NORM_GROUP_ROWS)

    @property
    def n_groups(self):
        return self.tile // self.group_rows

    @property
    def mods_per_group(self):
        return self.group_rows // self.sub_rows


def _rms(x):
    return x * lax.rsqrt(jnp.mean(x * x, axis=-1, keepdims=True) + NORM_EPS)


def _silu(x):
    return x * jax.nn.sigmoid(x)


def _for_groups(n_groups, fn):
    if n_groups == 1:
        fn(0)
    else:
        def body(g, carry):
            fn(g)
            return carry
        lax.fori_loop(0, n_groups, body, 0)


def _group_base(st, g):
    return 0 if st.n_groups == 1 else pl.multiple_of(g * st.group_rows, st.group_rows)


def _mod_row(st, i, g, s):
    if st.rows_per_mod >= st.tile:
        return (i * st.tile) // st.rows_per_mod
    return g * st.mods_per_group + s


def _pre_norm(st, i, x_ref, shift_ref, scale_ref, gain, h_ref):
    def group(g):
        base = _group_base(st, g)
        parts = []
        for s in range(st.mods_per_group):
            rows = pl.ds(base + s * st.sub_rows, st.sub_rows)
            m = _mod_row(st, i, g, s)
            x = x_ref[rows, :]
            row_gain = gain * (1.0 + scale_ref[pl.ds(m, 1), :])
            sh = shift_ref[pl.ds(m, 1), :]
            parts.append(_rms(x) * row_gain + sh)
        h = parts[0] if len(parts) == 1 else jnp.concatenate(parts, axis=0)
        h_ref[pl.ds(base, st.group_rows), :] = h.astype(BF16)
    _for_groups(st.n_groups, group)


def _post_residual(st, i, x_ref, gate_ref, gain, o_ref, coeff):
    def group(g):
        base = _group_base(st, g)
        results = []
        for s in range(st.mods_per_group):
            rows = pl.ds(base + s * st.sub_rows, st.sub_rows)
            m = _mod_row(st, i, g, s)
            row_gain = gate_ref[pl.ds(m, 1), :] * gain
            if coeff != 1.0:
                row_gain = coeff * row_gain
            results.append((rows, x_ref[rows, :] + _rms(o_ref[rows, :]) * row_gain))
        for rows, value in results:
            o_ref[rows, :] = value
    _for_groups(st.n_groups, group)


def _ada_kernel(c_ref, w_hbm_ref, b_ref, o_ref, w_buf_ref, sem_ref):
    j = pl.program_id(0)
    n_steps = pl.num_programs(0)

    def tile_copy(t):
        slot = t % ADA_BUFFERS
        start = t * ADA_COL_TILE
        if not isinstance(t, int):
            start = pl.multiple_of(start, ADA_COL_TILE)
        return pltpu.make_async_copy(w_hbm_ref.at[:, pl.ds(start, ADA_COL_TILE)],
                                     w_buf_ref.at[slot], sem_ref.at[slot])

    @pl.when(j == 0)
    def _():
        for t in range(ADA_BUFFERS - 1):
            tile_copy(t).start()

    @pl.when(j + ADA_BUFFERS - 1 < n_steps)
    def _():
        tile_copy(j + ADA_BUFFERS - 1).start()

    tile_copy(j).wait()
    a = _silu(c_ref[...]).astype(BF16)
    w = w_buf_ref[j % ADA_BUFFERS].astype(BF16)
    o_ref[0] = jnp.dot(a, w, preferred_element_type=F32) + b_ref[...]


def _ada(c_all, w_ada, b_ada):
    rows = c_all.shape[0]
    n_vec = N_SUBLAYERS * 3
    per_vec = D_MODEL // ADA_COL_TILE
    assert n_vec * per_vec >= ADA_BUFFERS
    return pl.pallas_call(
        _ada_kernel,
        out_shape=jax.ShapeDtypeStruct((n_vec, rows, D_MODEL), F32),
        grid=(n_vec * per_vec,),
        in_specs=[
            pl.BlockSpec((rows, D_MODEL), lambda j: (0, 0)),
            pl.BlockSpec(memory_space=pl.ANY),
            pl.BlockSpec((1, ADA_COL_TILE), lambda j: (0, j)),
        ],
        out_specs=pl.BlockSpec((1, rows, ADA_COL_TILE), lambda j: (j // per_vec, 0, j % per_vec)),
        scratch_shapes=[pltpu.VMEM((ADA_BUFFERS, D_MODEL, ADA_COL_TILE), F32),
                        pltpu.SemaphoreType.DMA((ADA_BUFFERS,))],
        compiler_params=pltpu.CompilerParams(
            dimension_semantics=("arbitrary",), vmem_limit_bytes=V7X_VMEM_LIMIT_BYTES),
        name="ada_mod",
    )(c_all, w_ada, b_ada.reshape(1, -1))


def _mod_spec(st, vec):
    if st.rows_per_mod >= st.tile:
        index = lambda i, j: (vec, st.mod_block_base, 0)
    else:
        index = lambda i, j: (vec, st.mod_block_base + i, 0)
    return pl.BlockSpec((None, st.mod_block_rows, D_MODEL), index)


def _ffn_kernel(st, sub, k_parts, emit_bf16, x_ref, shift_ref, scale_ref, gate_ref, npre_ref, npost_ref,
                *refs):
    i = pl.program_id(0)
    j = pl.program_id(1)
    n_up = max(k_parts, 1)
    wg_refs, wu_refs, wd_ref, o_ref = refs[:n_up], refs[n_up:2 * n_up], refs[2 * n_up], refs[2 * n_up + 1]
    rest = refs[2 * n_up + 2:]
    h_ref = rest[-1]

    def up_f32(h, w_refs, out_ref):
        rows = D_MODEL // k_parts
        acc = None
        for p, w_ref in enumerate(w_refs):
            w = w_ref[...].astype(BF16)
            if out_ref is not None:
                out_ref[p * rows:(p + 1) * rows, :] = w
            part = jnp.dot(h[:, p * rows:(p + 1) * rows], w, preferred_element_type=F32)
            acc = part if acc is None else acc + part
        return acc

    def partial_down():
        h = h_ref[...]
        if k_parts:
            wg_out_ref, wu_out_ref, wd_out_ref = rest[:3] if emit_bf16 else (None, None, None)
            wd = wd_ref[...].astype(BF16)
            if emit_bf16:
                wd_out_ref[...] = wd
            g = up_f32(h, wg_refs, wg_out_ref)
            u = up_f32(h, wu_refs, wu_out_ref)
            return jnp.dot((_silu(g) * u).astype(BF16), wd, preferred_element_type=F32)
        acts = []
        for t in range(wg_refs[0].shape[0]):
            g = jnp.dot(h, wg_refs[0][t], preferred_element_type=F32)
            u = jnp.dot(h, wu_refs[0][t], preferred_element_type=F32)
            acts.append((_silu(g) * u).astype(BF16))
        return jnp.dot(jnp.concatenate(acts, axis=1), wd_ref[...], preferred_element_type=F32)

    @pl.when(j == 0)
    def _():
        _pre_norm(st, i, x_ref, shift_ref, scale_ref, npre_ref[sub:sub + 1, :], h_ref)
        o_ref[...] = partial_down()

    last = pl.num_programs(1) - 1

    @pl.when((j > 0) & (j < last))
    def _():
        o_ref[...] += partial_down()

    @pl.when(j == last)
    def _():
        o_ref[...] += partial_down()
        _post_residual(st, i, x_ref, gate_ref, npost_ref[sub:sub + 1, :], o_ref, 0.5)


def _ffn(st, sub, x, mod, npre, npost, wg, wu, wd, emit_bf16=False):
    tiled = wg.ndim == 3
    n_sub = FF_SUBTILES_BF16 if tiled else 1
    ff_tile = n_sub * FF_SUBTILE
    row_spec = pl.BlockSpec((st.tile, D_MODEL), lambda i, j: (i, 0))
    full_spec = pl.BlockSpec((N_SUBLAYERS, D_MODEL), lambda i, j: (0, 0))
    tile_spec = pl.BlockSpec((None, D_MODEL, FF_SUBTILE), lambda i, j: (j, 0, 0))
    if tiled:
        k_parts = 0
        up_specs = [pl.BlockSpec((n_sub, D_MODEL, FF_SUBTILE), lambda i, j: (j, 0, 0))]
        up_args = lambda w: [w]
    else:
        k_parts = F32_WEIGHT_K_PARTS
        rows = D_MODEL // k_parts
        up_specs = [pl.BlockSpec((rows, ff_tile), functools.partial(lambda p, i, j: (p, j), p))
                    for p in range(k_parts)]
        up_args = lambda w: [w] * k_parts
    down_spec = pl.BlockSpec((ff_tile, D_MODEL), lambda i, j: (j, 0))
    out_shape = [jax.ShapeDtypeStruct((st.n_tokens, D_MODEL), F32)]
    out_specs = [row_spec]
    if emit_bf16:
        assert st.n_tiles == 1 and not tiled
        up_shape = jax.ShapeDtypeStruct((D_FF // FF_SUBTILE, D_MODEL, FF_SUBTILE), BF16)
        out_shape += [up_shape, up_shape, jax.ShapeDtypeStruct((D_FF, D_MODEL), BF16)]
        out_specs += [tile_spec, tile_spec, down_spec]
    outs = pl.pallas_call(
        functools.partial(_ffn_kernel, st, sub, k_parts, emit_bf16),
        out_shape=out_shape,
        grid=(st.n_tiles, D_FF // ff_tile),
        in_specs=[
            row_spec,
            _mod_spec(st, 3 * sub + 0), _mod_spec(st, 3 * sub + 1), _mod_spec(st, 3 * sub + 2),
            full_spec, full_spec,
            *up_specs, *up_specs, down_spec,
        ],
        out_specs=out_specs,
        scratch_shapes=[pltpu.VMEM((st.tile, D_MODEL), BF16)],
        compiler_params=pltpu.CompilerParams(
            dimension_semantics=("parallel", "arbitrary"), vmem_limit_bytes=V7X_VMEM_LIMIT_BYTES),
        name=f"ffn{sub}",
    )(x, mod, mod, mod, npre, npost, *up_args(wg), *up_args(wu), wd)
    return outs if emit_bf16 else outs[0]


_N_SIDE_IN = 9
_N_SIDE_OUT = 2


def _proj_kernel(st, k_parts, emit_bf16, with_wide, x_ref, shift_ref, scale_ref, npre_ref, *refs):
    i = pl.program_id(0)
    j = pl.program_id(1)
    w_refs = refs[:k_parts]
    refs = refs[k_parts:]
    n_main = 4 if with_wide else 3
    qa_ref, kva_ref, qkr_ref = refs[:3]
    wide_ref = refs[3] if with_wide else None
    rest = refs[n_main:]
    h_ref = rest[-1]
    rows = D_MODEL // k_parts

    def project(dst_ref):
        h = h_ref[...]
        acc = None
        for p, w_ref in enumerate(w_refs):
            w = w_ref[...].astype(BF16)
            if emit_bf16:
                rest[0][p * rows:(p + 1) * rows, :] = w
            part = jnp.dot(h[:, p * rows:(p + 1) * rows], w, preferred_element_type=F32)
            acc = part if acc is None else acc + part
        dst_ref[...] = acc.astype(dst_ref.dtype)

    @pl.when(j == 0)
    def _():
        _pre_norm(st, i, x_ref, shift_ref, scale_ref, npre_ref[1:2, :], h_ref)
        project(qa_ref)

    @pl.when((j > 0) & (j < _QA_TILES))
    def _():
        project(qa_ref)

    @pl.when(j == _QA_TILES)
    def _():
        project(kva_ref)

    @pl.when((j >= _QKR_START) & (j < _WIDE_START))
    def _():
        project(qkr_ref)

    if with_wide:
        @pl.when(j >= _WIDE_START)
        def _():
            project(wide_ref)


def _proj_wide_kernel(st, side_t_new, x_ref, shift_ref, scale_ref, npre_ref, w_lo_ref, w_hi_ref, *refs):
    i = pl.program_id(0)
    j = pl.program_id(1)
    side_in, wide_ref = refs[:_N_SIDE_IN], refs[_N_SIDE_IN]
    side_out, h_ref = refs[_N_SIDE_IN + 1:_N_SIDE_IN + 1 + _N_SIDE_OUT], refs[-1]

    def step():
        h = h_ref[...]
        y = jnp.concatenate([jnp.dot(h, w_lo_ref[...], preferred_element_type=F32),
                             jnp.dot(h, w_hi_ref[...], preferred_element_type=F32)], axis=1)
        wide_ref[...] = y.astype(wide_ref.dtype)
        log_g_ref, g_chunk_ref, *side_vmem = side_in
        _ret_sample_heads(side_t_new, j, 1, log_g_ref, g_chunk_ref, *side_vmem, *side_out)

    @pl.when(j == 0)
    def _():
        _pre_norm(st, i, x_ref, shift_ref, scale_ref, npre_ref[1:2, :], h_ref)
        step()

    @pl.when(j > 0)
    def _():
        step()


def _proj_wide(st, x, mod, npre, w_tiles, narrow_dtype, side):
    qkr_s, wide_s, cos2, sin2, state, t_new = side
    nb = RET_SAMPLE_SEQS
    n_rows = nb * t_new
    per = 2
    n_steps = _WIDE_TILES // per
    assert n_rows == RET_DK and state.shape[0] == st.n_tiles * nb and n_steps == RET_HEADS
    row_spec = pl.BlockSpec((st.tile, D_MODEL), lambda i, j: (i, 0))
    smem = pl.BlockSpec(memory_space=pltpu.SMEM)
    table_spec = pl.BlockSpec((n_rows, RET_DK), lambda i, j: (0, 0))
    state_spec = pl.BlockSpec((nb, 1, RET_DK, RET_DV), lambda i, j: (i, j, 0, 0))
    return pl.pallas_call(
        functools.partial(_proj_wide_kernel, st, t_new),
        out_shape=[jax.ShapeDtypeStruct((st.n_tokens, _WIDE_COLS), narrow_dtype),
                   jax.ShapeDtypeStruct((state.shape[0] * t_new, RET_V), F32),
                   jax.ShapeDtypeStruct(state.shape, F32)],
        grid=(st.n_tiles, n_steps),
        in_specs=[
            row_spec,
            _mod_spec(st, 3), _mod_spec(st, 4),
            pl.BlockSpec((N_SUBLAYERS, D_MODEL), lambda i, j: (0, 0)),
            pl.BlockSpec((None, D_MODEL, COL_TILE), lambda i, j: (_WIDE_START + per * j, 0, 0)),
            pl.BlockSpec((None, D_MODEL, COL_TILE), lambda i, j: (_WIDE_START + per * j + 1, 0, 0)),
            smem, smem,
            pl.BlockSpec((n_rows, RET_DK), lambda i, j: (i, j)),
            pl.BlockSpec((n_rows, RET_DK), lambda i, j: (i, RET_HEADS + j)),
            pl.BlockSpec((n_rows, RET_DV), lambda i, j: (i, j)),
            pl.BlockSpec((n_rows, RET_DV), lambda i, j: (i, RET_HEADS + j)),
            table_spec, table_spec,
            state_spec,
        ],
        out_specs=[pl.BlockSpec((st.tile, per * COL_TILE), lambda i, j: (i, j)),
                   pl.BlockSpec((n_rows, RET_DV), lambda i, j: (i, j)), state_spec],
        scratch_shapes=[pltpu.VMEM((st.tile, D_MODEL), BF16)],
        compiler_params=pltpu.CompilerParams(
            dimension_semantics=("parallel", "arbitrary"), vmem_limit_bytes=V7X_VMEM_LIMIT_BYTES),
        name="in_proj_wide",
    )(x, mod, mod, npre, w_tiles, w_tiles,
      jnp.asarray(_RET_LOG_G, F32), jnp.asarray([math.exp(t_new * g) for g in _RET_LOG_G], F32),
      qkr_s, qkr_s, wide_s, wide_s, jnp.tile(cos2, (nb, 1)), jnp.tile(sin2, (nb, 1)), state)


def _proj(st, x, mod, npre, w_in, narrow_dtype, emit_bf16=False, with_wide=True):
    n = st.n_tokens
    tiled = w_in.ndim == 3
    row_spec = pl.BlockSpec((st.tile, D_MODEL), lambda i, j: (i, 0))
    out_block = (st.tile, COL_TILE)
    tile_spec = pl.BlockSpec((None, D_MODEL, COL_TILE), lambda i, j: (j, 0, 0))
    out_shape = [
        jax.ShapeDtypeStruct((n, ATT_Q), narrow_dtype),
        jax.ShapeDtypeStruct((n, 2 * ATT_KV), F32),
        jax.ShapeDtypeStruct((n, 2 * RET_QK), F32),
        jax.ShapeDtypeStruct((n, _WIDE_COLS), narrow_dtype),
    ]
    out_specs = [
        pl.BlockSpec(out_block, lambda i, j: (i, jnp.minimum(j, _QA_TILES - 1))),
        pl.BlockSpec(out_block, lambda i, j: (i, 0)),
        pl.BlockSpec(out_block, lambda i, j: (i, jnp.clip(j - _QKR_START, 0, _QKR_TILES - 1))),
        pl.BlockSpec(out_block, lambda i, j: (i, jnp.maximum(j - _WIDE_START, 0))),
    ]
    if not with_wide:
        assert not emit_bf16
        out_shape, out_specs = out_shape[:3], out_specs[:3]
    if emit_bf16:
        assert st.n_tiles == 1 and not tiled
        out_shape.append(jax.ShapeDtypeStruct((D_IN // COL_TILE, D_MODEL, COL_TILE), BF16))
        out_specs.append(tile_spec)
    if tiled:
        k_parts = 1
        w_specs = [tile_spec]
    else:
        k_parts = F32_WEIGHT_K_PARTS
        w_specs = [pl.BlockSpec((D_MODEL // k_parts, COL_TILE), functools.partial(lambda p, i, j: (p, j), p))
                   for p in range(k_parts)]
    return pl.pallas_call(
        functools.partial(_proj_kernel, st, k_parts, emit_bf16, with_wide),
        out_shape=out_shape,
        grid=(st.n_tiles, D_IN // COL_TILE if with_wide else _WIDE_START),
        in_specs=[
            row_spec,
            _mod_spec(st, 3), _mod_spec(st, 4),
            pl.BlockSpec((N_SUBLAYERS, D_MODEL), lambda i, j: (0, 0)),
            *w_specs,
        ],
        out_specs=out_specs,
        scratch_shapes=[pltpu.VMEM((st.tile, D_MODEL), BF16)],
        compiler_params=pltpu.CompilerParams(
            dimension_semantics=("parallel", "arbitrary"), vmem_limit_bytes=V7X_VMEM_LIMIT_BYTES),
        name="in_proj",
    )(x, mod, mod, npre, *([w_in] * k_parts))


def _attention(q, k2, v2, sinks_ref, first_valid_key):
    tq = q.shape[0]
    a_idx = lax.broadcasted_iota(jnp.int32, (tq, 2 * WINDOW), 0)
    b_idx = lax.broadcasted_iota(jnp.int32, (tq, 2 * WINDOW), 1)
    dist = WINDOW + a_idx - b_idx
    mask = (dist >= 0) & (dist <= WINDOW) & (b_idx >= first_valid_key)
    dist_f = jnp.where(mask, dist.astype(F32), jnp.inf)
    q = q * _ATT_SCALE
    outs = []
    for kv in range(ATT_KV_HEADS):
        cols = slice(kv * ATT_HEAD_DIM, (kv + 1) * ATT_HEAD_DIM)
        kk = k2[:, cols]
        vv = v2[:, cols]
        heads = range(kv * ATT_GROUP, (kv + 1) * ATT_GROUP)
        qg = jnp.concatenate(
            [q[:, h * ATT_HEAD_DIM:(h + 1) * ATT_HEAD_DIM] for h in heads], axis=0).astype(BF16)
        s_all = lax.dot_general(qg, kk, (((1,), (1,)), ((), ())), preferred_element_type=F32)
        probs = []
        for g, h in enumerate(heads):
            s = s_all[g * tq:(g + 1) * tq] - _ALIBI_SLOPES[h] * dist_f
            sink = sinks_ref[h]
            m = jnp.maximum(jnp.max(s, axis=-1, keepdims=True), sink)
            p = jnp.exp(s - m)
            inv = 1.0 / (jnp.sum(p, axis=-1, keepdims=True) + jnp.exp(sink - m))
            probs.append((p * inv).astype(BF16))
        o_all = jnp.dot(jnp.concatenate(probs, axis=0), vv, preferred_element_type=F32)
        outs.extend(o_all[g * tq:(g + 1) * tq] for g in range(ATT_GROUP))
    return jnp.concatenate(outs, axis=-1)


def _attn_prompt_kernel(n_cast, sinks_ref, q_ref, kc_ref, vc_ref, kp_ref, vp_ref, *rest):
    cast_in, o_ref, cast_out = rest[:n_cast], rest[n_cast], rest[n_cast + 1:]
    for src_ref, dst_ref in zip(cast_in, cast_out):
        dst_ref[...] = src_ref[...].astype(BF16)
    step = pl.program_id(1)
    k_tiles = [kp_ref[...]] + [kc_ref[r * WINDOW:(r + 1) * WINDOW, :] for r in range(ATTN_BLOCKS_PER_STEP)]
    v_tiles = [vp_ref[...]] + [vc_ref[r * WINDOW:(r + 1) * WINDOW, :] for r in range(ATTN_BLOCKS_PER_STEP)]
    for r in range(ATTN_BLOCKS_PER_STEP):
        rows = slice(r * WINDOW, (r + 1) * WINDOW)
        k2 = jnp.concatenate(k_tiles[r:r + 2], axis=0).astype(BF16)
        v2 = jnp.concatenate(v_tiles[r:r + 2], axis=0).astype(BF16)
        first_valid = jnp.where(step == 0, WINDOW, 0) if r == 0 else 0
        o_ref[rows, :] = _attention(q_ref[rows, :], k2, v2, sinks_ref, first_valid).astype(o_ref.dtype)


def _attn_prompt(qa, kva, sinks, batch, seq, cast_weights):
    per = ATTN_BLOCKS_PER_STEP
    ns = seq // (per * WINDOW)
    n_steps = batch * ns
    cur = lambda col: (lambda b, i: (b * ns + i, col))
    prev = lambda col: (lambda b, i: (b * ns * per + jnp.maximum(i * per - 1, 0), col))
    cast_specs = []
    for w in cast_weights:
        slab = w.shape[0] // n_steps
        assert slab * n_steps == w.shape[0] and slab % V7X_BF16_ROWS == 0
        cast_specs.append(pl.BlockSpec((slab, w.shape[1]), lambda b, i: (b * ns + i, 0)))
    return pl.pallas_call(
        functools.partial(_attn_prompt_kernel, len(cast_weights)),
        out_shape=[jax.ShapeDtypeStruct((batch * seq, ATT_Q), BF16)]
        + [jax.ShapeDtypeStruct(w.shape, BF16) for w in cast_weights],
        grid=(batch, ns),
        in_specs=[
            pl.BlockSpec(memory_space=pltpu.SMEM),
            pl.BlockSpec((per * WINDOW, ATT_Q), lambda b, i: (b * ns + i, 0)),
            pl.BlockSpec((per * WINDOW, ATT_KV), cur(0)),
            pl.BlockSpec((per * WINDOW, ATT_KV), cur(1)),
            pl.BlockSpec((WINDOW, ATT_KV), prev(0)),
            pl.BlockSpec((WINDOW, ATT_KV), prev(1)),
            *cast_specs,
        ],
        out_specs=[pl.BlockSpec((per * WINDOW, ATT_Q), lambda b, i: (b * ns + i, 0)), *cast_specs],
        compiler_params=pltpu.CompilerParams(
            dimension_semantics=("parallel", "arbitrary"), vmem_limit_bytes=V7X_VMEM_LIMIT_BYTES),
        name="attn_prompt",
    )(sinks, qa, kva, kva, kva, kva, *cast_weights)


def _rotate(x, cos2, sin2):
    return x * cos2 + pltpu.roll(x, RET_DK // 2, axis=1) * sin2


def _ret_prompt_kernel(q_ref, k_ref, v_ref, gr_ref, cos_ref, sin_ref, o_ref, s_ref,
                       decay_ref, qw_ref, kw_ref):
    @pl.when(pl.program_id(1) == 0)
    def _():
        s_ref[...] = jnp.zeros_like(s_ref)
        row = lax.broadcasted_iota(jnp.int32, (RET_CHUNK, RET_CHUNK), 0).astype(F32)
        col = lax.broadcasted_iota(jnp.int32, (RET_CHUNK, RET_CHUNK), 1).astype(F32)
        diff = row - col
        for h in range(RET_HEADS):
            log_g = _RET_LOG_G[h]
            decay_ref[h] = jnp.where(diff >= 0, jnp.exp(jnp.maximum(diff, 0.0) * log_g), 0.0)
            qw_ref[h] = jnp.exp((row + 1.0) * log_g)
            kw_ref[h] = jnp.exp((RET_CHUNK - 1.0 - row) * log_g)

    for r in range(RET_CHUNKS_PER_STEP):
        rows = slice(r * RET_CHUNK, (r + 1) * RET_CHUNK)
        cos2, sin2 = cos_ref[rows, :], sin_ref[rows, :]
        for h in range(RET_HEADS):
            qk_cols = slice(h * RET_DK, (h + 1) * RET_DK)
            v_cols = slice(h * RET_DV, (h + 1) * RET_DV)
            qh = _rotate(q_ref[rows, qk_cols], cos2, sin2)
            kh = _rotate(k_ref[rows, qk_cols], cos2, sin2) * _RET_K_SCALE
            vb = v_ref[rows, v_cols].astype(BF16)
            s_prev = s_ref[0, h]
            scores = lax.dot_general(qh.astype(BF16), kh.astype(BF16), (((1,), (1,)), ((), ())),
                                     preferred_element_type=F32) * decay_ref[h]
            y = jnp.dot(scores.astype(BF16), vb, preferred_element_type=F32)
            y = y + jnp.dot((qh * qw_ref[h]).astype(BF16), s_prev.astype(BF16),
                            preferred_element_type=F32)
            kt = (kh * kw_ref[h]).T.astype(BF16)
            s_ref[0, h] = (math.exp(RET_CHUNK * _RET_LOG_G[h]) * s_prev
                           + jnp.dot(kt, vb, preferred_element_type=F32))
            gate = gr_ref[rows, v_cols].astype(F32)
            o_ref[rows, v_cols] = (_silu(gate) * _rms(y)).astype(o_ref.dtype)


def _ret_prompt(qkr, wide, cos2, sin2, batch, seq):
    step_rows = RET_CHUNKS_PER_STEP * RET_CHUNK
    nc = seq // step_rows
    rows = lambda col: (lambda b, c: (b * nc + c, col))
    return pl.pallas_call(
        _ret_prompt_kernel,
        out_shape=(
            jax.ShapeDtypeStruct((batch * seq, RET_V), BF16),
            jax.ShapeDtypeStruct((batch, RET_HEADS, RET_DK, RET_DV), F32),
        ),
        grid=(batch, nc),
        in_specs=[
            pl.BlockSpec((step_rows, RET_QK), rows(0)),
            pl.BlockSpec((step_rows, RET_QK), rows(1)),
            pl.BlockSpec((step_rows, RET_V), rows(0)),
            pl.BlockSpec((step_rows, RET_V), rows(1)),
            pl.BlockSpec((step_rows, RET_DK), lambda b, c: (c, 0)),
            pl.BlockSpec((step_rows, RET_DK), lambda b, c: (c, 0)),
        ],
        out_specs=(
            pl.BlockSpec((step_rows, RET_V), rows(0)),
            pl.BlockSpec((1, RET_HEADS, RET_DK, RET_DV), lambda b, c: (b, 0, 0, 0)),
        ),
        scratch_shapes=[pltpu.VMEM((RET_HEADS, RET_CHUNK, RET_CHUNK), F32)] * 3,
        compiler_params=pltpu.CompilerParams(
            dimension_semantics=("parallel", "arbitrary"), vmem_limit_bytes=V7X_VMEM_LIMIT_BYTES),
        name="ret_prompt",
    )(qkr, qkr, wide, wide, cos2, sin2)


def _attn_sample_kernel(t_new, sink_ref, qa_ref, kn_ref, vn_ref, ck_ref, cv_ref,
                        oa_ref, ko_ref, vo_ref):
    n_rows = ATT_HEADS * t_new
    pair = 2 * ATT_HEAD_DIM
    row = lax.broadcasted_iota(jnp.int32, (n_rows, 2 * WINDOW), 0)
    key = lax.broadcasted_iota(jnp.int32, (n_rows, 2 * WINDOW), 1)
    head = row // t_new
    dist = WINDOW + (row - head * t_new) - key
    slope = jnp.exp2((head.astype(F32) + 1.0) * (-8.0 / ATT_HEADS))
    bias = jnp.where((dist >= 0) & (dist <= WINDOW), -slope * dist.astype(F32), -jnp.inf)
    sink = sink_ref[...]
    lower = lax.broadcasted_iota(jnp.int32, (t_new, pair), 1) < ATT_HEAD_DIM
    zero_group = jnp.zeros((t_new, pair), F32)
    zero_keys = jnp.zeros((WINDOW - t_new, ATT_KV), F32)

    for n in range(SAMPLE_SEQS_PER_STEP):
        rows = slice(n * t_new, (n + 1) * t_new)
        kc, vc = ck_ref[n], cv_ref[n]
        kn, vn = kn_ref[rows, :], vn_ref[rows, :]
        ko_ref[n, :WINDOW - t_new, :] = kc[t_new:]
        ko_ref[n, WINDOW - t_new:, :] = kn
        vo_ref[n, :WINDOW - t_new, :] = vc[t_new:]
        vo_ref[n, WINDOW - t_new:, :] = vn
        k2 = jnp.concatenate([kc, kn, zero_keys], axis=0).astype(BF16)
        v2 = jnp.concatenate([vc, vn, zero_keys], axis=0).astype(BF16)

        q = qa_ref[rows, :] * _ATT_SCALE
        q_swapped = pltpu.roll(q, ATT_HEAD_DIM, axis=1)
        blocks = []
        for h in range(ATT_HEADS):
            kv = h // ATT_GROUP
            want_lower = kv % 2 == 0
            if (h % 2 == 0) == want_lower:
                src = q[:, (h // 2) * pair:(h // 2 + 1) * pair]
            else:
                g = (h + 1) // 2 % (ATT_HEADS // 2)
                src = q_swapped[:, g * pair:(g + 1) * pair]
            piece = jnp.where(lower if want_lower else ~lower, src, 0.0)
            blocks.append(jnp.concatenate(
                [piece, zero_group] if kv // 2 == 0 else [zero_group, piece], axis=1))
        q_bd = jnp.concatenate(blocks, axis=0).astype(BF16)

        s = lax.dot_general(q_bd, k2, (((1,), (1,)), ((), ())), preferred_element_type=F32) + bias
        m = jnp.maximum(jnp.max(s, axis=-1, keepdims=True), sink)
        p = jnp.exp(s - m)
        inv = 1.0 / (jnp.sum(p, axis=-1, keepdims=True) + jnp.exp(sink - m))
        o = jnp.dot((p * inv).astype(BF16), v2, preferred_element_type=F32)

        outs = []
        for g in range(ATT_HEADS // 2):
            h0 = 2 * g
            kv = h0 // ATT_GROUP
            cols = slice((kv // 2) * pair, (kv // 2 + 1) * pair)
            a = o[h0 * t_new:(h0 + 1) * t_new, cols]
            b = o[(h0 + 1) * t_new:(h0 + 2) * t_new, cols]
            if kv % 2 == 0:
                b = pltpu.roll(b, ATT_HEAD_DIM, axis=1)
            else:
                a = pltpu.roll(a, ATT_HEAD_DIM, axis=1)
            outs.append(jnp.where(lower, a, b))
        oa_ref[rows, :] = jnp.concatenate(outs, axis=1)


def _attn_sample(qa, kva, cache_k, cache_v, sink_rows, n_seq, t_new):
    nb = SAMPLE_SEQS_PER_STEP
    rows = lambda col: (lambda n: (n, col))
    cache_spec = pl.BlockSpec((nb, WINDOW, ATT_KV), lambda n: (n, 0, 0))
    return pl.pallas_call(
        functools.partial(_attn_sample_kernel, t_new),
        out_shape=(
            jax.ShapeDtypeStruct((n_seq * t_new, ATT_Q), F32),
            jax.ShapeDtypeStruct(cache_k.shape, F32),
            jax.ShapeDtypeStruct(cache_v.shape, F32),
        ),
        grid=(n_seq // nb,),
        in_specs=[
            pl.BlockSpec((ATT_HEADS * t_new, 1), lambda n: (0, 0)),
            pl.BlockSpec((nb * t_new, ATT_Q), rows(0)),
            pl.BlockSpec((nb * t_new, ATT_KV), rows(0)),
            pl.BlockSpec((nb * t_new, ATT_KV), rows(1)),
            cache_spec, cache_spec,
        ],
        out_specs=(pl.BlockSpec((nb * t_new, ATT_Q), rows(0)), cache_spec, cache_spec),
        compiler_params=pltpu.CompilerParams(
            dimension_semantics=("parallel",), vmem_limit_bytes=V7X_VMEM_LIMIT_BYTES),
        name="attn_sample",
    )(sink_rows, qa, kva, kva, cache_k, cache_v)


def _ret_sample_heads(t_new, first_head, n_heads, log_g_ref, g_chunk_ref, q_ref, k_ref, v_ref, gr_ref,
                      cos_ref, sin_ref, s_in_ref, o_ref, s_out_ref):
    n_seq = s_in_ref.shape[0]
    n_rows = n_seq * t_new
    cos2, sin2 = cos_ref[...], sin_ref[...]
    row = lax.broadcasted_iota(jnp.int32, (n_rows, n_rows), 0)
    col = lax.broadcasted_iota(jnp.int32, (n_rows, n_rows), 1)
    row_seq, col_seq = row // t_new, col // t_new
    same_chunk_causal = (row_seq == col_seq) & (row >= col)
    diff = jnp.maximum((row - col).astype(F32), 0.0)
    t = (row - row_seq * t_new).astype(F32)

    for hh in range(n_heads):
        h = first_head + hh
        log_g = log_g_ref[h]
        g_chunk = g_chunk_ref[h]
        qk_cols = slice(hh * RET_DK, (hh + 1) * RET_DK)
        v_cols = slice(hh * RET_DV, (hh + 1) * RET_DV)
        q = _rotate(q_ref[:, qk_cols], cos2, sin2)
        k = _rotate(k_ref[:, qk_cols], cos2, sin2) * _RET_K_SCALE
        vb = v_ref[:, v_cols].astype(BF16)

        decay = jnp.where(same_chunk_causal, jnp.exp(diff * log_g), 0.0)
        scores = lax.dot_general(q.astype(BF16), k.astype(BF16), (((1,), (1,)), ((), ())),
                                 preferred_element_type=F32) * decay
        y = jnp.dot(scores.astype(BF16), vb, preferred_element_type=F32)

        qw = q * jnp.exp((t + 1.0) * log_g)
        y_cross = [jnp.dot(qw[n * t_new:(n + 1) * t_new].astype(BF16), s_in_ref[n, hh].astype(BF16),
                           preferred_element_type=F32) for n in range(n_seq)]
        y = y + jnp.concatenate(y_cross, axis=0)

        kt = (k * jnp.exp((t_new - 1.0 - t) * log_g)).T
        lhs = jnp.concatenate([jnp.where(col_seq == n, kt, 0.0) for n in range(n_seq)], axis=0)
        kv = jnp.dot(lhs.astype(BF16), vb, preferred_element_type=F32)
        for n in range(n_seq):
            s_out_ref[n, hh] = g_chunk * s_in_ref[n, hh] + kv[n * RET_DK:(n + 1) * RET_DK]

        o_ref[:, v_cols] = _silu(gr_ref[:, v_cols]) * _rms(y)


def _merge_kernel(st, x_ref, gate_ref, npost_ref, oa_ref, or_ref, ga_ref, gr_ref,
                  wpa_ref, wpr_ref, wo_ref, o_ref, merged_ref):
    i = pl.program_id(0)
    oa = oa_ref[...].astype(BF16)
    orr = or_ref[...].astype(BF16)
    for c in range(D_MODEL // COL_TILE):
        cols = slice(c * COL_TILE, (c + 1) * COL_TILE)
        a = jnp.dot(oa, wpa_ref[:, cols], preferred_element_type=F32)
        r = jnp.dot(orr, wpr_ref[:, cols], preferred_element_type=F32)
        merged = (jax.nn.sigmoid(ga_ref[:, cols].astype(F32)) * a
                  + jax.nn.sigmoid(gr_ref[:, cols].astype(F32)) * r)
        merged_ref[:, cols] = merged.astype(BF16)
    o_ref[...] = jnp.dot(merged_ref[...], wo_ref[...], preferred_element_type=F32)
    _post_residual(st, i, x_ref, gate_ref, npost_ref[1:2, :], o_ref, 1.0)


def _merge(st, x, mod, npost, o_a, o_r, wide, w_pa, w_pr, w_o):
    gate_a_block = 2 * RET_V // D_MODEL
    row_spec = pl.BlockSpec((st.tile, D_MODEL), lambda i, j: (i, 0))
    resident = lambda shape: pl.BlockSpec(shape, lambda i, j: (0, 0), pipeline_mode=pl.Buffered(1))
    return pl.pallas_call(
        functools.partial(_merge_kernel, st),
        out_shape=jax.ShapeDtypeStruct((st.n_tokens, D_MODEL), F32),
        grid=(st.n_tiles, 1),
        in_specs=[
            row_spec,
            _mod_spec(st, 5),
            pl.BlockSpec((N_SUBLAYERS, D_MODEL), lambda i, j: (0, 0)),
            pl.BlockSpec((st.tile, ATT_Q), lambda i, j: (i, 0)),
            pl.BlockSpec((st.tile, RET_V), lambda i, j: (i, 0)),
            pl.BlockSpec((st.tile, D_MODEL), lambda i, j: (i, gate_a_block)),
            pl.BlockSpec((st.tile, D_MODEL), lambda i, j: (i, gate_a_block + 1)),
            resident((ATT_Q, D_MODEL)), resident((RET_V, D_MODEL)), resident((D_MODEL, D_MODEL)),
        ],
        out_specs=row_spec,
        scratch_shapes=[pltpu.VMEM((st.tile, D_MODEL), BF16)],
        compiler_params=pltpu.CompilerParams(
            dimension_semantics=("parallel", "arbitrary"), vmem_limit_bytes=V7X_VMEM_LIMIT_BYTES),
        name="merge_out",
    )(x, mod, npost, o_a, o_r, wide, wide, w_pa, w_pr, w_o)


def _rotation_tables(pos):
    half = RET_DK // 2
    inv_freq = ROPE_BASE ** (-jnp.linspace(0.0, 1.0, half, dtype=F32))
    ang = pos[:, None] * inv_freq[None, :]
    cos, sin = jnp.cos(ang), jnp.sin(ang)
    return jnp.concatenate([cos, cos], axis=-1), jnp.concatenate([-sin, sin], axis=-1)


def kernel(x_prompt, x_sample, cache_k_win, cache_v_win, state_ret, c_prompt, c_sample, w_ada, b_ada,
           norm_pre, norm_post, w_in, attn_sinks, w_pa, w_pr, w_o,
           ffn1_gate, ffn1_up, ffn1_down, ffn2_gate, ffn2_up, ffn2_down):
    batch, seq, _ = x_prompt.shape
    n_seq, t_new, _ = x_sample.shape
    assert w_ada.shape[0] == 1, "single-layer step"
    assert t_new == V7X_SUBLANES and seq % TOKEN_TILE == 0 and (n_seq * t_new) % TOKEN_TILE == 0
    assert seq % MERGE_TOKEN_TILE == 0 and (n_seq * t_new) % MERGE_TOKEN_TILE == 0
    assert batch <= MOD_PAD_ROWS and n_seq % MOD_PAD_ROWS == 0

    c_all = jnp.concatenate(
        [c_sample, c_prompt, jnp.zeros((MOD_PAD_ROWS - batch, D_MODEL), F32)], axis=0)
    mod = _ada(c_all, w_ada[0], b_ada[0])

    def prompt_stream(tile):
        return _Stream(batch * seq, tile, seq, MOD_PAD_ROWS, n_seq // MOD_PAD_ROWS)

    def sample_stream(tile):
        return _Stream(n_seq * t_new, tile, t_new, tile // t_new, 0)

    prompt, prompt_m = prompt_stream(TOKEN_TILE), prompt_stream(MERGE_TOKEN_TILE)
    sample, sample_m = sample_stream(TOKEN_TILE), sample_stream(MERGE_TOKEN_TILE // 2)

    npre, npost = norm_pre[0], norm_post[0]
    sinks = attn_sinks[0]

    xs = x_sample.reshape(n_seq * t_new, D_MODEL)
    xs, f1g, f1u, f1d = _ffn(sample, 0, xs, mod, npre, npost,
                             ffn1_gate[0], ffn1_up[0], ffn1_down[0], emit_bf16=True)
    qa_s, kva_s, qkr_s, wide_s, w_in_b = _proj(sample, xs, mod, npre, w_in[0], F32, emit_bf16=True)

    xp = x_prompt.reshape(batch * seq, D_MODEL)
    xp = _ffn(prompt, 0, xp, mod, npre, npost, f1g, f1u, f1d)
    cos_s, sin_s = _rotation_tables(jnp.arange(t_new, dtype=F32) + PAST_LEN)
    qa, kva, qkr = _proj(prompt, xp, mod, npre, w_in_b, BF16, with_wide=False)
    wide, o_r_s, state_s = _proj_wide(
        prompt, xp, mod, npre, w_in_b, BF16, side=(qkr_s, wide_s, cos_s, sin_s, state_ret[0], t_new))
    o_a, w_pa_b, w_pr_b, w_o_b = _attn_prompt(qa, kva, sinks, batch, seq, (w_pa[0], w_pr[0], w_o[0]))
    cos_p, sin_p = _rotation_tables(jnp.arange(seq, dtype=F32))
    o_r, state_p = _ret_prompt(qkr, wide, cos_p, sin_p, batch, seq)
    xp = _merge(prompt_m, xp, mod, npost, o_a, o_r, wide, w_pa_b, w_pr_b, w_o_b)
    kva_p = kva.reshape(batch, seq, 2 * ATT_KV)[:, seq - WINDOW:]
    kv_shape = (1, batch, WINDOW, ATT_KV_HEADS, ATT_HEAD_DIM)
    k_win_p = kva_p[..., :ATT_KV].reshape(kv_shape)
    v_win_p = kva_p[..., ATT_KV:].reshape(kv_shape)

    sink_rows = jnp.repeat(sinks, t_new)[:, None]
    o_a_s, k_s, v_s = _attn_sample(
        qa_s, kva_s,
        cache_k_win[0].reshape(n_seq, WINDOW, ATT_KV), cache_v_win[0].reshape(n_seq, WINDOW, ATT_KV),
        sink_rows, n_seq, t_new)
    xs = _merge(sample_m, xs, mod, npost, o_a_s, o_r_s, wide_s, w_pa_b, w_pr_b, w_o_b)
    xs, f2g, f2u, f2d = _ffn(sample, 2, xs, mod, npre, npost,
                             ffn2_gate[0], ffn2_up[0], ffn2_down[0], emit_bf16=True)
    xp = _ffn(prompt, 2, xp, mod, npre, npost, f2g, f2u, f2d)
    kvs_shape = (1, n_seq, WINDOW, ATT_KV_HEADS, ATT_HEAD_DIM)

    return (xp.reshape(batch, seq, D_MODEL), xs.reshape(n_seq, t_new, D_MODEL),
            k_win_p, v_win_p, state_p[None],
            k_s.reshape(kvs_shape), v_s.reshape(kvs_shape), state_s[None])
```
